```python
import math
import jax, jax.numpy as jnp
from jax import lax
import numpy as np

D_MODEL = 4096
BATCH = 4
SEQ = 4096
DEPTH = 1

CHUNK = 64
Q_BLOCK = 128
PLE_DIM = 256
NORM_EPS = 1e-6

RWKV_WIDTH = D_MODEL // 2
RWKV_HEAD_DIM = 64
RWKV_HEADS = RWKV_WIDTH // RWKV_HEAD_DIM
DECAY_LORA = max(32, int(round(1.8 * RWKV_WIDTH ** 0.5 / 32)) * 32)
AAA_LORA = max(32, int(round(2.5 * RWKV_WIDTH ** 0.5 / 32)) * 32)
GATE_LORA = max(32, int(round(0.6 * RWKV_WIDTH ** 0.8 / 32)) * 32)
RWKV_GN_EPS = 64e-5
RWKV_COLS = 3 * RWKV_WIDTH + DECAY_LORA + AAA_LORA + GATE_LORA
RWKV_SPLITS = [RWKV_WIDTH, 2 * RWKV_WIDTH, 3 * RWKV_WIDTH,
               3 * RWKV_WIDTH + DECAY_LORA, 3 * RWKV_WIDTH + DECAY_LORA + AAA_LORA]

DIFF_WIDTH = D_MODEL // 2
DIFF_HEAD_DIM = 64
DIFF_HEADS = DIFF_WIDTH // (2 * DIFF_HEAD_DIM)
DIFF_COLS = 3 * DIFF_WIDTH

GATE_COLS = 2 * D_MODEL
IN_COLS = RWKV_COLS + DIFF_COLS + GATE_COLS

D_FF = int(round(8 * D_MODEL / 3 / 256)) * 256
CONV_WIDTH = 3

kernel_name = 'hybrid_rwkv7_diffattn_convffn_block'


def rms_norm(x, g, eps=NORM_EPS):
    xf = x.astype(jnp.float32)
    y = xf * lax.rsqrt(jnp.mean(xf * xf, axis=-1, keepdims=True) + eps)
    return (y * g.astype(jnp.float32)).astype(x.dtype)


def token_shift(z):
    return jnp.pad(z, ((0, 0), (1, 0), (0, 0)))[:, :-1]


def causal_depthwise_conv(u, w, b):
    k_width, seq = w.shape[0], u.shape[1]
    up = jnp.pad(u, ((0, 0), (k_width - 1, 0), (0, 0)))
    out = b
    for j in range(k_width):
        out = out + up[:, j:j + seq] * w[j]
    return out


def rwkv7_time_mix(z, mu, w0, w2, a0, a2, g2, k_k, k_a, r_k, ln_w, ln_b):
    out_dtype = z.dtype
    f32 = jnp.float32
    bsz, seq, _ = z.shape
    z = z.astype(f32)
    z = z + (token_shift(z) - z) * mu.astype(f32)
    r, k, v, w_lo, a_lo, g_lo = jnp.split(z, RWKV_SPLITS, axis=-1)
    log_w = -jax.nn.softplus(-(w0.astype(f32) + jnp.tanh(w_lo) @ w2.astype(f32))) - 0.5
    decay = jnp.exp(-jnp.exp(log_w))
    a = jax.nn.sigmoid(a0.astype(f32) + a_lo @ a2.astype(f32))
    g = jax.nn.sigmoid(g_lo) @ g2.astype(f32)

    def heads(t):
        return t.reshape(bsz, seq, RWKV_HEADS, RWKV_HEAD_DIM)

    kk = heads(k * k_k.astype(f32))
    kk = kk / jnp.maximum(jnp.sqrt(jnp.sum(kk * kk, axis=-1, keepdims=True)), 1e-12)
    k = k * (1.0 + (a - 1.0) * k_a.astype(f32))
    rh, kh, vh, wh, ah = heads(r), heads(k), heads(v), heads(decay), heads(a)
    bh = kk * ah
    xs = tuple(jnp.moveaxis(t, 1, 0) for t in (rh, wh, kh, vh, kk, bh))

    def step(state, inp):
        r_t, w_t, k_t, v_t, kk_t, b_t = inp
        s_kk = jnp.einsum('bhvk,bhk->bhv', state, kk_t)
        state = (state * w_t[:, :, None, :]
                 - s_kk[..., None] * b_t[:, :, None, :]
                 + v_t[..., None] * k_t[:, :, None, :])
        return state, jnp.einsum('bhvk,bhk->bhv', state, r_t)

    s0 = jnp.zeros((bsz, RWKV_HEADS, RWKV_HEAD_DIM, RWKV_HEAD_DIM), f32)
    _, o = lax.scan(step, s0, xs)
    o = jnp.moveaxis(o, 0, 1)
    mean = jnp.mean(o, axis=-1, keepdims=True)
    var = jnp.mean(jnp.square(o - mean), axis=-1, keepdims=True)
    o = (o - mean) * lax.rsqrt(var + RWKV_GN_EPS)
    o = o.reshape(bsz, seq, RWKV_WIDTH) * ln_w.astype(f32) + ln_b.astype(f32)
    bonus = jnp.sum(rh * kh * r_k.astype(f32), axis=-1, keepdims=True) * vh
    o = o + bonus.reshape(bsz, seq, RWKV_WIDTH)
    return (o * g).astype(out_dtype)


def diff_attention(z, q_g, k_g, lam_q1, lam_k1, lam_q2, lam_k2, subln_g, lambda_init):
    out_dtype = z.dtype
    f32 = jnp.float32
    bsz, seq, _ = z.shape
    q, k, v = jnp.split(z.astype(f32), [DIFF_WIDTH, 2 * DIFF_WIDTH], axis=-1)
    q = rms_norm(q.reshape(bsz, seq, DIFF_HEADS, 2, DIFF_HEAD_DIM), q_g)
    k = rms_norm(k.reshape(bsz, seq, DIFF_HEADS, 2, DIFF_HEAD_DIM), k_g)
    v = v.reshape(bsz, seq, DIFF_HEADS, 2 * DIFF_HEAD_DIM)
    lam = (jnp.exp(jnp.sum(lam_q1.astype(f32) * lam_k1.astype(f32)))
           - jnp.exp(jnp.sum(lam_q2.astype(f32) * lam_k2.astype(f32))) + lambda_init)
    scale = DIFF_HEAD_DIM ** -0.5
    n_blocks = seq // Q_BLOCK
    q_blocks = jnp.moveaxis(q.reshape(bsz, n_blocks, Q_BLOCK, DIFF_HEADS, 2, DIFF_HEAD_DIM), 1, 0)
    key_chunk = jnp.arange(seq) // CHUNK

    def block(args):
        q_blk, bi = args
        q_chunk = (bi * Q_BLOCK + jnp.arange(Q_BLOCK)) // CHUNK
        mask = key_chunk[None, :] <= q_chunk[:, None]
        s = jnp.einsum('bqhcd,bkhcd->bhcqk', q_blk, k) * scale
        s = jnp.where(mask, s, -jnp.inf)
        pr = jax.nn.softmax(s, axis=-1)
        diff = pr[:, :, 0] - lam * pr[:, :, 1]
        return jnp.einsum('bhqk,bkhe->bqhe', diff, v)

    o = lax.map(block, (q_blocks, jnp.arange(n_blocks)))
    o = jnp.moveaxis(o, 0, 1).reshape(bsz, seq, DIFF_HEADS, 2 * DIFF_HEAD_DIM)
    o = rms_norm(o, subln_g) * (1.0 - lambda_init)
    return o.reshape(bsz, seq, DIFF_WIDTH).astype(out_dtype)


def conv_ffn(h, w_in, conv_w, conv_b, w_out):
    u = h @ w_in
    u = causal_depthwise_conv(u, conv_w, conv_b)
    gate, up = jnp.split(u, 2, axis=-1)
    return (jax.nn.silu(gate) * up) @ w_out


def setup_inputs(seed: int = 0) -> dict:
    key = jax.random.key(seed)
    ks = iter(jax.random.split(key, 40))
    f32 = jnp.float32
    L = DEPTH

    def nrm(shape, scale):
        return scale * jax.random.normal(next(ks), shape, f32)

    def uni(shape, lo, hi):
        return jax.random.uniform(next(ks), shape, f32, lo, hi)

    def gain(shape):
        return 1.0 + nrm(shape, 0.02)

    return {
        'x': nrm((BATCH, SEQ, D_MODEL), 1.0),
        'p': nrm((DEPTH, BATCH, SEQ, PLE_DIM), 1.0),
        'norm_mix_g': gain((L, D_MODEL)),
        'w_in': nrm((L, D_MODEL, IN_COLS), D_MODEL ** -0.5),
        'rwkv_mu': uni((L, RWKV_COLS), 0.0, 1.0),
        'rwkv_w0': uni((L, RWKV_WIDTH), -4.0, 0.0),
        'rwkv_w2': nrm((L, DECAY_LORA, RWKV_WIDTH), 0.5 * DECAY_LORA ** -0.5),
        'rwkv_a0': nrm((L, RWKV_WIDTH), 0.1),
        'rwkv_a2': nrm((L, AAA_LORA, RWKV_WIDTH), 0.5 * AAA_LORA ** -0.5),
        'rwkv_g2': nrm((L, GATE_LORA, RWKV_WIDTH), GATE_LORA ** -0.5),
        'rwkv_k_k': 0.85 + nrm((L, RWKV_WIDTH), 0.05),
        'rwkv_k_a': 1.0 + nrm((L, RWKV_WIDTH), 0.05),
        'rwkv_r_k': nrm((L, RWKV_HEADS, RWKV_HEAD_DIM), 0.1),
        'rwkv_ln_w': gain((L, RWKV_WIDTH)),
        'rwkv_ln_b': nrm((L, RWKV_WIDTH), 0.02),
        'q_norm_g': gain((L, DIFF_HEAD_DIM)),
        'k_norm_g': gain((L, DIFF_HEAD_DIM)),
        'lam_q1': nrm((L, DIFF_HEAD_DIM), 0.1),
        'lam_k1': nrm((L, DIFF_HEAD_DIM), 0.1),
        'lam_q2': nrm((L, DIFF_HEAD_DIM), 0.1),
        'lam_k2': nrm((L, DIFF_HEAD_DIM), 0.1),
        'subln_g': gain((L, 2 * DIFF_HEAD_DIM)),
        'w_branch_a': nrm((L, RWKV_WIDTH, D_MODEL), RWKV_WIDTH ** -0.5),
        'w_branch_b': nrm((L, DIFF_WIDTH, D_MODEL), DIFF_WIDTH ** -0.5),
        'w_out': nrm((L, D_MODEL, D_MODEL), D_MODEL ** -0.5),
        'norm_ffn_g': gain((L, D_MODEL)),
        'w_ffn_in': nrm((L, D_MODEL, 2 * D_FF), D_MODEL ** -0.5),
        'ffn_conv_w': nrm((L, CONV_WIDTH, 2 * D_FF), CONV_WIDTH ** -0.5),
        'ffn_conv_b': nrm((L, 2 * D_FF), 0.02),
        'w_ffn_out': nrm((L, D_FF, D_MODEL), D_FF ** -0.5),
        'norm_ple_g': gain((L, D_MODEL)),
        'w_ple_gate': nrm((L, D_MODEL, D_MODEL), D_MODEL ** -0.5),
        'w_ple_proj': nrm((L, PLE_DIM, D_MODEL), PLE_DIM ** -0.5),
    }


def reference(x, p, norm_mix_g, w_in, rwkv_mu, rwkv_w0, rwkv_w2, rwkv_a0, rwkv_a2, rwkv_g2,
              rwkv_k_k, rwkv_k_a, rwkv_r_k, rwkv_ln_w, rwkv_ln_b, q_norm_g, k_norm_g,
              lam_q1, lam_k1, lam_q2, lam_k2, subln_g, w_branch_a, w_branch_b, w_out,
              norm_ffn_g, w_ffn_in, ffn_conv_w, ffn_conv_b, w_ffn_out,
              norm_ple_g, w_ple_gate, w_ple_proj):
    for i in range(DEPTH):
        lambda_init = 0.8 - 0.6 * math.exp(-0.3 * i)
        h = rms_norm(x, norm_mix_g[i])
        z = h @ w_in[i]
        z_a, z_b, z_g = jnp.split(z, [RWKV_COLS, RWKV_COLS + DIFF_COLS], axis=-1)
        o_a = rwkv7_time_mix(z_a, rwkv_mu[i], rwkv_w0[i], rwkv_w2[i], rwkv_a0[i], rwkv_a2[i],
                             rwkv_g2[i], rwkv_k_k[i], rwkv_k_a[i], rwkv_r_k[i],
                             rwkv_ln_w[i], rwkv_ln_b[i])
        o_b = diff_attention(z_b, q_norm_g[i], k_norm_g[i], lam_q1[i], lam_k1[i],
                             lam_q2[i], lam_k2[i], subln_g[i], lambda_init)
        g_a, g_b = jnp.split(jax.nn.sigmoid(z_g), 2, axis=-1)
        merged = g_a * (o_a @ w_branch_a[i]) + g_b * (o_b @ w_branch_b[i])
        x = x + merged @ w_out[i]
        x = x + conv_ffn(rms_norm(x, norm_ffn_g[i]), w_ffn_in[i], ffn_conv_w[i],
                         ffn_conv_b[i], w_ffn_out[i])
        ple_gate = jax.nn.sigmoid(rms_norm(x, norm_ple_g[i]) @ w_ple_gate[i])
        x = x + ple_gate * (p[i] @ w_ple_proj[i])
    return x
```

```python
import functools
import math

import jax
import jax.numpy as jnp
import numpy as np
from jax import lax
from jax.experimental import pallas as pl
from jax.experimental.pallas import tpu as pltpu

F32 = jnp.float32
BF16 = jnp.bfloat16

NORM_EPS = 1e-6
RWKV_GN_EPS = 64e-5
HEAD = 64
CHUNK = 64
GROUP_W = 256
VMEM_LIMIT = 56 * 1024 * 1024
TILES = dict(tm=1024, tn=512, rwkv_rows=128, attn=512, prep_rows=512, norm_rows=256, ffn_out_tn=1024)


def _cparams(sem):
    return pltpu.CompilerParams(dimension_semantics=sem, vmem_limit_bytes=VMEM_LIMIT)


def _dot(a, b, dims):
    return lax.dot_general(a, b, (dims, ((), ())), preferred_element_type=F32)


_NN = ((1,), (0,))
_NT = ((1,), (1,))
_TN = ((0,), (0,))


def _split(x):
    hi = x.astype(BF16)
    lo = (x - hi.astype(F32)).astype(BF16)
    return hi, lo


def _dot3(a, b, dims):
    ah, al = _split(a)
    bh, bl = _split(b)
    return _dot(ah, bh, dims) + (_dot(ah, bl, dims) + _dot(al, bh, dims))


def _dot2_exact_rhs(a, b_bf16, dims):
    ah, al = _split(a)
    return _dot(ah, b_bf16, dims) + _dot(al, b_bf16, dims)


def _rmsnorm_kernel(x_ref, g_ref, o_ref):
    x = x_ref[...]
    ms = jnp.mean(x * x, axis=-1, keepdims=True)
    o_ref[...] = (x * lax.rsqrt(ms + NORM_EPS) * g_ref[...]).astype(o_ref.dtype)


def _rmsnorm(x2d, g, tr=256):
    t, d = x2d.shape
    return pl.pallas_call(
        _rmsnorm_kernel,
        out_shape=jax.ShapeDtypeStruct((t, d), BF16),
        grid=(t // tr,),
        in_specs=[pl.BlockSpec((tr, d), lambda i: (i, 0)), pl.BlockSpec((1, d), lambda i: (0, 0))],
        out_specs=pl.BlockSpec((tr, d), lambda i: (i, 0)),
        compiler_params=_cparams(("parallel",)),
        name="rmsnorm",
    )(x2d, g.reshape(1, d))


def _mm_kernel(*refs, n_dots, n_extra, epilogue):
    out_ref = refs[-1]
    res = []
    for d in range(n_dots):
        res.append(jnp.dot(refs[2 * d][...], refs[2 * d + 1][...], preferred_element_type=F32))
    extras = [refs[2 * n_dots + e][...] for e in range(n_extra)]
    out_ref[...] = epilogue(res, extras).astype(out_ref.dtype)


def _matmul(dots, extras, epilogue, n_out, out_dtype, tm, tn, name):
    m = dots[0][0].shape[0]
    in_specs, args = [], []
    for a, w in dots:
        k = a.shape[1]
        in_specs.append(pl.BlockSpec((tm, k), lambda i, j: (i, 0)))
        in_specs.append(pl.BlockSpec((k, tn), lambda i, j: (0, j)))
        args += [a, w]
    for arr, off in extras:
        in_specs.append(pl.BlockSpec((tm, tn), lambda i, j, off=off: (i, j + off)))
        args.append(arr)
    kern = functools.partial(_mm_kernel, n_dots=len(dots), n_extra=len(extras), epilogue=epilogue)
    return pl.pallas_call(
        kern,
        out_shape=jax.ShapeDtypeStruct((m, n_out), out_dtype),
        grid=(m // tm, n_out // tn),
        in_specs=in_specs,
        out_specs=pl.BlockSpec((tm, tn), lambda i, j: (i, j)),
        compiler_params=_cparams(("parallel", "arbitrary")),
        name=name,
    )(*args)


def _ep_identity(res, extras):
    return res[0]


def _ep_merge(res, extras):
    return jax.nn.sigmoid(extras[0]) * res[0] + jax.nn.sigmoid(extras[1]) * res[1]


def _ep_residual(res, extras):
    return extras[0] + res[0]


def _ep_ple(res, extras):
    return extras[0] + jax.nn.sigmoid(res[0]) * res[1]


def _mm_acc_kernel(a_ref, w_ref, r_ref, o_ref, acc_ref):
    kk = pl.program_id(2)

    @pl.when(kk == 0)
    def _():
        acc_ref[...] = jnp.zeros_like(acc_ref)

    acc_ref[...] += jnp.dot(a_ref[...], w_ref[...], preferred_element_type=F32)

    @pl.when(kk == pl.num_programs(2) - 1)
    def _():
        o_ref[...] = r_ref[...] + acc_ref[...]


def _matmul_acc_residual(a, w, resid, tm, tn, tk, name):
    m, k = a.shape
    n = w.shape[1]
    return pl.pallas_call(
        _mm_acc_kernel,
        out_shape=jax.ShapeDtypeStruct((m, n), F32),
        grid=(m // tm, n // tn, k // tk),
        in_specs=[pl.BlockSpec((tm, tk), lambda i, j, kk: (i, kk)),
                  pl.BlockSpec((tk, tn), lambda i, j, kk: (kk, j)),
                  pl.BlockSpec((tm, tn), lambda i, j, kk: (i, j))],
        out_specs=pl.BlockSpec((tm, tn), lambda i, j, kk: (i, j)),
        scratch_shapes=[pltpu.VMEM((tm, tn), F32)],
        compiler_params=_cparams(("parallel", "arbitrary", "arbitrary")),
        name=name,
    )(a, w, resid)


def _ffn_in_kernel(a_ref, wg_ref, wu_ref, cwg_ref, cwu_ref, cbg_ref, cbu_ref, o_ref, carry_ref, *, tiles_per_seq):
    i = pl.program_id(1)

    @pl.when(i % tiles_per_seq == 0)
    def _():
        carry_ref[...] = jnp.zeros_like(carry_ref)

    a = a_ref[...]
    tm = a.shape[0]
    row = lax.broadcasted_iota(jnp.int32, (tm, 1), 0)

    def conv(u, carry, cw, cb):
        c1 = carry[7:8, :]
        c2 = carry[6:7, :]
        p1 = jnp.where(row == 0, c1, pltpu.roll(u, 1, 0))
        p2 = jnp.where(row == 0, c2, jnp.where(row == 1, c1, pltpu.roll(u, 2, 0)))
        return cb + p2 * cw[0:1, :] + p1 * cw[1:2, :] + u * cw[2:3, :]

    ug = jnp.dot(a, wg_ref[...], preferred_element_type=F32)
    uu = jnp.dot(a, wu_ref[...], preferred_element_type=F32)
    gate = conv(ug, carry_ref[0], cwg_ref[...], cbg_ref[...])
    up = conv(uu, carry_ref[1], cwu_ref[...], cbu_ref[...])
    carry_ref[0] = ug[tm - 8:tm, :]
    carry_ref[1] = uu[tm - 8:tm, :]
    o_ref[...] = (gate * jax.nn.sigmoid(gate) * up).astype(o_ref.dtype)


def _ffn_in(h, w, conv_w, conv_b, dff, seq, tm, tn):
    t, k = h.shape
    nj = dff // tn
    kern = functools.partial(_ffn_in_kernel, tiles_per_seq=seq // tm)
    return pl.pallas_call(
        kern,
        out_shape=jax.ShapeDtypeStruct((t, dff), BF16),
        grid=(nj, t // tm),
        in_specs=[pl.BlockSpec((tm, k), lambda j, i: (i, 0)),
                  pl.BlockSpec((k, tn), lambda j, i: (0, j)),
                  pl.BlockSpec((k, tn), lambda j, i: (0, j + nj)),
                  pl.BlockSpec((3, tn), lambda j, i: (0, j)),
                  pl.BlockSpec((3, tn), lambda j, i: (0, j + nj)),
                  pl.BlockSpec((1, tn), lambda j, i: (0, j)),
                  pl.BlockSpec((1, tn), lambda j, i: (0, j + nj))],
        out_specs=pl.BlockSpec((tm, tn), lambda j, i: (i, j)),
        scratch_shapes=[pltpu.VMEM((2, 8, tn), F32)],
        compiler_params=_cparams(("arbitrary", "arbitrary")),
        name="ffn_in_conv_gate",
    )(h, w, w, conv_w, conv_w, conv_b, conv_b)


def _rwkv_consts():
    r = lax.broadcasted_iota(jnp.int32, (GROUP_W, GROUP_W), 0) // HEAD
    c = lax.broadcasted_iota(jnp.int32, (GROUP_W, GROUP_W), 1) // HEAD
    bd = r == c
    t = lax.broadcasted_iota(jnp.int32, (CHUNK, GROUP_W), 0)
    s = lax.broadcasted_iota(jnp.int32, (CHUNK, GROUP_W), 1) % HEAD
    tt = lax.broadcasted_iota(jnp.int32, (CHUNK, CHUNK), 0)
    ss = lax.broadcasted_iota(jnp.int32, (CHUNK, CHUNK), 1)
    return dict(bd=bd, ones_bd=bd.astype(BF16), strict=s < t, incl=s <= t,
                eye=(s == t).astype(F32), ltri=(ss <= tt).astype(BF16))


def _block_diag(y, c):
    return jnp.where(c['bd'], jnp.concatenate([y, y, y, y], axis=0), 0.0)


def _segsum(x, c):
    return _dot2_exact_rhs(x, c['ones_bd'], _NN)


def _rwkv_chunk(r, kraw, v, wpre, apre, g, prm, ht, c):
    k_k, k_a, r_k, ln_w, ln_b = prm
    ld = -np.float32(math.exp(-0.5)) * jax.nn.sigmoid(wpre)
    a = jax.nn.sigmoid(apre)
    kk = kraw * k_k
    kk = kk / jnp.maximum(jnp.sqrt(_segsum(kk * kk, c)), 1e-12)
    k = kraw * (1.0 + (a - 1.0) * k_a)
    b = kk * a
    cum = _dot2_exact_rhs_lhs(c['ltri'], ld)
    cum_last = cum[CHUNK - 1:CHUNK, :]
    dec_in = jnp.exp(cum)
    dec_out = jnp.exp(-cum)
    r_t = r * dec_in
    a_t = -kk * jnp.exp(cum - ld)
    b_t = b * dec_out
    k_t = k * dec_out
    to_end = jnp.exp(cum_last - cum)
    b_p = b * to_end
    k_p = k * to_end
    ar = jnp.concatenate([a_t, r_t], axis=0)
    sb = _dot3(ar, _block_diag(b_t, c), _NT)
    sk = _dot3(ar, _block_diag(k_t, c), _NT)
    s_ab = jnp.where(c['strict'], sb[:CHUNK], 0.0)
    s_rb = jnp.where(c['incl'], sb[CHUNK:], 0.0)
    s_ak = jnp.where(c['strict'], sk[:CHUNK], 0.0)
    s_rk = jnp.where(c['incl'], sk[CHUNK:], 0.0)
    pw = s_ab
    inv = c['eye'] + pw
    for _ in range(5):
        pw = _dot3(pw, _block_diag(pw, c), _NN)
        inv = inv + _dot3(inv, _block_diag(pw, c), _NN)
    v_bd = _block_diag(v, c)
    arh = _dot3(ar, ht, _NT)
    y0 = arh[:CHUNK] + _dot3(s_ak, v_bd, _NN)
    u = _dot3(inv, _block_diag(y0, c), _NN)
    o = arh[CHUNK:] + _dot3(s_rb, _block_diag(u, c), _NN) + _dot3(s_rk, v_bd, _NN)
    uv = jnp.concatenate([u, v], axis=0)
    bk = jnp.concatenate([b_p, k_p], axis=0)
    ht_new = ht * jnp.exp(cum_last) + jnp.where(c['bd'], _dot3(uv, bk, _TN), 0.0)
    mean = _segsum(o, c) * (1.0 / HEAD)
    d = o - mean
    var = _segsum(d * d, c) * (1.0 / HEAD)
    o = d * lax.rsqrt(var + RWKV_GN_EPS) * ln_w + ln_b
    bonus = _segsum(r * k * r_k, c) * v
    return (o + bonus) * g, ht_new


def _dot2_exact_rhs_lhs(l_bf16, x):
    xh, xl = _split(x)
    xl2 = (x - xh.astype(F32) - xl.astype(F32)).astype(BF16)
    return _dot(l_bf16, xh, _NN) + (_dot(l_bf16, xl, _NN) + _dot(l_bf16, xl2, _NN))


def _rwkv_kernel(zr_ref, zk_ref, zv_ref, zl_ref, mur_ref, muk_ref, muv_ref, mul_ref,
                 w0_ref, a0_ref, w2_ref, a2_ref, g2_ref, kk_ref, ka_ref, rk_ref, lnw_ref, lnb_ref,
                 o_ref, ht_ref, carry_ref, carryl_ref, *, n_chunks):
    step = pl.program_id(2)

    @pl.when(step == 0)
    def _():
        ht_ref[...] = jnp.zeros_like(ht_ref)
        carry_ref[...] = jnp.zeros_like(carry_ref)
        carryl_ref[...] = jnp.zeros_like(carryl_ref)

    rows = zr_ref.shape[0]
    row = lax.broadcasted_iota(jnp.int32, (rows, 1), 0)

    def shifted(z, prev_row, mu):
        prev = jnp.where(row == 0, prev_row, pltpu.roll(z, 1, 0))
        return z + (prev - z) * mu

    zr, zk, zv, zl = zr_ref[...], zk_ref[...], zv_ref[...], zl_ref[...]
    r_all = shifted(zr, carry_ref[0, 0:1, :], mur_ref[...])
    k_all = shifted(zk, carry_ref[1, 0:1, :], muk_ref[...])
    v_all = shifted(zv, carry_ref[2, 0:1, :], muv_ref[...])
    l_all = shifted(zl, carryl_ref[0:1, :], mul_ref[...])
    carry_ref[0] = jnp.broadcast_to(zr[rows - 1:rows, :], (8, GROUP_W))
    carry_ref[1] = jnp.broadcast_to(zk[rows - 1:rows, :], (8, GROUP_W))
    carry_ref[2] = jnp.broadcast_to(zv[rows - 1:rows, :], (8, GROUP_W))
    carryl_ref[...] = jnp.broadcast_to(zl[rows - 1:rows, :], carryl_ref.shape)

    wpre_all = w0_ref[...] + _dot3(jnp.tanh(l_all[:, 0:128]), w2_ref[...], _NN)
    apre_all = a0_ref[...] + _dot3(l_all[:, 128:256], a2_ref[...], _NN)
    g_all = _dot3(jax.nn.sigmoid(l_all[:, 256:512]), g2_ref[...], _NN)

    c = _rwkv_consts()
    prm = (kk_ref[...], ka_ref[...], rk_ref[...], lnw_ref[...], lnb_ref[...])
    ht = ht_ref[...]
    for ci in range(n_chunks):
        sl = slice(ci * CHUNK, (ci + 1) * CHUNK)
        out, ht = _rwkv_chunk(r_all[sl], k_all[sl], v_all[sl], wpre_all[sl], apre_all[sl], g_all[sl], prm, ht, c)
        o_ref[sl, :] = out.astype(o_ref.dtype)
    ht_ref[...] = ht


def _rwkv(z3, col_r, col_k, col_v, col_l, mu_rkv, mu_l, w0, a0, w2p, a2, g2, k_k, k_a, r_k, ln_w, ln_b,
          width, rows):
    bsz, seq, _ = z3.shape
    ng = width // GROUP_W
    gw = GROUP_W

    def zspec(col):
        return pl.BlockSpec((None, rows, gw), lambda b, g, s, o=col // gw: (b, s, o + g))

    def vspec(off=0):
        return pl.BlockSpec((1, gw), lambda b, g, s, o=off: (0, o + g))

    def mspec(k):
        return pl.BlockSpec((k, gw), lambda b, g, s: (0, g))

    in_specs = [zspec(col_r), zspec(col_k), zspec(col_v),
                pl.BlockSpec((None, rows, 512), lambda b, g, s, o=col_l // 512: (b, s, o)),
                vspec(0), vspec(ng), vspec(2 * ng),
                pl.BlockSpec((1, 512), lambda b, g, s: (0, 0)),
                vspec(), vspec(), mspec(128), mspec(128), mspec(256),
                vspec(), vspec(), vspec(), vspec(), vspec()]
    kern = functools.partial(_rwkv_kernel, n_chunks=rows // CHUNK)
    return pl.pallas_call(
        kern,
        out_shape=jax.ShapeDtypeStruct((bsz, seq, width), BF16),
        grid=(bsz, ng, seq // rows),
        in_specs=in_specs,
        out_specs=pl.BlockSpec((None, rows, gw), lambda b, g, s: (b, s, g)),
        scratch_shapes=[pltpu.VMEM((gw, gw), F32), pltpu.VMEM((3, 8, gw), F32), pltpu.VMEM((8, 512), F32)],
        compiler_params=_cparams(("parallel", "parallel", "arbitrary")),
        name="rwkv7_chunked",
    )(z3, z3, z3, z3, mu_rkv, mu_rkv, mu_rkv, mu_l, w0, a0, w2p, a2, g2, k_k, k_a, r_k, ln_w, ln_b)


def _qk_prep_kernel(q_ref, k_ref, v_ref, qg_ref, kg_ref, qo_ref, ko_ref, vo_ref):
    r = lax.broadcasted_iota(jnp.int32, (GROUP_W, GROUP_W), 0) // HEAD
    c = lax.broadcasted_iota(jnp.int32, (GROUP_W, GROUP_W), 1) // HEAD
    ones_bd = (r == c).astype(BF16)

    def norm(x, g):
        ms = _dot2_exact_rhs(x * x, ones_bd, _NN) * (1.0 / HEAD)
        return x * lax.rsqrt(ms + NORM_EPS) * g

    scale = HEAD ** -0.5
    qo_ref[...] = (norm(q_ref[...], qg_ref[...]) * scale).astype(BF16)
    ko_ref[...] = norm(k_ref[...], kg_ref[...]).astype(BF16)
    vo_ref[...] = v_ref[...].astype(BF16)


def _qk_prep(z2, col_q, width, q_g, k_g, tr=512):
    t = z2.shape[0]
    gw = GROUP_W
    nw = width // gw
    qg = jnp.tile(q_g, gw // HEAD).reshape(1, gw)
    kg = jnp.tile(k_g, gw // HEAD).reshape(1, gw)

    def zspec(col):
        return pl.BlockSpec((tr, gw), lambda i, j, o=col // gw: (i, o + j))

    ospec = pl.BlockSpec((tr, gw), lambda i, j: (i, j))
    gspec = pl.BlockSpec((1, gw), lambda i, j: (0, 0))
    shp = jax.ShapeDtypeStruct((t, width), BF16)
    return pl.pallas_call(
        _qk_prep_kernel,
        out_shape=(shp, shp, shp),
        grid=(t // tr, nw),
        in_specs=[zspec(col_q), zspec(col_q + width), zspec(col_q + 2 * width), gspec, gspec],
        out_specs=(ospec, ospec, ospec),
        compiler_params=_cparams(("parallel", "parallel")),
        name="qk_norm_prep",
    )(z2, z2, z2, qg, kg)


def _diff_attn_kernel(qi_ref, kj_ref, q_ref, k_ref, v_ref, lam_ref, g_ref, o_ref,
                      q2_ref, m_ref, l_ref, acc_ref, *, tq, tk, out_scale):
    t = pl.program_id(2)
    qi = qi_ref[t]
    kj = kj_ref[t]
    hd = 2 * HEAD

    @pl.when(kj == 0)
    def _():
        q = q_ref[...]
        lane = lax.broadcasted_iota(jnp.int32, (tq, hd), 1)
        zero = jnp.zeros_like(q)
        q2_ref[0:tq, :] = jnp.where(lane < HEAD, q, zero)
        q2_ref[tq:2 * tq, :] = jnp.where(lane >= HEAD, q, zero)
        m_ref[...] = jnp.full_like(m_ref, -jnp.inf)
        l_ref[...] = jnp.zeros_like(l_ref)
        acc_ref[...] = jnp.zeros_like(acc_ref)

    def update(masked):
        s = _dot(q2_ref[...], k_ref[...], _NT)
        if masked:
            qpos = lax.broadcasted_iota(jnp.int32, (2 * tq, tk), 0) % tq
            kpos = lax.broadcasted_iota(jnp.int32, (2 * tq, tk), 1)
            s = jnp.where(kpos // CHUNK <= qpos // CHUNK, s, -jnp.inf)
        m_old = m_ref[...]
        m_new = jnp.maximum(m_old, jnp.max(s, axis=-1, keepdims=True))
        p = jnp.exp(s - m_new)
        alpha = jnp.exp(m_old - m_new)
        l_ref[...] = alpha * l_ref[...] + jnp.sum(p, axis=-1, keepdims=True)
        acc_ref[...] = alpha * acc_ref[...] + _dot(p.astype(BF16), v_ref[...], _NN)
        m_ref[...] = m_new

    @pl.when(kj < qi)
    def _():
        update(False)

    @pl.when(kj == qi)
    def _():
        update(True)
        acc = acc_ref[...] / l_ref[...]
        o = acc[0:tq] - lam_ref[...] * acc[tq:2 * tq]
        ms = jnp.mean(o * o, axis=-1, keepdims=True)
        o_ref[...] = (o * lax.rsqrt(ms + NORM_EPS) * g_ref[...] * out_scale).astype(o_ref.dtype)


def _diff_attn(q3, k3, v3, lam, subln_g, out_scale, tq):
    bsz, seq, width = q3.shape
    hd = 2 * HEAD
    nh = width // hd
    nq = seq // tq
    pairs = [(i, j) for i in range(nq) for j in range(i + 1)]
    qi = jnp.asarray([p[0] for p in pairs], jnp.int32)
    kj = jnp.asarray([p[1] for p in pairs], jnp.int32)
    lam_row = jnp.broadcast_to(lam.astype(F32).reshape(1, 1), (1, hd))
    kern = functools.partial(_diff_attn_kernel, tq=tq, tk=tq, out_scale=out_scale)
    grid_spec = pltpu.PrefetchScalarGridSpec(
        num_scalar_prefetch=2,
        grid=(bsz, nh, len(pairs)),
        in_specs=[pl.BlockSpec((None, tq, hd), lambda b, h, t, qi, kj: (b, qi[t], h)),
                  pl.BlockSpec((None, tq, hd), lambda b, h, t, qi, kj: (b, kj[t], h)),
                  pl.BlockSpec((None, tq, hd), lambda b, h, t, qi, kj: (b, kj[t], h)),
                  pl.BlockSpec((1, hd), lambda b, h, t, qi, kj: (0, 0)),
                  pl.BlockSpec((1, hd), lambda b, h, t, qi, kj: (0, 0))],
        out_specs=pl.BlockSpec((None, tq, hd), lambda b, h, t, qi, kj: (b, qi[t], h)),
        scratch_shapes=[pltpu.VMEM((2 * tq, hd), BF16), pltpu.VMEM((2 * tq, 1), F32),
                        pltpu.VMEM((2 * tq, 1), F32), pltpu.VMEM((2 * tq, hd), F32)],
    )
    return pl.pallas_call(
        kern,
        out_shape=jax.ShapeDtypeStruct((bsz, seq, width), BF16),
        grid_spec=grid_spec,
        compiler_params=_cparams(("parallel", "parallel", "arbitrary")),
        name="diff_flash_attention",
    )(qi, kj, q3, k3, v3, lam_row, subln_g.reshape(1, hd))


def _layer(x, p_i, lambda_init, norm_mix_g, w_in, rwkv_mu, rwkv_w0, rwkv_w2, rwkv_a0, rwkv_a2, rwkv_g2,
           rwkv_k_k, rwkv_k_a, rwkv_r_k, rwkv_ln_w, rwkv_ln_b, q_norm_g, k_norm_g,
           lam_q1, lam_k1, lam_q2, lam_k2, subln_g, w_branch_a, w_branch_b, w_out,
           norm_ffn_g, w_ffn_in, ffn_conv_w, ffn_conv_b, w_ffn_out, norm_ple_g, w_ple_gate, w_ple_proj):
    bsz, seq, d = x.shape
    t = bsz * seq
    rw = w_branch_a.shape[0]
    dw = w_branch_b.shape[0]
    n_w, n_a, n_g = rwkv_w2.shape[0], rwkv_a2.shape[0], rwkv_g2.shape[0]
    lora_pad = 128 - n_w
    assert n_w <= 128 and n_a == 128 and n_g == 256 and rw % GROUP_W == 0 and dw % GROUP_W == 0
    dff = w_ffn_out.shape[0]
    dff_p = -(-dff // 1024) * 1024

    c_lo = 3 * rw
    c_diff = c_lo + n_w + n_a + n_g
    c_gate = c_diff + 3 * dw
    zeros_pad = jnp.zeros((d, lora_pad), w_in.dtype)
    w_in_p = jnp.concatenate(
        [w_in[:, :c_lo], w_in[:, c_diff:c_gate], w_in[:, c_gate:],
         w_in[:, c_lo:c_lo + n_w], zeros_pad, w_in[:, c_lo + n_w:c_diff]], axis=1).astype(BF16)
    col_q = 3 * rw
    col_ga = col_q + 3 * dw
    col_gb = col_ga + d
    col_l = col_gb + d
    n_in = col_l + 512
    mu_rkv = rwkv_mu[:c_lo].reshape(1, c_lo)
    mu_l = jnp.concatenate([rwkv_mu[c_lo:c_lo + n_w], jnp.zeros((lora_pad,), F32),
                            rwkv_mu[c_lo + n_w:]]).reshape(1, 512)
    w2p = jnp.concatenate([rwkv_w2, jnp.zeros((lora_pad, rw), F32)], axis=0)

    tm, tn, nr = TILES['tm'], TILES['tn'], TILES['norm_rows']
    x2 = x.reshape(t, d)
    h = _rmsnorm(x2, norm_mix_g, nr)
    z = _matmul([(h, w_in_p)], [], _ep_identity, n_in, F32, tm, tn, "in_proj")
    z3 = z.reshape(bsz, seq, n_in)
    row = lambda v_: v_.reshape(1, -1)
    o_a = _rwkv(z3, 0, rw, 2 * rw, col_l, mu_rkv, mu_l, row(rwkv_w0), row(rwkv_a0), w2p, rwkv_a2, rwkv_g2,
                row(rwkv_k_k), row(rwkv_k_a), row(rwkv_r_k), row(rwkv_ln_w), row(rwkv_ln_b), rw,
                TILES['rwkv_rows'])
    qn, kn, vb = _qk_prep(z, col_q, dw, q_norm_g, k_norm_g, TILES['prep_rows'])
    lam = (jnp.exp(jnp.sum(lam_q1 * lam_k1)) - jnp.exp(jnp.sum(lam_q2 * lam_k2)) + lambda_init)
    shp3 = (bsz, seq, dw)
    o_b = _diff_attn(qn.reshape(shp3), kn.reshape(shp3), vb.reshape(shp3), lam, subln_g,
                     1.0 - lambda_init, TILES['attn'])
    merged = _matmul([(o_a.reshape(t, rw), w_branch_a.astype(BF16)), (o_b.reshape(t, dw), w_branch_b.astype(BF16))],
                     [(z, col_ga // tn), (z, col_gb // tn)], _ep_merge, d, BF16, tm, tn, "branch_merge")
    x2 = _matmul([(merged, w_out.astype(BF16))], [(x2, 0)], _ep_residual, d, F32, tm, tn, "out_proj")
    h = _rmsnorm(x2, norm_ffn_g, nr)
    pad_c = dff_p - dff
    w1 = w_ffn_in.astype(BF16)
    zc = jnp.zeros((d, pad_c), BF16)
    w1p = jnp.concatenate([w1[:, :dff], zc, w1[:, dff:], zc], axis=1)
    zcw = jnp.zeros((ffn_conv_w.shape[0], pad_c), F32)
    cwp = jnp.concatenate([ffn_conv_w[:, :dff], zcw, ffn_conv_w[:, dff:], zcw], axis=1)
    zcb = jnp.zeros((pad_c,), F32)
    cbp = jnp.concatenate([ffn_conv_b[:dff], zcb, ffn_conv_b[dff:], zcb]).reshape(1, 2 * dff_p)
    act = _ffn_in(h, w1p, cwp, cbp, dff_p, seq, tm, tn)
    w2f = jnp.concatenate([w_ffn_out.astype(BF16), jnp.zeros((pad_c, d), BF16)], axis=0)
    x2 = _matmul_acc_residual(act, w2f, x2, tm, TILES['ffn_out_tn'], dff_p // 4, "ffn_out")
    h = _rmsnorm(x2, norm_ple_g, nr)
    pe = p_i.reshape(t, -1).astype(BF16)
    x2 = _matmul([(h, w_ple_gate.astype(BF16)), (pe, w_ple_proj.astype(BF16))], [(x2, 0)], _ep_ple, d, F32,
                 tm, tn, "ple_gate")
    return x2.reshape(bsz, seq, d)


def kernel(x, p, norm_mix_g, w_in, rwkv_mu, rwkv_w0, rwkv_w2, rwkv_a0, rwkv_a2, rwkv_g2, rwkv_k_k, rwkv_k_a,
           rwkv_r_k, rwkv_ln_w, rwkv_ln_b, q_norm_g, k_norm_g, lam_q1, lam_k1, lam_q2, lam_k2, subln_g,
           w_branch_a, w_branch_b, w_out, norm_ffn_g, w_ffn_in, ffn_conv_w, ffn_conv_b, w_ffn_out,
           norm_ple_g, w_ple_gate, w_ple_proj):
    depth = p.shape[0]
    for i in range(depth):
        lambda_init = 0.8 - 0.6 * math.exp(-0.3 * i)
        x = _layer(x, p[i], lambda_init, norm_mix_g[i], w_in[i], rwkv_mu[i], rwkv_w0[i], rwkv_w2[i], rwkv_a0[i],
                   rwkv_a2[i], rwkv_g2[i], rwkv_k_k[i], rwkv_k_a[i], rwkv_r_k[i], rwkv_ln_w[i], rwkv_ln_b[i],
                   q_norm_g[i], k_norm_g[i], lam_q1[i], lam_k1[i], lam_q2[i], lam_k2[i], subln_g[i],
                   w_branch_a[i], w_branch_b[i], w_out[i], norm_ffn_g[i], w_ffn_in[i], ffn_conv_w[i],
                   ffn_conv_b[i], w_ffn_out[i], norm_ple_g[i], w_ple_gate[i], w_ple_proj[i])
    return x
```

```python
import functools
import math

import jax
import jax.numpy as jnp
import numpy as np
from jax import lax
from jax.experimental import pallas as pl
from jax.experimental.pallas import tpu as pltpu

F32 = jnp.float32
BF16 = jnp.bfloat16

NORM_EPS = 1e-6
RWKV_GN_EPS = 64e-5
HEAD = 64
CHUNK = 64
GROUP_W = 256
VMEM_LIMIT = 56 * 1024 * 1024
TILES = dict(tm=1024, tn=512, rwkv_rows=256, attn=512, norm_rows=256, ffn_out_tn=1024)


def _cparams(sem):
    return pltpu.CompilerParams(dimension_semantics=sem, vmem_limit_bytes=VMEM_LIMIT)


def _dot(a, b, dims):
    return lax.dot_general(a, b, (dims, ((), ())), preferred_element_type=F32)


_NN = ((1,), (0,))
_NT = ((1,), (1,))
_TN = ((0,), (0,))


def _split(x):
    hi = x.astype(BF16)
    lo = (x - hi.astype(F32)).astype(BF16)
    return hi, lo


def _dot3(a, b, dims):
    ah, al = _split(a)
    bh, bl = _split(b)
    return _dot(ah, bh, dims) + (_dot(ah, bl, dims) + _dot(al, bh, dims))


def _dot2_exact_rhs(a, b_bf16, dims):
    ah, al = _split(a)
    return _dot(ah, b_bf16, dims) + _dot(al, b_bf16, dims)


def _rmsnorm_kernel(x_ref, g_ref, o_ref):
    x = x_ref[...]
    ms = jnp.mean(x * x, axis=-1, keepdims=True)
    o_ref[...] = (x * lax.rsqrt(ms + NORM_EPS) * g_ref[...]).astype(o_ref.dtype)


def _rmsnorm(x2d, g, tr=256):
    t, d = x2d.shape
    return pl.pallas_call(
        _rmsnorm_kernel,
        out_shape=jax.ShapeDtypeStruct((t, d), BF16),
        grid=(t // tr,),
        in_specs=[pl.BlockSpec((tr, d), lambda i: (i, 0)), pl.BlockSpec((1, d), lambda i: (0, 0))],
        out_specs=pl.BlockSpec((tr, d), lambda i: (i, 0)),
        compiler_params=_cparams(("parallel",)),
        name="rmsnorm",
    )(x2d, g.reshape(1, d))


def _mm_kernel(*refs, n_dots, n_extra, epilogue):
    out_ref = refs[-1]
    res = []
    for d in range(n_dots):
        res.append(jnp.dot(refs[2 * d][...], refs[2 * d + 1][...], preferred_element_type=F32))
    extras = [refs[2 * n_dots + e][...] for e in range(n_extra)]
    out_ref[...] = epilogue(res, extras).astype(out_ref.dtype)


def _matmul(dots, extras, epilogue, n_out, out_dtype, tm, tn, name):
    m = dots[0][0].shape[0]
    in_specs, args = [], []
    for a, w in dots:
        k = a.shape[1]
        in_specs.append(pl.BlockSpec((tm, k), lambda i, j: (i, 0)))
        in_specs.append(pl.BlockSpec((k, tn), lambda i, j: (0, j)))
        args += [a, w]
    for arr, off in extras:
        in_specs.append(pl.BlockSpec((tm, tn), lambda i, j, off=off: (i, j + off)))
        args.append(arr)
    kern = functools.partial(_mm_kernel, n_dots=len(dots), n_extra=len(extras), epilogue=epilogue)
    return pl.pallas_call(
        kern,
        out_shape=jax.ShapeDtypeStruct((m, n_out), out_dtype),
        grid=(m // tm, n_out // tn),
        in_specs=in_specs,
        out_specs=pl.BlockSpec((tm, tn), lambda i, j: (i, j)),
        compiler_params=_cparams(("parallel", "arbitrary")),
        name=name,
    )(*args)


def _ep_identity(res, extras):
    return res[0]


def _ep_merge(res, extras):
    return jax.nn.sigmoid(extras[0]) * res[0] + jax.nn.sigmoid(extras[1]) * res[1]


def _ep_residual(res, extras):
    return extras[0] + res[0]


def _ep_ple(res, extras):
    return extras[0] + jax.nn.sigmoid(res[0]) * res[1]


def _mm_acc_kernel(a_ref, w_ref, r_ref, o_ref, acc_ref):
    kk = pl.program_id(2)

    @pl.when(kk == 0)
    def _():
        acc_ref[...] = jnp.zeros_like(acc_ref)

    acc_ref[...] += jnp.dot(a_ref[...], w_ref[...], preferred_element_type=F32)

    @pl.when(kk == pl.num_programs(2) - 1)
    def _():
        o_ref[...] = r_ref[...] + acc_ref[...]


def _matmul_acc_residual(a, w, resid, tm, tn, tk, name):
    m, k = a.shape
    n = w.shape[1]
    return pl.pallas_call(
        _mm_acc_kernel,
        out_shape=jax.ShapeDtypeStruct((m, n), F32),
        grid=(m // tm, n // tn, k // tk),
        in_specs=[pl.BlockSpec((tm, tk), lambda i, j, kk: (i, kk)),
                  pl.BlockSpec((tk, tn), lambda i, j, kk: (kk, j)),
                  pl.BlockSpec((tm, tn), lambda i, j, kk: (i, j))],
        out_specs=pl.BlockSpec((tm, tn), lambda i, j, kk: (i, j)),
        scratch_shapes=[pltpu.VMEM((tm, tn), F32)],
        compiler_params=_cparams(("parallel", "arbitrary", "arbitrary")),
        name=name,
    )(a, w, resid)


def _ffn_in_kernel(a_ref, wg_ref, wu_ref, cwg_ref, cwu_ref, cbg_ref, cbu_ref, o_ref, carry_ref, *, tiles_per_seq):
    i = pl.program_id(1)

    @pl.when(i % tiles_per_seq == 0)
    def _():
        carry_ref[...] = jnp.zeros_like(carry_ref)

    a = a_ref[...]
    tm = a.shape[0]
    row = lax.broadcasted_iota(jnp.int32, (tm, 1), 0)

    def conv(u, carry, cw, cb):
        c1 = carry[7:8, :]
        c2 = carry[6:7, :]
        p1 = jnp.where(row == 0, c1, pltpu.roll(u, 1, 0))
        p2 = jnp.where(row == 0, c2, jnp.where(row == 1, c1, pltpu.roll(u, 2, 0)))
        return cb + p2 * cw[0:1, :] + p1 * cw[1:2, :] + u * cw[2:3, :]

    ug = jnp.dot(a, wg_ref[...], preferred_element_type=F32)
    uu = jnp.dot(a, wu_ref[...], preferred_element_type=F32)
    gate = conv(ug, carry_ref[0], cwg_ref[...], cbg_ref[...])
    up = conv(uu, carry_ref[1], cwu_ref[...], cbu_ref[...])
    carry_ref[0] = ug[tm - 8:tm, :]
    carry_ref[1] = uu[tm - 8:tm, :]
    o_ref[...] = (gate * jax.nn.sigmoid(gate) * up).astype(o_ref.dtype)


def _ffn_in(h, w, conv_w, conv_b, dff, seq, tm, tn):
    t, k = h.shape
    nj = dff // tn
    kern = functools.partial(_ffn_in_kernel, tiles_per_seq=seq // tm)
    return pl.pallas_call(
        kern,
        out_shape=jax.ShapeDtypeStruct((t, dff), BF16),
        grid=(nj, t // tm),
        in_specs=[pl.BlockSpec((tm, k), lambda j, i: (i, 0)),
                  pl.BlockSpec((k, tn), lambda j, i: (0, j)),
                  pl.BlockSpec((k, tn), lambda j, i: (0, j + nj)),
                  pl.BlockSpec((3, tn), lambda j, i: (0, j)),
                  pl.BlockSpec((3, tn), lambda j, i: (0, j + nj)),
                  pl.BlockSpec((1, tn), lambda j, i: (0, j)),
                  pl.BlockSpec((1, tn), lambda j, i: (0, j + nj))],
        out_specs=pl.BlockSpec((tm, tn), lambda j, i: (i, j)),
        scratch_shapes=[pltpu.VMEM((2, 8, tn), F32)],
        compiler_params=_cparams(("arbitrary", "arbitrary")),
        name="ffn_in_conv_gate",
    )(h, w, w, conv_w, conv_w, conv_b, conv_b)


def _rwkv_consts():
    r = lax.broadcasted_iota(jnp.int32, (GROUP_W, GROUP_W), 0) // HEAD
    c = lax.broadcasted_iota(jnp.int32, (GROUP_W, GROUP_W), 1) // HEAD
    bd = r == c
    t = lax.broadcasted_iota(jnp.int32, (CHUNK, GROUP_W), 0)
    s = lax.broadcasted_iota(jnp.int32, (CHUNK, GROUP_W), 1) % HEAD
    tt = lax.broadcasted_iota(jnp.int32, (CHUNK, CHUNK), 0)
    ss = lax.broadcasted_iota(jnp.int32, (CHUNK, CHUNK), 1)
    return dict(bd=bd, ones_bd=bd.astype(BF16), strict=s < t, incl=s <= t,
                eye=(s == t).astype(F32), ltri=(ss <= tt).astype(BF16))


def _block_diag(y, c):
    yb = y.astype(BF16)
    return jnp.where(c['bd'], jnp.concatenate([yb, yb, yb, yb], axis=0), jnp.zeros((), BF16))


def _mm(a, b_bf16, dims):
    return _dot(a.astype(BF16), b_bf16, dims)


def _segsum(x, c):
    return _mm(x, c['ones_bd'], _NN)


def _cumsum_rows(l_bf16, x):
    xh, xl = _split(x)
    xl2 = (x - xh.astype(F32) - xl.astype(F32)).astype(BF16)
    return _dot(l_bf16, xh, _NN) + (_dot(l_bf16, xl, _NN) + _dot(l_bf16, xl2, _NN))


def _rwkv_chunks(rs, kraws, vs, wpres, apres, gs, prm, ht, c):
    k_k, k_a, r_k, ln_w, ln_b = prm
    n = len(rs)
    rng = range(n)
    lds = [-np.float32(math.exp(-0.5)) * jax.nn.sigmoid(wpres[i]) for i in rng]
    avs = [jax.nn.sigmoid(apres[i]) for i in rng]
    kk0 = [kraws[i] * k_k for i in rng]
    ssq = [_segsum(kk0[i] * kk0[i], c) for i in rng]
    cums = [_cumsum_rows(c['ltri'], lds[i]) for i in rng]
    kks = [kk0[i] / jnp.maximum(jnp.sqrt(ssq[i]), 1e-12) for i in rng]
    ks = [kraws[i] * (1.0 + (avs[i] - 1.0) * k_a) for i in rng]
    bs = [kks[i] * avs[i] for i in rng]
    last = [cums[i][CHUNK - 1:CHUNK, :] for i in rng]
    dec_out = [jnp.exp(-cums[i]) for i in rng]
    to_end = [jnp.exp(last[i] - cums[i]) for i in rng]
    r_t = [rs[i] * jnp.exp(cums[i]) for i in rng]
    a_t = [-kks[i] * jnp.exp(cums[i] - lds[i]) for i in rng]
    ar = [jnp.concatenate([a_t[i], r_t[i]], axis=0).astype(BF16) for i in rng]
    sb = [_dot(ar[i], _block_diag(bs[i] * dec_out[i], c), _NT) for i in rng]
    sk = [_dot(ar[i], _block_diag(ks[i] * dec_out[i], c), _NT) for i in rng]
    s_ab = [jnp.where(c['strict'], sb[i][:CHUNK], 0.0) for i in rng]
    s_rb = [jnp.where(c['incl'], sb[i][CHUNK:], 0.0).astype(BF16) for i in rng]
    s_ak = [jnp.where(c['strict'], sk[i][:CHUNK], 0.0).astype(BF16) for i in rng]
    s_rk = [jnp.where(c['incl'], sk[i][CHUNK:], 0.0).astype(BF16) for i in rng]
    pw = s_ab
    pw_bd = [_block_diag(pw[i], c) for i in rng]
    inv = [c['eye'] + pw[i] for i in rng]
    for _ in range(5):
        pw = [_mm(pw[i], pw_bd[i], _NN) for i in rng]
        pw_bd = [_block_diag(pw[i], c) for i in rng]
        inv = [inv[i] + _mm(inv[i], pw_bd[i], _NN) for i in rng]
    inv = [inv[i].astype(BF16) for i in rng]
    v_bd = [_block_diag(vs[i], c) for i in rng]
    w = [_dot(inv[i], _block_diag(a_t[i], c), _NN) for i in rng]
    y = [_dot(s_ak[i], v_bd[i], _NN) for i in rng]
    uv = [_dot(inv[i], _block_diag(y[i], c), _NN) for i in rng]
    rw = [(r_t[i] + _dot(s_rb[i], _block_diag(w[i], c), _NN)).astype(BF16) for i in rng]
    ov = [_dot(s_rb[i], _block_diag(uv[i], c), _NN) + _dot(s_rk[i], v_bd[i], _NN) for i in rng]
    bk = [jnp.concatenate([bs[i] * to_end[i], ks[i] * to_end[i]], axis=0).astype(BF16) for i in rng]
    m_st = [jnp.where(c['bd'], _mm(w[i], bk[i][:CHUNK], _TN), 0.0).astype(BF16) for i in rng]
    c_st = [jnp.where(c['bd'], _mm(jnp.concatenate([uv[i], vs[i]], axis=0), bk[i], _TN), 0.0) for i in rng]
    bonus = [_segsum(rs[i] * ks[i] * r_k, c) * vs[i] for i in rng]
    outs = []
    for i in rng:
        hb = ht.astype(BF16)
        outs.append(_dot(rw[i], hb, _NT) + ov[i])
        ht = ht * jnp.exp(last[i]) + _dot(hb, m_st[i], _NN) + c_st[i]
    mean = [_segsum(outs[i], c) * (1.0 / HEAD) for i in rng]
    d = [outs[i] - mean[i] for i in rng]
    var = [_segsum(d[i] * d[i], c) * (1.0 / HEAD) for i in rng]
    res = [(d[i] * lax.rsqrt(var[i] + RWKV_GN_EPS) * ln_w + ln_b + bonus[i]) * gs[i] for i in rng]
    return res, ht


def _rwkv_kernel(zr_ref, zk_ref, zv_ref, zl_ref, mur_ref, muk_ref, muv_ref, mul_ref,
                 w0_ref, a0_ref, w2_ref, a2_ref, g2_ref, kk_ref, ka_ref, rk_ref, lnw_ref, lnb_ref,
                 o_ref, ht_ref, carry_ref, carryl_ref, *, n_chunks):
    step = pl.program_id(2)

    @pl.when(step == 0)
    def _():
        ht_ref[...] = jnp.zeros_like(ht_ref)
        carry_ref[...] = jnp.zeros_like(carry_ref)
        carryl_ref[...] = jnp.zeros_like(carryl_ref)

    rows = zr_ref.shape[0]
    row = lax.broadcasted_iota(jnp.int32, (rows, 1), 0)

    def shifted(z, prev_row, mu):
        prev = jnp.where(row == 0, prev_row, pltpu.roll(z, 1, 0))
        return z + (prev - z) * mu

    zr, zk, zv, zl = zr_ref[...], zk_ref[...], zv_ref[...], zl_ref[...]
    r_all = shifted(zr, carry_ref[0, 0:1, :], mur_ref[...])
    k_all = shifted(zk, carry_ref[1, 0:1, :], muk_ref[...])
    v_all = shifted(zv, carry_ref[2, 0:1, :], muv_ref[...])
    l_all = shifted(zl, carryl_ref[0:1, :], mul_ref[...])
    carry_ref[0] = jnp.broadcast_to(zr[rows - 1:rows, :], (8, GROUP_W))
    carry_ref[1] = jnp.broadcast_to(zk[rows - 1:rows, :], (8, GROUP_W))
    carry_ref[2] = jnp.broadcast_to(zv[rows - 1:rows, :], (8, GROUP_W))
    carryl_ref[...] = jnp.broadcast_to(zl[rows - 1:rows, :], carryl_ref.shape)

    wpre_all = w0_ref[...] + _dot3(jnp.tanh(l_all[:, 0:128]), w2_ref[...], _NN)
    apre_all = a0_ref[...] + _mm(l_all[:, 128:256], a2_ref[...].astype(BF16), _NN)
    g_all = _mm(jax.nn.sigmoid(l_all[:, 256:512]), g2_ref[...].astype(BF16), _NN)

    c = _rwkv_consts()
    prm = (kk_ref[...], ka_ref[...], rk_ref[...], lnw_ref[...], lnb_ref[...])
    sls = [slice(ci * CHUNK, (ci + 1) * CHUNK) for ci in range(n_chunks)]
    pick = lambda x: [x[sl] for sl in sls]
    outs, ht = _rwkv_chunks(pick(r_all), pick(k_all), pick(v_all), pick(wpre_all), pick(apre_all), pick(g_all),
                            prm, ht_ref[...], c)
    for sl, out in zip(sls, outs):
        o_ref[sl, :] = out.astype(o_ref.dtype)
    ht_ref[...] = ht


def _rwkv(z3, col_r, col_k, col_v, col_l, mu_rkv, mu_l, w0, a0, w2p, a2, g2, k_k, k_a, r_k, ln_w, ln_b,
          width, rows):
    bsz, seq, _ = z3.shape
    ng = width // GROUP_W
    gw = GROUP_W

    def zspec(col):
        return pl.BlockSpec((None, rows, gw), lambda b, g, s, o=col // gw: (b, s, o + g))

    def vspec(off=0):
        return pl.BlockSpec((1, gw), lambda b, g, s, o=off: (0, o + g))

    def mspec(k):
        return pl.BlockSpec((k, gw), lambda b, g, s: (0, g))

    in_specs = [zspec(col_r), zspec(col_k), zspec(col_v),
                pl.BlockSpec((None, rows, 512), lambda b, g, s, o=col_l // 512: (b, s, o)),
                vspec(0), vspec(ng), vspec(2 * ng),
                pl.BlockSpec((1, 512), lambda b, g, s: (0, 0)),
                vspec(), vspec(), mspec(128), mspec(128), mspec(256),
                vspec(), vspec(), vspec(), vspec(), vspec()]
    kern = functools.partial(_rwkv_kernel, n_chunks=rows // CHUNK)
    return pl.pallas_call(
        kern,
        out_shape=jax.ShapeDtypeStruct((bsz, seq, width), BF16),
        grid=(bsz, ng, seq // rows),
        in_specs=in_specs,
        out_specs=pl.BlockSpec((None, rows, gw), lambda b, g, s: (b, s, g)),
        scratch_shapes=[pltpu.VMEM((gw, gw), F32), pltpu.VMEM((3, 8, gw), F32), pltpu.VMEM((8, 512), F32)],
        compiler_params=_cparams(("parallel", "parallel", "arbitrary")),
        name="rwkv7_chunked",
    )(z3, z3, z3, z3, mu_rkv, mu_rkv, mu_rkv, mu_l, w0, a0, w2p, a2, g2, k_k, k_a, r_k, ln_w, ln_b)


def _qk_prep_kernel(q_ref, k_ref, v_ref, qg_ref, kg_ref, qo_ref, ko_ref, vo_ref):
    r = lax.broadcasted_iota(jnp.int32, (GROUP_W, GROUP_W), 0) // HEAD
    c = lax.broadcasted_iota(jnp.int32, (GROUP_W, GROUP_W), 1) // HEAD
    ones_bd = (r == c).astype(BF16)

    def norm(x, g):
        ms = _dot2_exact_rhs(x * x, ones_bd, _NN) * (1.0 / HEAD)
        return x * lax.rsqrt(ms + NORM_EPS) * g

    scale = HEAD ** -0.5 * math.log2(math.e)
    qo_ref[...] = (norm(q_ref[...], qg_ref[...]) * scale).astype(BF16)
    ko_ref[...] = norm(k_ref[...], kg_ref[...]).astype(BF16)
    vt = jnp.transpose(v_ref[...]).astype(BF16)
    vo_ref[...] = vt.reshape(vo_ref.shape)


def _qk_prep(z2, col_q, width, q_g, k_g, bsz, tr):
    t = z2.shape[0]
    gw = GROUP_W
    hd = 2 * HEAD
    nw = width // gw
    tiles_per_seq = t // bsz // tr
    qg = jnp.tile(q_g, gw // HEAD).reshape(1, gw)
    kg = jnp.tile(k_g, gw // HEAD).reshape(1, gw)

    def zspec(col):
        return pl.BlockSpec((tr, gw), lambda i, j, o=col // gw: (i, o + j))

    ospec = pl.BlockSpec((tr, gw), lambda i, j: (i, j))
    vspec = pl.BlockSpec((None, gw // hd, None, hd, tr),
                         lambda i, j: (i // tiles_per_seq, j, i % tiles_per_seq, 0, 0))
    gspec = pl.BlockSpec((1, gw), lambda i, j: (0, 0))
    shp = jax.ShapeDtypeStruct((t, width), BF16)
    vshp = jax.ShapeDtypeStruct((bsz, width // hd, tiles_per_seq, hd, tr), BF16)
    return pl.pallas_call(
        _qk_prep_kernel,
        out_shape=(shp, shp, vshp),
        grid=(t // tr, nw),
        in_specs=[zspec(col_q), zspec(col_q + width), zspec(col_q + 2 * width), gspec, gspec],
        out_specs=(ospec, ospec, vspec),
        compiler_params=_cparams(("parallel", "parallel")),
        name="qk_norm_prep",
    )(z2, z2, z2, qg, kg)


ATTN_COLS = 256


def _diff_attn_kernel(lam_ref, q_ref, k_ref, vt_ref, g_ref, o_ref, q2_ref, m_ref, l_ref, acc_ref,
                      *, tq, out_scale):
    i = pl.program_id(2)
    hd = 2 * HEAD
    q = q_ref[...]
    lane = lax.broadcasted_iota(jnp.int32, (tq, hd), 1)
    zero = jnp.zeros_like(q)
    q2_ref[0:tq, :] = jnp.where(lane < HEAD, q, zero)
    q2_ref[tq:2 * tq, :] = jnp.where(lane >= HEAD, q, zero)
    m_ref[...] = jnp.full_like(m_ref, -jnp.inf)
    l_ref[...] = jnp.zeros_like(l_ref)
    acc_ref[...] = jnp.zeros_like(acc_ref)

    n_col = 2 * tq // ATTN_COLS

    def run(blocks):
        ks = [k_ref[pl.ds(pl.multiple_of(j * tq, tq), tq), :] for j, _ in blocks]
        vts = [vt_ref[j] for j, _ in blocks]
        items = [(b, c) for b in range(len(blocks)) for c in range(n_col)]

        def scores(item):
            b, c = item
            return _dot(ks[b], q2_ref[c * ATTN_COLS:(c + 1) * ATTN_COLS, :], _NT)

        s_next = scores(items[0])
        for n, (b, c) in enumerate(items):
            cs = slice(c * ATTN_COLS, (c + 1) * ATTN_COLS)
            s = s_next
            if n + 1 < len(items):
                s_next = scores(items[n + 1])
            if blocks[b][1]:
                kpos = lax.broadcasted_iota(jnp.int32, s.shape, 0)
                qpos = (lax.broadcasted_iota(jnp.int32, s.shape, 1) + c * ATTN_COLS) % tq
                s = jnp.where(kpos // CHUNK <= qpos // CHUNK, s, -jnp.inf)
            m_old = m_ref[:, cs]
            m_new = jnp.maximum(m_old, jnp.max(s, axis=0, keepdims=True))
            p = jnp.exp2(s - m_new)
            alpha = jnp.exp2(m_old - m_new)
            l_ref[:, cs] = alpha * l_ref[:, cs] + jnp.sum(p, axis=0, keepdims=True)
            acc_ref[:, cs] = alpha * acc_ref[:, cs] + _dot(vts[b], p.astype(BF16), _NN)
            m_ref[:, cs] = m_new

    def body(jj, carry):
        run([(2 * jj, False), (2 * jj + 1, False)])
        return carry

    lax.fori_loop(0, i // 2, body, 0)

    @pl.when(i % 2 == 1)
    def _():
        run([(i - 1, False), (i, True)])

    @pl.when(i % 2 == 0)
    def _():
        run([(i, True)])

    acc = acc_ref[...] / l_ref[...]
    o = jnp.transpose(acc[:, 0:tq] - lam_ref[0] * acc[:, tq:2 * tq])
    ms = jnp.mean(o * o, axis=-1, keepdims=True)
    o_ref[...] = (o * lax.rsqrt(ms + NORM_EPS) * g_ref[...] * out_scale).astype(o_ref.dtype)


def _diff_attn(q3, k3, vt5, lam, subln_g, out_scale):
    bsz, seq, width = q3.shape
    hd = 2 * HEAD
    nh = width // hd
    nq, tq = vt5.shape[2], vt5.shape[4]
    kern = functools.partial(_diff_attn_kernel, tq=tq, out_scale=out_scale)
    return pl.pallas_call(
        kern,
        out_shape=jax.ShapeDtypeStruct((bsz, seq, width), BF16),
        grid=(bsz, nh, nq),
        in_specs=[pl.BlockSpec(memory_space=pltpu.SMEM),
                  pl.BlockSpec((None, tq, hd), lambda b, h, i: (b, i, h)),
                  pl.BlockSpec((None, seq, hd), lambda b, h, i: (b, 0, h)),
                  pl.BlockSpec((None, None, nq, hd, tq), lambda b, h, i: (b, h, 0, 0, 0)),
                  pl.BlockSpec((1, hd), lambda b, h, i: (0, 0))],
        out_specs=pl.BlockSpec((None, tq, hd), lambda b, h, i: (b, i, h)),
        scratch_shapes=[pltpu.VMEM((2 * tq, hd), BF16), pltpu.VMEM((1, 2 * tq), F32),
                        pltpu.VMEM((1, 2 * tq), F32), pltpu.VMEM((hd, 2 * tq), F32)],
        compiler_params=_cparams(("parallel", "parallel", "arbitrary")),
        name="diff_flash_attention",
    )(lam.astype(F32).reshape(1), q3, k3, vt5, subln_g.reshape(1, hd))


def _layer(x, p_i, lambda_init, norm_mix_g, w_in, rwkv_mu, rwkv_w0, rwkv_w2, rwkv_a0, rwkv_a2, rwkv_g2,
           rwkv_k_k, rwkv_k_a, rwkv_r_k, rwkv_ln_w, rwkv_ln_b, q_norm_g, k_norm_g,
           lam_q1, lam_k1, lam_q2, lam_k2, subln_g, w_branch_a, w_branch_b, w_out,
           norm_ffn_g, w_ffn_in, ffn_conv_w, ffn_conv_b, w_ffn_out, norm_ple_g, w_ple_gate, w_ple_proj):
    bsz, seq, d = x.shape
    t = bsz * seq
    rw = w_branch_a.shape[0]
    dw = w_branch_b.shape[0]
    n_w, n_a, n_g = rwkv_w2.shape[0], rwkv_a2.shape[0], rwkv_g2.shape[0]
    lora_pad = 128 - n_w
    assert n_w <= 128 and n_a == 128 and n_g == 256 and rw % GROUP_W == 0 and dw % GROUP_W == 0
    dff = w_ffn_out.shape[0]
    dff_p = -(-dff // 1024) * 1024

    c_lo = 3 * rw
    c_diff = c_lo + n_w + n_a + n_g
    c_gate = c_diff + 3 * dw
    zeros_pad = jnp.zeros((d, lora_pad), w_in.dtype)
    w_in_p = jnp.concatenate(
        [w_in[:, :c_lo], w_in[:, c_diff:c_gate], w_in[:, c_gate:],
         w_in[:, c_lo:c_lo + n_w], zeros_pad, w_in[:, c_lo + n_w:c_diff]], axis=1).astype(BF16)
    col_q = 3 * rw
    col_ga = col_q + 3 * dw
    col_gb = col_ga + d
    col_l = col_gb + d
    n_in = col_l + 512
    mu_rkv = rwkv_mu[:c_lo].reshape(1, c_lo)
    mu_l = jnp.concatenate([rwkv_mu[c_lo:c_lo + n_w], jnp.zeros((lora_pad,), F32),
                            rwkv_mu[c_lo + n_w:]]).reshape(1, 512)
    w2p = jnp.concatenate([rwkv_w2, jnp.zeros((lora_pad, rw), F32)], axis=0)

    tm, tn, nr = TILES['tm'], TILES['tn'], TILES['norm_rows']
    x2 = x.reshape(t, d)
    h = _rmsnorm(x2, norm_mix_g, nr)
    z = _matmul([(h, w_in_p)], [], _ep_identity, n_in, F32, tm, tn, "in_proj")
    z3 = z.reshape(bsz, seq, n_in)
    row = lambda v_: v_.reshape(1, -1)
    o_a = _rwkv(z3, 0, rw, 2 * rw, col_l, mu_rkv, mu_l, row(rwkv_w0), row(rwkv_a0), w2p, rwkv_a2, rwkv_g2,
                row(rwkv_k_k), row(rwkv_k_a), row(rwkv_r_k), row(rwkv_ln_w), row(rwkv_ln_b), rw,
                TILES['rwkv_rows'])
    qn, kn, vt = _qk_prep(z, col_q, dw, q_norm_g, k_norm_g, bsz, TILES['attn'])
    lam = (jnp.exp(jnp.sum(lam_q1 * lam_k1)) - jnp.exp(jnp.sum(lam_q2 * lam_k2)) + lambda_init)
    shp3 = (bsz, seq, dw)
    o_b = _diff_attn(qn.reshape(shp3), kn.reshape(shp3), vt, lam, subln_g, 1.0 - lambda_init)
    merged = _matmul([(o_a.reshape(t, rw), w_branch_a.astype(BF16)), (o_b.reshape(t, dw), w_branch_b.astype(BF16))],
                     [(z, col_ga // tn), (z, col_gb // tn)], _ep_merge, d, BF16, tm, tn, "branch_merge")
    x2 = _matmul([(merged, w_out.astype(BF16))], [(x2, 0)], _ep_residual, d, F32, tm, tn, "out_proj")
    h = _rmsnorm(x2, norm_ffn_g, nr)
    pad_c = dff_p - dff
    w1 = w_ffn_in.astype(BF16)
    zc = jnp.zeros((d, pad_c), BF16)
    w1p = jnp.concatenate([w1[:, :dff], zc, w1[:, dff:], zc], axis=1)
    zcw = jnp.zeros((ffn_conv_w.shape[0], pad_c), F32)
    cwp = jnp.concatenate([ffn_conv_w[:, :dff], zcw, ffn_conv_w[:, dff:], zcw], axis=1)
    zcb = jnp.zeros((pad_c,), F32)
    cbp = jnp.concatenate([ffn_conv_b[:dff], zcb, ffn_conv_b[dff:], zcb]).reshape(1, 2 * dff_p)
    act = _ffn_in(h, w1p, cwp, cbp, dff_p, seq, tm, tn)
    w2f = jnp.concatenate([w_ffn_out.astype(BF16), jnp.zeros((pad_c, d), BF16)], axis=0)
    x2 = _matmul_acc_residual(act, w2f, x2, tm, TILES['ffn_out_tn'], dff_p // 4, "ffn_out")
    h = _rmsnorm(x2, norm_ple_g, nr)
    pe = p_i.reshape(t, -1).astype(BF16)
    x2 = _matmul([(h, w_ple_gate.astype(BF16)), (pe, w_ple_proj.astype(BF16))], [(x2, 0)], _ep_ple, d, F32,
                 tm, tn, "ple_gate")
    return x2.reshape(bsz, seq, d)


def kernel(x, p, norm_mix_g, w_in, rwkv_mu, rwkv_w0, rwkv_w2, rwkv_a0, rwkv_a2, rwkv_g2, rwkv_k_k, rwkv_k_a,
           rwkv_r_k, rwkv_ln_w, rwkv_ln_b, q_norm_g, k_norm_g, lam_q1, lam_k1, lam_q2, lam_k2, subln_g,
           w_branch_a, w_branch_b, w_out, norm_ffn_g, w_ffn_in, ffn_conv_w, ffn_conv_b, w_ffn_out,
           norm_ple_g, w_ple_gate, w_ple_proj):
    depth = p.shape[0]
    for i in range(depth):
        lambda_init = 0.8 - 0.6 * math.exp(-0.3 * i)
        x = _layer(x, p[i], lambda_init, norm_mix_g[i], w_in[i], rwkv_mu[i], rwkv_w0[i], rwkv_w2[i], rwkv_a0[i],
                   rwkv_a2[i], rwkv_g2[i], rwkv_k_k[i], rwkv_k_a[i], rwkv_r_k[i], rwkv_ln_w[i], rwkv_ln_b[i],
                   q_norm_g[i], k_norm_g[i], lam_q1[i], lam_k1[i], lam_q2[i], lam_k2[i], subln_g[i],
                   w_branch_a[i], w_branch_b[i], w_out[i], norm_ffn_g[i], w_ffn_in[i], ffn_conv_w[i],
                   ffn_conv_b[i], w_ffn_out[i], norm_ple_g[i], w_ple_gate[i], w_ple_proj[i])
    return x
```

```python
import functools
import math

import jax
import jax.numpy as jnp
import numpy as np
from jax import lax
from jax.experimental import pallas as pl
from jax.experimental.pallas import tpu as pltpu

F32 = jnp.float32
BF16 = jnp.bfloat16

NORM_EPS = 1e-6
RWKV_GN_EPS = 64e-5
HEAD = 64
CHUNK = 64
GROUP_W = 256
VMEM_LIMIT = 56 * 1024 * 1024
TILES = dict(tm=1024, tn=512, rwkv_rows=256, attn=512, norm_rows=256, ffn_in_tn=256, ffn_out_tn=512)


def _cparams(sem):
    return pltpu.CompilerParams(dimension_semantics=sem, vmem_limit_bytes=VMEM_LIMIT)


def _dot(a, b, dims):
    return lax.dot_general(a, b, (dims, ((), ())), preferred_element_type=F32)


_NN = ((1,), (0,))
_NT = ((1,), (1,))
_TN = ((0,), (0,))


def _split(x):
    hi = x.astype(BF16)
    lo = (x - hi.astype(F32)).astype(BF16)
    return hi, lo


def _dot3(a, b, dims):
    ah, al = _split(a)
    bh, bl = _split(b)
    return _dot(ah, bh, dims) + (_dot(ah, bl, dims) + _dot(al, bh, dims))


def _dot2_exact_rhs(a, b_bf16, dims):
    ah, al = _split(a)
    return _dot(ah, b_bf16, dims) + _dot(al, b_bf16, dims)


def _rmsnorm_kernel(x_ref, g_ref, o_ref):
    x = x_ref[...]
    ms = jnp.mean(x * x, axis=-1, keepdims=True)
    o_ref[...] = (x * lax.rsqrt(ms + NORM_EPS) * g_ref[...]).astype(o_ref.dtype)


def _rmsnorm(x2d, g, tr=256):
    t, d = x2d.shape
    return pl.pallas_call(
        _rmsnorm_kernel,
        out_shape=jax.ShapeDtypeStruct((t, d), BF16),
        grid=(t // tr,),
        in_specs=[pl.BlockSpec((tr, d), lambda i: (i, 0)), pl.BlockSpec((1, d), lambda i: (0, 0))],
        out_specs=pl.BlockSpec((tr, d), lambda i: (i, 0)),
        compiler_params=_cparams(("parallel",)),
        name="rmsnorm",
    )(x2d, g.reshape(1, d))


def _mm_kernel(*refs, n_dots, n_extra, epilogue):
    out_ref = refs[2 * n_dots + n_extra]
    res = []
    for d in range(n_dots):
        res.append(jnp.dot(refs[2 * d][...], refs[2 * d + 1][...], preferred_element_type=F32))
    extras = [refs[2 * n_dots + e][...] for e in range(n_extra)]
    out_ref[...] = epilogue(res, extras).astype(out_ref.dtype)


def _mm_ws_kernel(*refs, n_dots, n_extra, epilogue):
    out_ref = refs[2 * n_dots + n_extra]
    w_bf = refs[2 * n_dots + n_extra + 1:]

    @pl.when(pl.program_id(1) == 0)
    def _():
        for d in range(n_dots):
            w_bf[d][...] = refs[2 * d + 1][...].astype(BF16)

    res = [jnp.dot(refs[2 * d][...], w_bf[d][...], preferred_element_type=F32) for d in range(n_dots)]
    extras = [refs[2 * n_dots + e][...] for e in range(n_extra)]
    out_ref[...] = epilogue(res, extras).astype(out_ref.dtype)


def _matmul(dots, extras, epilogue, n_out, out_dtype, tm, tn, name):
    m = dots[0][0].shape[0]
    stationary = dots[0][1].dtype == F32
    if stationary:
        ij = lambda f: (lambda j, i: f(i, j))
        grid = (n_out // tn, m // tm)
        kern_fn = _mm_ws_kernel
        scratch = [pltpu.VMEM((a.shape[1], tn), BF16) for a, _ in dots]
        sem = ("arbitrary", "arbitrary")
    else:
        ij = lambda f: f
        grid = (m // tm, n_out // tn)
        kern_fn = _mm_kernel
        scratch = []
        sem = ("parallel", "arbitrary")
    in_specs, args = [], []
    for a, w in dots:
        k = a.shape[1]
        in_specs.append(pl.BlockSpec((tm, k), ij(lambda i, j: (i, 0))))
        in_specs.append(pl.BlockSpec((k, tn), ij(lambda i, j: (0, j))))
        args += [a, w]
    for arr, off in extras:
        in_specs.append(pl.BlockSpec((tm, tn), ij(lambda i, j, off=off: (i, j + off))))
        args.append(arr)
    kern = functools.partial(kern_fn, n_dots=len(dots), n_extra=len(extras), epilogue=epilogue)
    return pl.pallas_call(
        kern,
        out_shape=jax.ShapeDtypeStruct((m, n_out), out_dtype),
        grid=grid,
        in_specs=in_specs,
        out_specs=pl.BlockSpec((tm, tn), ij(lambda i, j: (i, j))),
        scratch_shapes=scratch,
        compiler_params=_cparams(sem),
        name=name,
    )(*args)


def _ep_identity(res, extras):
    return res[0]


def _ep_merge(res, extras):
    return jax.nn.sigmoid(extras[0]) * res[0] + jax.nn.sigmoid(extras[1]) * res[1]


def _ep_residual(res, extras):
    return extras[0] + res[0]


def _ep_ple(res, extras):
    return extras[0] + jax.nn.sigmoid(res[0]) * res[1]


def _mm_acc_kernel(a_ref, w_ref, r_ref, o_ref, acc_ref):
    kk = pl.program_id(2)

    @pl.when(kk == 0)
    def _():
        acc_ref[...] = jnp.zeros_like(acc_ref)

    acc_ref[...] += jnp.dot(a_ref[...], w_ref[...], preferred_element_type=F32)

    @pl.when(kk == pl.num_programs(2) - 1)
    def _():
        o_ref[...] = r_ref[...] + acc_ref[...]


def _matmul_acc_residual(a, w, resid, tm, tn, tk, name):
    m, k = a.shape
    n = w.shape[1]
    return pl.pallas_call(
        _mm_acc_kernel,
        out_shape=jax.ShapeDtypeStruct((m, n), F32),
        grid=(m // tm, n // tn, k // tk),
        in_specs=[pl.BlockSpec((tm, tk), lambda i, j, kk: (i, kk)),
                  pl.BlockSpec((tk, tn), lambda i, j, kk: (kk, j)),
                  pl.BlockSpec((tm, tn), lambda i, j, kk: (i, j))],
        out_specs=pl.BlockSpec((tm, tn), lambda i, j, kk: (i, j)),
        scratch_shapes=[pltpu.VMEM((tm, tn), F32)],
        compiler_params=_cparams(("parallel", "arbitrary", "arbitrary")),
        name=name,
    )(a, w, resid)


def _ffn_in_kernel(a_ref, wg_ref, wu_ref, cwg_ref, cwu_ref, cbg_ref, cbu_ref, o_ref, carry_ref, w_ref,
                   *, tiles_per_seq):
    i = pl.program_id(1)
    tn = wg_ref.shape[1]

    @pl.when(i == 0)
    def _():
        w_ref[:, 0:tn] = wg_ref[...].astype(BF16)
        w_ref[:, tn:2 * tn] = wu_ref[...].astype(BF16)

    @pl.when(i % tiles_per_seq == 0)
    def _():
        carry_ref[...] = jnp.zeros_like(carry_ref)

    a = a_ref[...]
    tm = a.shape[0]
    row = lax.broadcasted_iota(jnp.int32, (tm, 1), 0)

    def conv(u, carry, cw, cb):
        c1 = carry[7:8, :]
        c2 = carry[6:7, :]
        p1 = jnp.where(row == 0, c1, pltpu.roll(u, 1, 0))
        p2 = jnp.where(row == 0, c2, jnp.where(row == 1, c1, pltpu.roll(u, 2, 0)))
        return cb + p2 * cw[0:1, :] + p1 * cw[1:2, :] + u * cw[2:3, :]

    u = jnp.dot(a, w_ref[...], preferred_element_type=F32)
    ug = u[:, 0:tn]
    uu = u[:, tn:2 * tn]
    gate = conv(ug, carry_ref[0], cwg_ref[...], cbg_ref[...])
    up = conv(uu, carry_ref[1], cwu_ref[...], cbu_ref[...])
    carry_ref[0] = ug[tm - 8:tm, :]
    carry_ref[1] = uu[tm - 8:tm, :]
    o_ref[...] = (gate * jax.nn.sigmoid(gate) * up).astype(o_ref.dtype)


def _ffn_in(h, w, conv_w, conv_b, dff, seq, tm, tn):
    t, k = h.shape
    nj = dff // tn
    kern = functools.partial(_ffn_in_kernel, tiles_per_seq=seq // tm)
    return pl.pallas_call(
        kern,
        out_shape=jax.ShapeDtypeStruct((t, dff), BF16),
        grid=(nj, t // tm),
        in_specs=[pl.BlockSpec((tm, k), lambda j, i: (i, 0)),
                  pl.BlockSpec((k, tn), lambda j, i: (0, j)),
                  pl.BlockSpec((k, tn), lambda j, i: (0, j + nj)),
                  pl.BlockSpec((3, tn), lambda j, i: (0, j)),
                  pl.BlockSpec((3, tn), lambda j, i: (0, j + nj)),
                  pl.BlockSpec((1, tn), lambda j, i: (0, j)),
                  pl.BlockSpec((1, tn), lambda j, i: (0, j + nj))],
        out_specs=pl.BlockSpec((tm, tn), lambda j, i: (i, j)),
        scratch_shapes=[pltpu.VMEM((2, 8, tn), F32), pltpu.VMEM((k, 2 * tn), BF16)],
        compiler_params=_cparams(("arbitrary", "arbitrary")),
        name="ffn_in_conv_gate",
    )(h, w, w, conv_w, conv_w, conv_b, conv_b)


def _rwkv_consts():
    r = lax.broadcasted_iota(jnp.int32, (GROUP_W, GROUP_W), 0) // HEAD
    c = lax.broadcasted_iota(jnp.int32, (GROUP_W, GROUP_W), 1) // HEAD
    bd = r == c
    t = lax.broadcasted_iota(jnp.int32, (CHUNK, GROUP_W), 0)
    s = lax.broadcasted_iota(jnp.int32, (CHUNK, GROUP_W), 1) % HEAD
    tt = lax.broadcasted_iota(jnp.int32, (CHUNK, CHUNK), 0)
    ss = lax.broadcasted_iota(jnp.int32, (CHUNK, CHUNK), 1)
    return dict(bd=bd, ones_bd=bd.astype(BF16), strict=s < t, incl=s <= t,
                eye=(s == t).astype(F32), ltri=(ss <= tt).astype(BF16))


def _block_diag(y, c):
    yb = y.astype(BF16)
    return jnp.where(c['bd'], jnp.concatenate([yb, yb, yb, yb], axis=0), jnp.zeros((), BF16))


def _mm(a, b_bf16, dims):
    return _dot(a.astype(BF16), b_bf16, dims)


def _segsum(x, c):
    return _mm(x, c['ones_bd'], _NN)


def _cumsum_rows(l_bf16, x):
    xh, xl = _split(x)
    xl2 = (x - xh.astype(F32) - xl.astype(F32)).astype(BF16)
    return _dot(l_bf16, xh, _NN) + (_dot(l_bf16, xl, _NN) + _dot(l_bf16, xl2, _NN))


def _rwkv_chunks(rs, kraws, vs, wpres, apres, gs, prm, ht, c):
    k_k, k_a, r_k, ln_w, ln_b = prm
    n = len(rs)
    rng = range(n)
    lds = [-np.float32(math.exp(-0.5)) * jax.nn.sigmoid(wpres[i]) for i in rng]
    avs = [jax.nn.sigmoid(apres[i]) for i in rng]
    kk0 = [kraws[i] * k_k for i in rng]
    ssq = [_segsum(kk0[i] * kk0[i], c) for i in rng]
    cums = [_cumsum_rows(c['ltri'], lds[i]) for i in rng]
    kks = [kk0[i] / jnp.maximum(jnp.sqrt(ssq[i]), 1e-12) for i in rng]
    ks = [kraws[i] * (1.0 + (avs[i] - 1.0) * k_a) for i in rng]
    bs = [kks[i] * avs[i] for i in rng]
    last = [cums[i][CHUNK - 1:CHUNK, :] for i in rng]
    dec_out = [jnp.exp(-cums[i]) for i in rng]
    to_end = [jnp.exp(last[i] - cums[i]) for i in rng]
    r_t = [rs[i] * jnp.exp(cums[i]) for i in rng]
    a_t = [-kks[i] * jnp.exp(cums[i] - lds[i]) for i in rng]
    ar = [jnp.concatenate([a_t[i], r_t[i]], axis=0).astype(BF16) for i in rng]
    sb = [_dot(ar[i], _block_diag(bs[i] * dec_out[i], c), _NT) for i in rng]
    sk = [_dot(ar[i], _block_diag(ks[i] * dec_out[i], c), _NT) for i in rng]
    s_ab = [jnp.where(c['strict'], sb[i][:CHUNK], 0.0) for i in rng]
    s_rb = [jnp.where(c['incl'], sb[i][CHUNK:], 0.0).astype(BF16) for i in rng]
    s_ak = [jnp.where(c['strict'], sk[i][:CHUNK], 0.0).astype(BF16) for i in rng]
    s_rk = [jnp.where(c['incl'], sk[i][CHUNK:], 0.0).astype(BF16) for i in rng]
    pw = s_ab
    pw_bd = [_block_diag(pw[i], c) for i in rng]
    inv = [c['eye'] + pw[i] for i in rng]
    for _ in range(5):
        pw = [_mm(pw[i], pw_bd[i], _NN) for i in rng]
        pw_bd = [_block_diag(pw[i], c) for i in rng]
        inv = [inv[i] + _mm(inv[i], pw_bd[i], _NN) for i in rng]
    inv = [inv[i].astype(BF16) for i in rng]
    v_bd = [_block_diag(vs[i], c) for i in rng]
    w = [_dot(inv[i], _block_diag(a_t[i], c), _NN) for i in rng]
    y = [_dot(s_ak[i], v_bd[i], _NN) for i in rng]
    uv = [_dot(inv[i], _block_diag(y[i], c), _NN) for i in rng]
    rw = [(r_t[i] + _dot(s_rb[i], _block_diag(w[i], c), _NN)).astype(BF16) for i in rng]
    ov = [_dot(s_rb[i], _block_diag(uv[i], c), _NN) + _dot(s_rk[i], v_bd[i], _NN) for i in rng]
    bk = [jnp.concatenate([bs[i] * to_end[i], ks[i] * to_end[i]], axis=0).astype(BF16) for i in rng]
    m_st = [jnp.where(c['bd'], _mm(w[i], bk[i][:CHUNK], _TN), 0.0).astype(BF16) for i in rng]
    c_st = [jnp.where(c['bd'], _mm(jnp.concatenate([uv[i], vs[i]], axis=0), bk[i], _TN), 0.0) for i in rng]
    bonus = [_segsum(rs[i] * ks[i] * r_k, c) * vs[i] for i in rng]
    outs = []
    for i in rng:
        hb = ht.astype(BF16)
        outs.append(_dot(rw[i], hb, _NT) + ov[i])
        ht = ht * jnp.exp(last[i]) + _dot(hb, m_st[i], _NN) + c_st[i]
    mean = [_segsum(outs[i], c) * (1.0 / HEAD) for i in rng]
    d = [outs[i] - mean[i] for i in rng]
    var = [_segsum(d[i] * d[i], c) * (1.0 / HEAD) for i in rng]
    res = [(d[i] * lax.rsqrt(var[i] + RWKV_GN_EPS) * ln_w + ln_b + bonus[i]) * gs[i] for i in rng]
    return res, ht


def _rwkv_kernel(zr_ref, zk_ref, zv_ref, zl_ref, mur_ref, muk_ref, muv_ref, mul_ref,
                 w0_ref, a0_ref, w2_ref, a2_ref, g2_ref, kk_ref, ka_ref, rk_ref, lnw_ref, lnb_ref,
                 o_ref, ht_ref, carry_ref, carryl_ref, *, n_chunks):
    step = pl.program_id(2)

    @pl.when(step == 0)
    def _():
        ht_ref[...] = jnp.zeros_like(ht_ref)
        carry_ref[...] = jnp.zeros_like(carry_ref)
        carryl_ref[...] = jnp.zeros_like(carryl_ref)

    rows = zr_ref.shape[0]
    row = lax.broadcasted_iota(jnp.int32, (rows, 1), 0)

    def shifted(z, prev_row, mu):
        prev = jnp.where(row == 0, prev_row, pltpu.roll(z, 1, 0))
        return z + (prev - z) * mu

    zr, zk, zv, zl = zr_ref[...], zk_ref[...], zv_ref[...], zl_ref[...]
    r_all = shifted(zr, carry_ref[0, 0:1, :], mur_ref[...])
    k_all = shifted(zk, carry_ref[1, 0:1, :], muk_ref[...])
    v_all = shifted(zv, carry_ref[2, 0:1, :], muv_ref[...])
    l_all = shifted(zl, carryl_ref[0:1, :], mul_ref[...])
    carry_ref[0] = jnp.broadcast_to(zr[rows - 1:rows, :], (8, GROUP_W))
    carry_ref[1] = jnp.broadcast_to(zk[rows - 1:rows, :], (8, GROUP_W))
    carry_ref[2] = jnp.broadcast_to(zv[rows - 1:rows, :], (8, GROUP_W))
    carryl_ref[...] = jnp.broadcast_to(zl[rows - 1:rows, :], carryl_ref.shape)

    wpre_all = w0_ref[...] + _dot3(jnp.tanh(l_all[:, 0:128]), w2_ref[...], _NN)
    apre_all = a0_ref[...] + _mm(l_all[:, 128:256], a2_ref[...].astype(BF16), _NN)
    g_all = _mm(jax.nn.sigmoid(l_all[:, 256:512]), g2_ref[...].astype(BF16), _NN)

    c = _rwkv_consts()
    prm = (kk_ref[...], ka_ref[...], rk_ref[...], lnw_ref[...], lnb_ref[...])
    sls = [slice(ci * CHUNK, (ci + 1) * CHUNK) for ci in range(n_chunks)]
    pick = lambda x: [x[sl] for sl in sls]
    outs, ht = _rwkv_chunks(pick(r_all), pick(k_all), pick(v_all), pick(wpre_all), pick(apre_all), pick(g_all),
                            prm, ht_ref[...], c)
    for sl, out in zip(sls, outs):
        o_ref[sl, :] = out.astype(o_ref.dtype)
    ht_ref[...] = ht


def _rwkv(z3, col_r, col_k, col_v, col_l, mu_rkv, mu_l, w0, a0, w2p, a2, g2, k_k, k_a, r_k, ln_w, ln_b,
          width, rows):
    bsz, seq, _ = z3.shape
    ng = width // GROUP_W
    gw = GROUP_W

    def zspec(col):
        return pl.BlockSpec((None, rows, gw), lambda b, g, s, o=col // gw: (b, s, o + g))

    def vspec(off=0):
        return pl.BlockSpec((1, gw), lambda b, g, s, o=off: (0, o + g))

    def mspec(k):
        return pl.BlockSpec((k, gw), lambda b, g, s: (0, g))

    in_specs = [zspec(col_r), zspec(col_k), zspec(col_v),
                pl.BlockSpec((None, rows, 512), lambda b, g, s, o=col_l // 512: (b, s, o)),
                vspec(0), vspec(ng), vspec(2 * ng),
                pl.BlockSpec((1, 512), lambda b, g, s: (0, 0)),
                vspec(), vspec(), mspec(128), mspec(128), mspec(256),
                vspec(), vspec(), vspec(), vspec(), vspec()]
    kern = functools.partial(_rwkv_kernel, n_chunks=rows // CHUNK)
    return pl.pallas_call(
        kern,
        out_shape=jax.ShapeDtypeStruct((bsz, seq, width), BF16),
        grid=(bsz, ng, seq // rows),
        in_specs=in_specs,
        out_specs=pl.BlockSpec((None, rows, gw), lambda b, g, s: (b, s, g)),
        scratch_shapes=[pltpu.VMEM((gw, gw), F32), pltpu.VMEM((3, 8, gw), F32), pltpu.VMEM((8, 512), F32)],
        compiler_params=_cparams(("parallel", "parallel", "arbitrary")),
        name="rwkv7_chunked",
    )(z3, z3, z3, z3, mu_rkv, mu_rkv, mu_rkv, mu_l, w0, a0, w2p, a2, g2, k_k, k_a, r_k, ln_w, ln_b)


def _qk_prep_kernel(q_ref, k_ref, v_ref, qg_ref, kg_ref, qo_ref, ko_ref, vo_ref):
    r = lax.broadcasted_iota(jnp.int32, (GROUP_W, GROUP_W), 0) // HEAD
    c = lax.broadcasted_iota(jnp.int32, (GROUP_W, GROUP_W), 1) // HEAD
    ones_bd = (r == c).astype(BF16)

    def norm(x, g):
        ms = _dot2_exact_rhs(x * x, ones_bd, _NN) * (1.0 / HEAD)
        return x * lax.rsqrt(ms + NORM_EPS) * g

    scale = HEAD ** -0.5 * math.log2(math.e)
    qo_ref[...] = (norm(q_ref[...], qg_ref[...]) * scale).astype(BF16)
    ko_ref[...] = norm(k_ref[...], kg_ref[...]).astype(BF16)
    vt = jnp.transpose(v_ref[...]).astype(BF16)
    vo_ref[...] = vt.reshape(vo_ref.shape)


def _qk_prep(z2, col_q, width, q_g, k_g, bsz, tr):
    t = z2.shape[0]
    gw = GROUP_W
    hd = 2 * HEAD
    nw = width // gw
    tiles_per_seq = t // bsz // tr
    qg = jnp.tile(q_g, gw // HEAD).reshape(1, gw)
    kg = jnp.tile(k_g, gw // HEAD).reshape(1, gw)

    def zspec(col):
        return pl.BlockSpec((tr, gw), lambda i, j, o=col // gw: (i, o + j))

    ospec = pl.BlockSpec((tr, gw), lambda i, j: (i, j))
    vspec = pl.BlockSpec((None, gw // hd, None, hd, tr),
                         lambda i, j: (i // tiles_per_seq, j, i % tiles_per_seq, 0, 0))
    gspec = pl.BlockSpec((1, gw), lambda i, j: (0, 0))
    shp = jax.ShapeDtypeStruct((t, width), BF16)
    vshp = jax.ShapeDtypeStruct((bsz, width // hd, tiles_per_seq, hd, tr), BF16)
    return pl.pallas_call(
        _qk_prep_kernel,
        out_shape=(shp, shp, vshp),
        grid=(t // tr, nw),
        in_specs=[zspec(col_q), zspec(col_q + width), zspec(col_q + 2 * width), gspec, gspec],
        out_specs=(ospec, ospec, vspec),
        compiler_params=_cparams(("parallel", "parallel")),
        name="qk_norm_prep",
    )(z2, z2, z2, qg, kg)


ATTN_COLS = 512


def _diff_attn_kernel(lam_ref, q_ref, k_ref, vt_ref, g_ref, o_ref, q2_ref, m_ref, l_ref, acc_ref,
                      *, tq, out_scale):
    i = pl.program_id(2)
    hd = 2 * HEAD
    q = q_ref[...]
    lane = lax.broadcasted_iota(jnp.int32, (tq, hd), 1)
    zero = jnp.zeros_like(q)
    q2_ref[0:tq, :] = jnp.where(lane < HEAD, q, zero)
    q2_ref[tq:2 * tq, :] = jnp.where(lane >= HEAD, q, zero)
    m_ref[...] = jnp.full_like(m_ref, -jnp.inf)
    l_ref[...] = jnp.zeros_like(l_ref)
    acc_ref[...] = jnp.zeros_like(acc_ref)

    n_col = 2 * tq // ATTN_COLS

    def run(blocks):
        ks = [k_ref[pl.ds(pl.multiple_of(j * tq, tq), tq), :] for j, _ in blocks]
        vts = [vt_ref[j] for j, _ in blocks]
        items = [(b, c) for b in range(len(blocks)) for c in range(n_col)]

        def scores(item):
            b, c = item
            return _dot(ks[b], q2_ref[c * ATTN_COLS:(c + 1) * ATTN_COLS, :], _NT)

        s_next = scores(items[0])
        for n, (b, c) in enumerate(items):
            cs = slice(c * ATTN_COLS, (c + 1) * ATTN_COLS)
            s = s_next
            if n + 1 < len(items):
                s_next = scores(items[n + 1])
            if blocks[b][1]:
                kpos = lax.broadcasted_iota(jnp.int32, s.shape, 0)
                qpos = (lax.broadcasted_iota(jnp.int32, s.shape, 1) + c * ATTN_COLS) % tq
                s = jnp.where(kpos // CHUNK <= qpos // CHUNK, s, -jnp.inf)
            m_old = m_ref[:, cs]
            m_new = jnp.maximum(m_old, jnp.max(s, axis=0, keepdims=True))
            p = jnp.exp2(s - m_new)
            alpha = jnp.exp2(m_old - m_new)
            l_ref[:, cs] = alpha * l_ref[:, cs] + jnp.sum(p, axis=0, keepdims=True)
            acc_ref[:, cs] = alpha * acc_ref[:, cs] + _dot(vts[b], p.astype(BF16), _NN)
            m_ref[:, cs] = m_new

    def body(jj, carry):
        run([(2 * jj, False), (2 * jj + 1, False)])
        return carry

    lax.fori_loop(0, i // 2, body, 0)

    @pl.when(i % 2 == 1)
    def _():
        run([(i - 1, False), (i, True)])

    @pl.when(i % 2 == 0)
    def _():
        run([(i, True)])

    acc = acc_ref[...] / l_ref[...]
    o = jnp.transpose(acc[:, 0:tq] - lam_ref[0] * acc[:, tq:2 * tq])
    ms = jnp.mean(o * o, axis=-1, keepdims=True)
    o_ref[...] = (o * lax.rsqrt(ms + NORM_EPS) * g_ref[...] * out_scale).astype(o_ref.dtype)


def _diff_attn(q3, k3, vt5, lam, subln_g, out_scale):
    bsz, seq, width = q3.shape
    hd = 2 * HEAD
    nh = width // hd
    nq, tq = vt5.shape[2], vt5.shape[4]
    kern = functools.partial(_diff_attn_kernel, tq=tq, out_scale=out_scale)
    return pl.pallas_call(
        kern,
        out_shape=jax.ShapeDtypeStruct((bsz, seq, width), BF16),
        grid=(bsz, nh, nq),
        in_specs=[pl.BlockSpec(memory_space=pltpu.SMEM),
                  pl.BlockSpec((None, tq, hd), lambda b, h, i: (b, i, h)),
                  pl.BlockSpec((None, seq, hd), lambda b, h, i: (b, 0, h)),
                  pl.BlockSpec((None, None, nq, hd, tq), lambda b, h, i: (b, h, 0, 0, 0)),
                  pl.BlockSpec((1, hd), lambda b, h, i: (0, 0))],
        out_specs=pl.BlockSpec((None, tq, hd), lambda b, h, i: (b, i, h)),
        scratch_shapes=[pltpu.VMEM((2 * tq, hd), BF16), pltpu.VMEM((1, 2 * tq), F32),
                        pltpu.VMEM((1, 2 * tq), F32), pltpu.VMEM((hd, 2 * tq), F32)],
        compiler_params=_cparams(("parallel", "parallel", "arbitrary")),
        name="diff_flash_attention",
    )(lam.astype(F32).reshape(1), q3, k3, vt5, subln_g.reshape(1, hd))


def _layer(x, p_i, lambda_init, norm_mix_g, w_in, rwkv_mu, rwkv_w0, rwkv_w2, rwkv_a0, rwkv_a2, rwkv_g2,
           rwkv_k_k, rwkv_k_a, rwkv_r_k, rwkv_ln_w, rwkv_ln_b, q_norm_g, k_norm_g,
           lam_q1, lam_k1, lam_q2, lam_k2, subln_g, w_branch_a, w_branch_b, w_out,
           norm_ffn_g, w_ffn_in, ffn_conv_w, ffn_conv_b, w_ffn_out, norm_ple_g, w_ple_gate, w_ple_proj):
    bsz, seq, d = x.shape
    t = bsz * seq
    rw = w_branch_a.shape[0]
    dw = w_branch_b.shape[0]
    n_w, n_a, n_g = rwkv_w2.shape[0], rwkv_a2.shape[0], rwkv_g2.shape[0]
    lora_pad = 128 - n_w
    assert n_w <= 128 and n_a == 128 and n_g == 256 and rw % GROUP_W == 0 and dw % GROUP_W == 0
    dff = w_ffn_out.shape[0]

    c_lo = 3 * rw
    c_diff = c_lo + n_w + n_a + n_g
    c_gate = c_diff + 3 * dw
    zeros_pad = jnp.zeros((d, lora_pad), w_in.dtype)
    w_in_p = jnp.concatenate(
        [w_in[:, :c_lo], w_in[:, c_diff:c_gate], w_in[:, c_gate:],
         w_in[:, c_lo:c_lo + n_w], zeros_pad, w_in[:, c_lo + n_w:c_diff]], axis=1).astype(BF16)
    col_q = 3 * rw
    col_ga = col_q + 3 * dw
    col_gb = col_ga + d
    col_l = col_gb + d
    n_in = col_l + 512
    mu_rkv = rwkv_mu[:c_lo].reshape(1, c_lo)
    mu_l = jnp.concatenate([rwkv_mu[c_lo:c_lo + n_w], jnp.zeros((lora_pad,), F32),
                            rwkv_mu[c_lo + n_w:]]).reshape(1, 512)
    w2p = jnp.concatenate([rwkv_w2, jnp.zeros((lora_pad, rw), F32)], axis=0)

    tm, tn, nr = TILES['tm'], TILES['tn'], TILES['norm_rows']
    x2 = x.reshape(t, d)
    h = _rmsnorm(x2, norm_mix_g, nr)
    z = _matmul([(h, w_in_p)], [], _ep_identity, n_in, F32, tm, tn, "in_proj")
    z3 = z.reshape(bsz, seq, n_in)
    row = lambda v_: v_.reshape(1, -1)
    o_a = _rwkv(z3, 0, rw, 2 * rw, col_l, mu_rkv, mu_l, row(rwkv_w0), row(rwkv_a0), w2p, rwkv_a2, rwkv_g2,
                row(rwkv_k_k), row(rwkv_k_a), row(rwkv_r_k), row(rwkv_ln_w), row(rwkv_ln_b), rw,
                TILES['rwkv_rows'])
    qn, kn, vt = _qk_prep(z, col_q, dw, q_norm_g, k_norm_g, bsz, TILES['attn'])
    lam = (jnp.exp(jnp.sum(lam_q1 * lam_k1)) - jnp.exp(jnp.sum(lam_q2 * lam_k2)) + lambda_init)
    shp3 = (bsz, seq, dw)
    o_b = _diff_attn(qn.reshape(shp3), kn.reshape(shp3), vt, lam, subln_g, 1.0 - lambda_init)
    merged = _matmul([(o_a.reshape(t, rw), w_branch_a), (o_b.reshape(t, dw), w_branch_b)],
                     [(z, col_ga // tn), (z, col_gb // tn)], _ep_merge, d, BF16, tm, tn, "branch_merge")
    x2 = _matmul([(merged, w_out)], [(x2, 0)], _ep_residual, d, F32, tm, tn, "out_proj")
    h = _rmsnorm(x2, norm_ffn_g, nr)
    act = _ffn_in(h, w_ffn_in, ffn_conv_w, ffn_conv_b.reshape(1, 2 * dff), dff, seq, tm, TILES['ffn_in_tn'])
    x2 = _matmul_acc_residual(act, w_ffn_out.astype(BF16), x2, tm, TILES['ffn_out_tn'], dff // 2, "ffn_out")
    h = _rmsnorm(x2, norm_ple_g, nr)
    pe = p_i.reshape(t, -1).astype(BF16)
    x2 = _matmul([(h, w_ple_gate), (pe, w_ple_proj)], [(x2, 0)], _ep_ple, d, F32, tm, tn, "ple_gate")
    return x2.reshape(bsz, seq, d)


def kernel(x, p, norm_mix_g, w_in, rwkv_mu, rwkv_w0, rwkv_w2, rwkv_a0, rwkv_a2, rwkv_g2, rwkv_k_k, rwkv_k_a,
           rwkv_r_k, rwkv_ln_w, rwkv_ln_b, q_norm_g, k_norm_g, lam_q1, lam_k1, lam_q2, lam_k2, subln_g,
           w_branch_a, w_branch_b, w_out, norm_ffn_g, w_ffn_in, ffn_conv_w, ffn_conv_b, w_ffn_out,
           norm_ple_g, w_ple_gate, w_ple_proj):
    depth = p.shape[0]
    for i in range(depth):
        lambda_init = 0.8 - 0.6 * math.exp(-0.3 * i)
        x = _layer(x, p[i], lambda_init, norm_mix_g[i], w_in[i], rwkv_mu[i], rwkv_w0[i], rwkv_w2[i], rwkv_a0[i],
                   rwkv_a2[i], rwkv_g2[i], rwkv_k_k[i], rwkv_k_a[i], rwkv_r_k[i], rwkv_ln_w[i], rwkv_ln_b[i],
                   q_norm_g[i], k_norm_g[i], lam_q1[i], lam_k1[i], lam_q2[i], lam_k2[i], subln_g[i],
                   w_branch_a[i], w_branch_b[i], w_out[i], norm_ffn_g[i], w_ffn_in[i], ffn_conv_w[i],
                   ffn_conv_b[i], w_ffn_out[i], norm_ple_g[i], w_ple_gate[i], w_ple_proj[i])
    return x
```

```python
import functools
import math

import jax
import jax.numpy as jnp
import numpy as np
from jax import lax
from jax.experimental import pallas as pl
from jax.experimental.pallas import tpu as pltpu

F32 = jnp.float32
BF16 = jnp.bfloat16

NORM_EPS = 1e-6
RWKV_GN_EPS = 64e-5
HEAD = 64
CHUNK = 64
GROUP_W = 256
LANES = 128
VMEM_LIMIT = 56 * 1024 * 1024
TILES = dict(tm=1024, tn=512, rwkv_rows=512, attn=512, norm_rows=256, ffn_in_tn=256, ffn_out_tn=512)


def _cparams(sem):
    return pltpu.CompilerParams(dimension_semantics=sem, vmem_limit_bytes=VMEM_LIMIT)


def _dot(a, b, dims):
    return lax.dot_general(a, b, (dims, ((), ())), preferred_element_type=F32)


_NN = ((1,), (0,))
_NT = ((1,), (1,))
_TN = ((0,), (0,))


def _split(x):
    hi = x.astype(BF16)
    lo = (x - hi.astype(F32)).astype(BF16)
    return hi, lo


def _dot3(a, b, dims):
    ah, al = _split(a)
    bh, bl = _split(b)
    return _dot(ah, bh, dims) + (_dot(ah, bl, dims) + _dot(al, bh, dims))


def _dot2_exact_rhs(a, b_bf16, dims):
    ah, al = _split(a)
    return _dot(ah, b_bf16, dims) + _dot(al, b_bf16, dims)


def _rmsnorm_kernel(x_ref, g_ref, o_ref):
    x = x_ref[...]
    ms = jnp.mean(x * x, axis=-1, keepdims=True)
    o_ref[...] = (x * lax.rsqrt(ms + NORM_EPS) * g_ref[...]).astype(o_ref.dtype)


def _rmsnorm(x2d, g, tr=256):
    t, d = x2d.shape
    return pl.pallas_call(
        _rmsnorm_kernel,
        out_shape=jax.ShapeDtypeStruct((t, d), BF16),
        grid=(t // tr,),
        in_specs=[pl.BlockSpec((tr, d), lambda i: (i, 0)), pl.BlockSpec((1, d), lambda i: (0, 0))],
        out_specs=pl.BlockSpec((tr, d), lambda i: (i, 0)),
        compiler_params=_cparams(("parallel",)),
        name="rmsnorm",
    )(x2d, g.reshape(1, d))


def _mm_kernel(*refs, n_dots, n_extra, epilogue):
    out_ref = refs[2 * n_dots + n_extra]
    res = []
    for d in range(n_dots):
        res.append(jnp.dot(refs[2 * d][...], refs[2 * d + 1][...], preferred_element_type=F32))
    extras = [refs[2 * n_dots + e][...] for e in range(n_extra)]
    out_ref[...] = epilogue(res, extras).astype(out_ref.dtype)


def _mm_ws_kernel(*refs, n_dots, n_extra, epilogue):
    out_ref = refs[2 * n_dots + n_extra]
    w_bf = refs[2 * n_dots + n_extra + 1:]

    @pl.when(pl.program_id(1) == 0)
    def _():
        for d in range(n_dots):
            w_bf[d][...] = refs[2 * d + 1][...].astype(BF16)

    res = [jnp.dot(refs[2 * d][...], w_bf[d][...], preferred_element_type=F32) for d in range(n_dots)]
    extras = [refs[2 * n_dots + e][...] for e in range(n_extra)]
    out_ref[...] = epilogue(res, extras).astype(out_ref.dtype)


def _matmul(dots, extras, epilogue, n_out, out_dtype, tm, tn, name):
    m = dots[0][0].shape[0]
    stationary = dots[0][1].dtype == F32
    if stationary:
        ij = lambda f: (lambda j, i: f(i, j))
        grid = (n_out // tn, m // tm)
        kern_fn = _mm_ws_kernel
        scratch = [pltpu.VMEM((a.shape[1], tn), BF16) for a, _ in dots]
        sem = ("arbitrary", "arbitrary")
    else:
        ij = lambda f: f
        grid = (m // tm, n_out // tn)
        kern_fn = _mm_kernel
        scratch = []
        sem = ("parallel", "arbitrary")
    in_specs, args = [], []
    for a, w in dots:
        k = a.shape[1]
        in_specs.append(pl.BlockSpec((tm, k), ij(lambda i, j: (i, 0))))
        in_specs.append(pl.BlockSpec((k, tn), ij(lambda i, j: (0, j))))
        args += [a, w]
    for arr, off in extras:
        in_specs.append(pl.BlockSpec((tm, tn), ij(lambda i, j, off=off: (i, j + off))))
        args.append(arr)
    kern = functools.partial(kern_fn, n_dots=len(dots), n_extra=len(extras), epilogue=epilogue)
    return pl.pallas_call(
        kern,
        out_shape=jax.ShapeDtypeStruct((m, n_out), out_dtype),
        grid=grid,
        in_specs=in_specs,
        out_specs=pl.BlockSpec((tm, tn), ij(lambda i, j: (i, j))),
        scratch_shapes=scratch,
        compiler_params=_cparams(sem),
        name=name,
    )(*args)


def _ep_identity(res, extras):
    return res[0]


def _ep_merge(res, extras):
    return jax.nn.sigmoid(extras[0]) * res[0] + jax.nn.sigmoid(extras[1]) * res[1]


def _ep_residual(res, extras):
    return extras[0] + res[0]


def _ep_ple(res, extras):
    return extras[0] + jax.nn.sigmoid(res[0]) * res[1]


def _mm_acc_kernel(a_ref, w_ref, r_ref, o_ref, acc_ref):
    kk = pl.program_id(2)

    @pl.when(kk == 0)
    def _():
        acc_ref[...] = jnp.zeros_like(acc_ref)

    acc_ref[...] += jnp.dot(a_ref[...], w_ref[...], preferred_element_type=F32)

    @pl.when(kk == pl.num_programs(2) - 1)
    def _():
        o_ref[...] = r_ref[...] + acc_ref[...]


def _matmul_acc_residual(a, w, resid, tm, tn, tk, name):
    m, k = a.shape
    n = w.shape[1]
    return pl.pallas_call(
        _mm_acc_kernel,
        out_shape=jax.ShapeDtypeStruct((m, n), F32),
        grid=(m // tm, n // tn, k // tk),
        in_specs=[pl.BlockSpec((tm, tk), lambda i, j, kk: (i, kk)),
                  pl.BlockSpec((tk, tn), lambda i, j, kk: (kk, j)),
                  pl.BlockSpec((tm, tn), lambda i, j, kk: (i, j))],
        out_specs=pl.BlockSpec((tm, tn), lambda i, j, kk: (i, j)),
        scratch_shapes=[pltpu.VMEM((tm, tn), F32)],
        compiler_params=_cparams(("parallel", "arbitrary", "arbitrary")),
        name=name,
    )(a, w, resid)


def _ffn_in_kernel(a_ref, wg_ref, wu_ref, cwg_ref, cwu_ref, cbg_ref, cbu_ref, o_ref, carry_ref, w_ref,
                   *, tiles_per_seq):
    i = pl.program_id(1)
    tn = wg_ref.shape[1]

    @pl.when(i == 0)
    def _():
        w_ref[:, 0:tn] = wg_ref[...].astype(BF16)
        w_ref[:, tn:2 * tn] = wu_ref[...].astype(BF16)

    @pl.when(i % tiles_per_seq == 0)
    def _():
        carry_ref[...] = jnp.zeros_like(carry_ref)

    a = a_ref[...]
    tm = a.shape[0]
    row = lax.broadcasted_iota(jnp.int32, (tm, 1), 0)

    def conv(u, carry, cw, cb):
        c1 = carry[7:8, :]
        c2 = carry[6:7, :]
        p1 = jnp.where(row == 0, c1, pltpu.roll(u, 1, 0))
        p2 = jnp.where(row == 0, c2, jnp.where(row == 1, c1, pltpu.roll(u, 2, 0)))
        return cb + p2 * cw[0:1, :] + p1 * cw[1:2, :] + u * cw[2:3, :]

    u = jnp.dot(a, w_ref[...], preferred_element_type=F32)
    ug = u[:, 0:tn]
    uu = u[:, tn:2 * tn]
    gate = conv(ug, carry_ref[0], cwg_ref[...], cbg_ref[...])
    up = conv(uu, carry_ref[1], cwu_ref[...], cbu_ref[...])
    carry_ref[0] = ug[tm - 8:tm, :]
    carry_ref[1] = uu[tm - 8:tm, :]
    o_ref[...] = (gate * jax.nn.sigmoid(gate) * up).astype(o_ref.dtype)


def _ffn_in(h, w, conv_w, conv_b, dff, seq, tm, tn):
    t, k = h.shape
    nj = dff // tn
    kern = functools.partial(_ffn_in_kernel, tiles_per_seq=seq // tm)
    return pl.pallas_call(
        kern,
        out_shape=jax.ShapeDtypeStruct((t, dff), BF16),
        grid=(nj, t // tm),
        in_specs=[pl.BlockSpec((tm, k), lambda j, i: (i, 0)),
                  pl.BlockSpec((k, tn), lambda j, i: (0, j)),
                  pl.BlockSpec((k, tn), lambda j, i: (0, j + nj)),
                  pl.BlockSpec((3, tn), lambda j, i: (0, j)),
                  pl.BlockSpec((3, tn), lambda j, i: (0, j + nj)),
                  pl.BlockSpec((1, tn), lambda j, i: (0, j)),
                  pl.BlockSpec((1, tn), lambda j, i: (0, j + nj))],
        out_specs=pl.BlockSpec((tm, tn), lambda j, i: (i, j)),
        scratch_shapes=[pltpu.VMEM((2, 8, tn), F32), pltpu.VMEM((k, 2 * tn), BF16)],
        compiler_params=_cparams(("arbitrary", "arbitrary")),
        name="ffn_in_conv_gate",
    )(h, w, w, conv_w, conv_w, conv_b, conv_b)


def _rwkv_consts():
    r = lax.broadcasted_iota(jnp.int32, (GROUP_W, GROUP_W), 0) // HEAD
    c = lax.broadcasted_iota(jnp.int32, (GROUP_W, GROUP_W), 1) // HEAD
    bd = r == c
    t = lax.broadcasted_iota(jnp.int32, (CHUNK, GROUP_W), 0)
    s = lax.broadcasted_iota(jnp.int32, (CHUNK, GROUP_W), 1) % HEAD
    tt = lax.broadcasted_iota(jnp.int32, (CHUNK, CHUNK), 0)
    ss = lax.broadcasted_iota(jnp.int32, (CHUNK, CHUNK), 1)
    lane_head = lax.broadcasted_iota(jnp.int32, (CHUNK, LANES), 1) // HEAD
    head_lanes = [lane_head == hh for hh in range(LANES // HEAD)]
    return dict(bd=bd, ones_bd=bd.astype(BF16), strict=s < t, incl=s <= t, head_lanes=head_lanes,
                eye=(s == t).astype(F32), ltri=(ss <= tt).astype(BF16))


def _block_diag(y, c):
    yb = y.astype(BF16)
    zero = jnp.zeros((CHUNK, LANES), BF16)
    heads_per_tile = LANES // HEAD
    rows = []
    for h in range(GROUP_W // HEAD):
        tile = h // heads_per_tile
        piece = jnp.where(c['head_lanes'][h % heads_per_tile], yb[:, tile * LANES:(tile + 1) * LANES], zero)
        rows.append(jnp.concatenate([piece if t == tile else zero for t in range(GROUP_W // LANES)], axis=1))
    return jnp.concatenate(rows, axis=0)


def _mm(a, b_bf16, dims):
    return _dot(a.astype(BF16), b_bf16, dims)


def _segsum(x, c):
    return _mm(x, c['ones_bd'], _NN)


def _cumsum_rows(l_bf16, x):
    xh, xl = _split(x)
    return _dot(l_bf16, xh, _NN) + _dot(l_bf16, xl, _NN)


def _rwkv_chunks(rs, kraws, vs, wpres, apres, gs, prm, ht, c):
    k_k, k_a, r_k, ln_w, ln_b = prm
    n = len(rs)
    rng = range(n)
    lds = [-np.float32(math.exp(-0.5)) * jax.nn.sigmoid(wpres[i]) for i in rng]
    avs = [jax.nn.sigmoid(apres[i]) for i in rng]
    kk0 = [kraws[i] * k_k for i in rng]
    ssq = [_segsum(kk0[i] * kk0[i], c) for i in rng]
    cums = [_cumsum_rows(c['ltri'], lds[i]) for i in rng]
    kks = [kk0[i] / jnp.maximum(jnp.sqrt(ssq[i]), 1e-12) for i in rng]
    ks = [kraws[i] * (1.0 + (avs[i] - 1.0) * k_a) for i in rng]
    bs = [kks[i] * avs[i] for i in rng]
    last = [cums[i][CHUNK - 1:CHUNK, :] for i in rng]
    dec_out = [jnp.exp(-cums[i]) for i in rng]
    to_end = [jnp.exp(last[i] - cums[i]) for i in rng]
    r_t = [rs[i] * jnp.exp(cums[i]) for i in rng]
    a_t = [-kks[i] * jnp.exp(cums[i] - lds[i]) for i in rng]
    ar = [jnp.concatenate([a_t[i], r_t[i]], axis=0).astype(BF16) for i in rng]
    sb = [_dot(ar[i], _block_diag(bs[i] * dec_out[i], c), _NT) for i in rng]
    sk = [_dot(ar[i], _block_diag(ks[i] * dec_out[i], c), _NT) for i in rng]
    s_ab = [jnp.where(c['strict'], sb[i][:CHUNK], 0.0) for i in rng]
    s_rb = [jnp.where(c['incl'], sb[i][CHUNK:], 0.0).astype(BF16) for i in rng]
    s_ak = [jnp.where(c['strict'], sk[i][:CHUNK], 0.0).astype(BF16) for i in rng]
    s_rk = [jnp.where(c['incl'], sk[i][CHUNK:], 0.0).astype(BF16) for i in rng]
    pw = s_ab
    pw_bd = [_block_diag(pw[i], c) for i in rng]
    inv = [c['eye'] + pw[i] for i in rng]
    for _ in range(5):
        pw = [_mm(pw[i], pw_bd[i], _NN) for i in rng]
        pw_bd = [_block_diag(pw[i], c) for i in rng]
        inv = [inv[i] + _mm(inv[i], pw_bd[i], _NN) for i in rng]
    inv = [inv[i].astype(BF16) for i in rng]
    v_bd = [_block_diag(vs[i], c) for i in rng]
    w = [_dot(inv[i], _block_diag(a_t[i], c), _NN) for i in rng]
    y = [_dot(s_ak[i], v_bd[i], _NN) for i in rng]
    uv = [_dot(inv[i], _block_diag(y[i], c), _NN) for i in rng]
    rw = [(r_t[i] + _dot(s_rb[i], _block_diag(w[i], c), _NN)).astype(BF16) for i in rng]
    ov = [_dot(s_rb[i], _block_diag(uv[i], c), _NN) + _dot(s_rk[i], v_bd[i], _NN) for i in rng]
    bk = [jnp.concatenate([bs[i] * to_end[i], ks[i] * to_end[i]], axis=0).astype(BF16) for i in rng]
    m_st = [jnp.where(c['bd'], _mm(w[i], bk[i][:CHUNK], _TN), 0.0).astype(BF16) for i in rng]
    c_st = [jnp.where(c['bd'], _mm(jnp.concatenate([uv[i], vs[i]], axis=0), bk[i], _TN), 0.0) for i in rng]
    bonus = [_segsum(rs[i] * ks[i] * r_k, c) * vs[i] for i in rng]
    outs = []
    for i in rng:
        hb = ht.astype(BF16)
        outs.append(_dot(rw[i], hb, _NT) + ov[i])
        ht = ht * jnp.exp(last[i]) + _dot(hb, m_st[i], _NN) + c_st[i]
    mean = [_segsum(outs[i], c) * (1.0 / HEAD) for i in rng]
    d = [outs[i] - mean[i] for i in rng]
    var = [_segsum(d[i] * d[i], c) * (1.0 / HEAD) for i in rng]
    res = [(d[i] * lax.rsqrt(var[i] + RWKV_GN_EPS) * ln_w + ln_b + bonus[i]) * gs[i] for i in rng]
    return res, ht


RWKV_GROUP = 4


def _rwkv_pipeline(rs, kraws, vs, wpres, apres, gs, prm, ht, c):
    k_k, k_a, r_k, ln_w, ln_b = prm
    n = len(rs)
    st = [dict() for _ in range(n)]

    def prepare_sums(i):
        s = st[i]
        s['ld'] = -np.float32(math.exp(-0.5)) * jax.nn.sigmoid(wpres[i])
        s['kk0'] = kraws[i] * k_k
        s['ssq'] = _segsum(s['kk0'] * s['kk0'], c)
        s['cum'] = _cumsum_rows(c['ltri'], s['ld'])

    def prepare(i):
        s = st[i]
        ld, kk0, ssq, cum = s['ld'], s['kk0'], s['ssq'], s['cum']
        av = jax.nn.sigmoid(apres[i])
        kk = kk0 / jnp.maximum(jnp.sqrt(ssq), 1e-12)
        k = kraws[i] * (1.0 + (av - 1.0) * k_a)
        b = kk * av
        last = cum[CHUNK - 1:CHUNK, :]
        dec_out = jnp.exp(-cum)
        to_end = jnp.exp(last - cum)
        r_t = rs[i] * jnp.exp(cum)
        a_t = -kk * jnp.exp(cum - ld)
        ar = jnp.concatenate([a_t, r_t], axis=0).astype(BF16)
        sb = _dot(ar, _block_diag(b * dec_out, c), _NT)
        sk = _dot(ar, _block_diag(k * dec_out, c), _NT)
        s['pw'] = jnp.where(c['strict'], sb[:CHUNK], 0.0)
        s['s_rb'] = jnp.where(c['incl'], sb[CHUNK:], 0.0).astype(BF16)
        s['s_ak'] = jnp.where(c['strict'], sk[:CHUNK], 0.0).astype(BF16)
        s['s_rk'] = jnp.where(c['incl'], sk[CHUNK:], 0.0).astype(BF16)
        s['bk'] = jnp.concatenate([b * to_end, k * to_end], axis=0).astype(BF16)
        s['bonus'] = _segsum(rs[i] * k * r_k, c) * vs[i]
        s['decay'] = jnp.exp(last)
        s['r_t'], s['a_t'] = r_t, a_t

    def inverse(idx):
        pw = [st[i]['pw'] for i in idx]
        pw_bd = [_block_diag(p_, c) for p_ in pw]
        inv = [c['eye'] + p_ for p_ in pw]
        for _ in range(5):
            pw = [_mm(pw[j], pw_bd[j], _NN) for j in range(len(idx))]
            pw_bd = [_block_diag(p_, c) for p_ in pw]
            inv = [inv[j] + _mm(inv[j], pw_bd[j], _NN) for j in range(len(idx))]
        for j, i in enumerate(idx):
            st[i]['inv'] = inv[j].astype(BF16)

    def apply(i):
        s = st[i]
        v_bd = _block_diag(vs[i], c)
        w = _dot(s['inv'], _block_diag(s['a_t'], c), _NN)
        y = _dot(s['s_ak'], v_bd, _NN)
        uv = _dot(s['inv'], _block_diag(y, c), _NN)
        s['rw'] = (s['r_t'] + _dot(s['s_rb'], _block_diag(w, c), _NN)).astype(BF16)
        s['ov'] = _dot(s['s_rb'], _block_diag(uv, c), _NN) + _dot(s['s_rk'], v_bd, _NN)
        s['m_st'] = jnp.where(c['bd'], _mm(w, s['bk'][:CHUNK], _TN), 0.0).astype(BF16)
        s['c_st'] = jnp.where(c['bd'], _mm(jnp.concatenate([uv, vs[i]], axis=0), s['bk'], _TN), 0.0)

    groups = [list(range(n))[j:j + RWKV_GROUP] for j in range(0, n, RWKV_GROUP)]
    for i in groups[0]:
        prepare_sums(i)
    for i in groups[0]:
        prepare(i)
    for gi, grp in enumerate(groups):
        nxt = groups[gi + 1] if gi + 1 < len(groups) else []
        for i in nxt:
            prepare_sums(i)
        inverse(grp)
        for i in nxt:
            prepare(i)
        for i in grp:
            apply(i)
    outs = []
    for i in range(n):
        s = st[i]
        hb = ht.astype(BF16)
        o = _dot(s['rw'], hb, _NT) + s['ov']
        ht = ht * s['decay'] + _dot(hb, s['m_st'], _NN) + s['c_st']
        d = o - _segsum(o, c) * (1.0 / HEAD)
        var = _segsum(d * d, c) * (1.0 / HEAD)
        outs.append((d * lax.rsqrt(var + RWKV_GN_EPS) * ln_w + ln_b + s['bonus']) * gs[i])
    return outs, ht


def _rwkv_kernel(zr_ref, zk_ref, zv_ref, zl_ref, mur_ref, muk_ref, muv_ref, mul_ref,
                 w0_ref, a0_ref, w2_ref, a2_ref, g2_ref, kk_ref, ka_ref, rk_ref, lnw_ref, lnb_ref,
                 o_ref, ht_ref, carry_ref, carryl_ref, *, n_chunks):
    step = pl.program_id(2)

    @pl.when(step == 0)
    def _():
        ht_ref[...] = jnp.zeros_like(ht_ref)
        carry_ref[...] = jnp.zeros_like(carry_ref)
        carryl_ref[...] = jnp.zeros_like(carryl_ref)

    rows = zr_ref.shape[0]
    row = lax.broadcasted_iota(jnp.int32, (rows, 1), 0)

    def shifted(z, prev_row, mu):
        prev = jnp.where(row == 0, prev_row, pltpu.roll(z, 1, 0))
        return z + (prev - z) * mu

    zr, zk, zv, zl = zr_ref[...], zk_ref[...], zv_ref[...], zl_ref[...]
    r_all = shifted(zr, carry_ref[0, 0:1, :], mur_ref[...])
    k_all = shifted(zk, carry_ref[1, 0:1, :], muk_ref[...])
    v_all = shifted(zv, carry_ref[2, 0:1, :], muv_ref[...])
    l_all = shifted(zl, carryl_ref[0:1, :], mul_ref[...])
    carry_ref[0] = jnp.broadcast_to(zr[rows - 1:rows, :], (8, GROUP_W))
    carry_ref[1] = jnp.broadcast_to(zk[rows - 1:rows, :], (8, GROUP_W))
    carry_ref[2] = jnp.broadcast_to(zv[rows - 1:rows, :], (8, GROUP_W))
    carryl_ref[...] = jnp.broadcast_to(zl[rows - 1:rows, :], carryl_ref.shape)

    wpre_all = w0_ref[...] + _mm(jnp.tanh(l_all[:, 0:128]), w2_ref[...].astype(BF16), _NN)
    apre_all = a0_ref[...] + _mm(l_all[:, 128:256], a2_ref[...].astype(BF16), _NN)
    g_all = _mm(jax.nn.sigmoid(l_all[:, 256:512]), g2_ref[...].astype(BF16), _NN)

    c = _rwkv_consts()
    prm = (kk_ref[...], ka_ref[...], rk_ref[...], lnw_ref[...], lnb_ref[...])
    sls = [slice(ci * CHUNK, (ci + 1) * CHUNK) for ci in range(n_chunks)]
    pick = lambda x: [x[sl] for sl in sls]
    outs, ht = _rwkv_chunks(pick(r_all), pick(k_all), pick(v_all), pick(wpre_all), pick(apre_all), pick(g_all),
                            prm, ht_ref[...], c)
    for sl, out in zip(sls, outs):
        o_ref[sl, :] = out.astype(o_ref.dtype)
    ht_ref[...] = ht


def _rwkv(z3, col_r, col_k, col_v, zl3, col_l, mu_rkv, mu_l, w0, a0, w2p, a2, g2, k_k, k_a, r_k, ln_w, ln_b,
          width, rows):
    bsz, seq, _ = z3.shape
    ng = width // GROUP_W
    gw = GROUP_W

    def zspec(col):
        return pl.BlockSpec((None, rows, gw), lambda b, g, s, o=col // gw: (b, s, o + g))

    def vspec(off=0):
        return pl.BlockSpec((1, gw), lambda b, g, s, o=off: (0, o + g))

    def mspec(k):
        return pl.BlockSpec((k, gw), lambda b, g, s: (0, g))

    in_specs = [zspec(col_r), zspec(col_k), zspec(col_v),
                pl.BlockSpec((None, rows, 512), lambda b, g, s, o=col_l // 512: (b, s, o)),
                vspec(0), vspec(ng), vspec(2 * ng),
                pl.BlockSpec((1, 512), lambda b, g, s: (0, 0)),
                vspec(), vspec(), mspec(128), mspec(128), mspec(256),
                vspec(), vspec(), vspec(), vspec(), vspec()]
    kern = functools.partial(_rwkv_kernel, n_chunks=rows // CHUNK)
    return pl.pallas_call(
        kern,
        out_shape=jax.ShapeDtypeStruct((bsz, seq, width), BF16),
        grid=(bsz, ng, seq // rows),
        in_specs=in_specs,
        out_specs=pl.BlockSpec((None, rows, gw), lambda b, g, s: (b, s, g)),
        scratch_shapes=[pltpu.VMEM((gw, gw), F32), pltpu.VMEM((3, 8, gw), F32), pltpu.VMEM((8, 512), F32)],
        compiler_params=_cparams(("parallel", "parallel", "arbitrary")),
        name="rwkv7_chunked",
    )(z3, z3, z3, zl3, mu_rkv, mu_rkv, mu_rkv, mu_l, w0, a0, w2p, a2, g2, k_k, k_a, r_k, ln_w, ln_b)


def _qk_prep_kernel(q_ref, k_ref, v_ref, qg_ref, kg_ref, qo_ref, ko_ref, vo_ref):
    r = lax.broadcasted_iota(jnp.int32, (GROUP_W, GROUP_W), 0) // HEAD
    c = lax.broadcasted_iota(jnp.int32, (GROUP_W, GROUP_W), 1) // HEAD
    ones_bd = (r == c).astype(BF16)

    def norm(x, g):
        ms = _dot2_exact_rhs(x * x, ones_bd, _NN) * (1.0 / HEAD)
        return x * lax.rsqrt(ms + NORM_EPS) * g

    scale = HEAD ** -0.5 * math.log2(math.e)
    qo_ref[...] = (norm(q_ref[...], qg_ref[...]) * scale).astype(BF16)
    ko_ref[...] = norm(k_ref[...], kg_ref[...]).astype(BF16)
    vt = jnp.transpose(v_ref[...]).astype(BF16)
    vo_ref[...] = vt.reshape(vo_ref.shape)


def _qk_prep(z2, col_q, width, q_g, k_g, bsz, tr):
    t = z2.shape[0]
    gw = GROUP_W
    hd = 2 * HEAD
    nw = width // gw
    tiles_per_seq = t // bsz // tr
    qg = jnp.tile(q_g, gw // HEAD).reshape(1, gw)
    kg = jnp.tile(k_g, gw // HEAD).reshape(1, gw)

    def zspec(col):
        return pl.BlockSpec((tr, gw), lambda i, j, o=col // gw: (i, o + j))

    ospec = pl.BlockSpec((tr, gw), lambda i, j: (i, j))
    vspec = pl.BlockSpec((None, gw // hd, None, hd, tr),
                         lambda i, j: (i // tiles_per_seq, j, i % tiles_per_seq, 0, 0))
    gspec = pl.BlockSpec((1, gw), lambda i, j: (0, 0))
    shp = jax.ShapeDtypeStruct((t, width), BF16)
    vshp = jax.ShapeDtypeStruct((bsz, width // hd, tiles_per_seq, hd, tr), BF16)
    return pl.pallas_call(
        _qk_prep_kernel,
        out_shape=(shp, shp, vshp),
        grid=(t // tr, nw),
        in_specs=[zspec(col_q), zspec(col_q + width), zspec(col_q + 2 * width), gspec, gspec],
        out_specs=(ospec, ospec, vspec),
        compiler_params=_cparams(("parallel", "parallel")),
        name="qk_norm_prep",
    )(z2, z2, z2, qg, kg)


ATTN_COLS = 512
ATTN_SUM_ROWS = 16


def _diff_attn_kernel(lam_ref, q_ref, k_ref, vt_ref, g_ref, o_ref, q2_ref, m_ref, acc_ref, *, tq, out_scale):
    i = pl.program_id(2)
    hd = 2 * HEAD
    q = q_ref[...]
    lane = lax.broadcasted_iota(jnp.int32, (tq, hd), 1)
    zero = jnp.zeros_like(q)
    q2_ref[0:tq, :] = jnp.where(lane < HEAD, q, zero)
    q2_ref[tq:2 * tq, :] = jnp.where(lane >= HEAD, q, zero)
    m_ref[...] = jnp.full_like(m_ref, -jnp.inf)
    acc_ref[...] = jnp.zeros_like(acc_ref)

    cols = min(ATTN_COLS, 2 * tq)
    n_col = 2 * tq // cols
    ones_rows = jnp.ones((ATTN_SUM_ROWS, tq), BF16)

    def run(blocks):
        ks = [k_ref[pl.ds(pl.multiple_of(j * tq, tq), tq), :] for j, _ in blocks]
        vts = [jnp.concatenate([vt_ref[j], ones_rows], axis=0) for j, _ in blocks]
        items = [(b, c) for b in range(len(blocks)) for c in range(n_col)]

        def scores(item):
            b, c = item
            return _dot(ks[b], q2_ref[c * cols:(c + 1) * cols, :], _NT)

        s_next = scores(items[0])
        for n, (b, c) in enumerate(items):
            cs = slice(c * cols, (c + 1) * cols)
            s = s_next
            if n + 1 < len(items):
                s_next = scores(items[n + 1])
            if blocks[b][1]:
                kpos = lax.broadcasted_iota(jnp.int32, s.shape, 0)
                qpos = (lax.broadcasted_iota(jnp.int32, s.shape, 1) + c * cols) % tq
                s = jnp.where(kpos // CHUNK <= qpos // CHUNK, s, -jnp.inf)
            m_old = m_ref[:, cs]
            m_new = jnp.maximum(m_old, jnp.max(s, axis=0, keepdims=True))
            p = jnp.exp2((s - m_new).astype(BF16))
            alpha = jnp.exp2(m_old - m_new)
            acc_ref[:, cs] = alpha * acc_ref[:, cs] + _dot(vts[b], p, _NN)
            m_ref[:, cs] = m_new

    def body(jj, carry):
        run([(2 * jj, False), (2 * jj + 1, False)])
        return carry

    lax.fori_loop(0, i // 2, body, 0)

    @pl.when(i % 2 == 1)
    def _():
        run([(i - 1, False), (i, True)])

    @pl.when(i % 2 == 0)
    def _():
        run([(i, True)])

    acc = acc_ref[0:hd, :] / acc_ref[hd:hd + 1, :]
    o = jnp.transpose(acc[:, 0:tq] - lam_ref[0] * acc[:, tq:2 * tq])
    ms = jnp.mean(o * o, axis=-1, keepdims=True)
    o_ref[...] = (o * lax.rsqrt(ms + NORM_EPS) * g_ref[...] * out_scale).astype(o_ref.dtype)


def _diff_attn(q3, k3, vt5, lam, subln_g, out_scale):
    bsz, seq, width = q3.shape
    hd = 2 * HEAD
    nh = width // hd
    nq, tq = vt5.shape[2], vt5.shape[4]
    kern = functools.partial(_diff_attn_kernel, tq=tq, out_scale=out_scale)
    return pl.pallas_call(
        kern,
        out_shape=jax.ShapeDtypeStruct((bsz, seq, width), BF16),
        grid=(bsz, nh, nq),
        in_specs=[pl.BlockSpec(memory_space=pltpu.SMEM),
                  pl.BlockSpec((None, tq, hd), lambda b, h, i: (b, i, h)),
                  pl.BlockSpec((None, seq, hd), lambda b, h, i: (b, 0, h)),
                  pl.BlockSpec((None, None, nq, hd, tq), lambda b, h, i: (b, h, 0, 0, 0)),
                  pl.BlockSpec((1, hd), lambda b, h, i: (0, 0))],
        out_specs=pl.BlockSpec((None, tq, hd), lambda b, h, i: (b, i, h)),
        scratch_shapes=[pltpu.VMEM((2 * tq, hd), BF16), pltpu.VMEM((1, 2 * tq), F32),
                        pltpu.VMEM((hd + ATTN_SUM_ROWS, 2 * tq), F32)],
        compiler_params=_cparams(("parallel", "parallel", "arbitrary")),
        name="diff_flash_attention",
    )(lam.astype(F32).reshape(1), q3, k3, vt5, subln_g.reshape(1, hd))


def _layer(x, p_i, lambda_init, norm_mix_g, w_in, rwkv_mu, rwkv_w0, rwkv_w2, rwkv_a0, rwkv_a2, rwkv_g2,
           rwkv_k_k, rwkv_k_a, rwkv_r_k, rwkv_ln_w, rwkv_ln_b, q_norm_g, k_norm_g,
           lam_q1, lam_k1, lam_q2, lam_k2, subln_g, w_branch_a, w_branch_b, w_out,
           norm_ffn_g, w_ffn_in, ffn_conv_w, ffn_conv_b, w_ffn_out, norm_ple_g, w_ple_gate, w_ple_proj):
    bsz, seq, d = x.shape
    t = bsz * seq
    rw = w_branch_a.shape[0]
    dw = w_branch_b.shape[0]
    n_w, n_a, n_g = rwkv_w2.shape[0], rwkv_a2.shape[0], rwkv_g2.shape[0]
    lora_pad = 128 - n_w
    assert n_w <= 128 and n_a == 128 and n_g == 256 and rw % GROUP_W == 0 and dw % GROUP_W == 0
    dff = w_ffn_out.shape[0]

    c_lo = 3 * rw
    c_diff = c_lo + n_w + n_a + n_g
    zeros_pad = jnp.zeros((d, lora_pad), BF16)
    w_in_b = w_in[:, c_lo:].astype(BF16)
    w_rest = jnp.concatenate(
        [w_in_b[:, c_diff - c_lo:], w_in_b[:, :n_w], zeros_pad, w_in_b[:, n_w:c_diff - c_lo]], axis=1)
    col_q = 0
    col_ga = col_q + 3 * dw
    col_gb = col_ga + d
    col_l = col_gb + d
    n_rest = col_l + 512
    mu_rkv = rwkv_mu[:c_lo].reshape(1, c_lo)
    mu_l = jnp.concatenate([rwkv_mu[c_lo:c_lo + n_w], jnp.zeros((lora_pad,), F32),
                            rwkv_mu[c_lo + n_w:]]).reshape(1, 512)
    w2p = jnp.concatenate([rwkv_w2, jnp.zeros((lora_pad, rw), F32)], axis=0)

    tm, tn, nr = TILES['tm'], TILES['tn'], TILES['norm_rows']
    x2 = x.reshape(t, d)
    h = _rmsnorm(x2, norm_mix_g, nr)
    z_rkv = _matmul([(h, w_in)], [], _ep_identity, c_lo, F32, tm, tn, "in_proj_rkv")
    z = _matmul([(h, w_rest)], [], _ep_identity, n_rest, F32, tm, tn, "in_proj_rest")
    row = lambda v_: v_.reshape(1, -1)
    o_a = _rwkv(z_rkv.reshape(bsz, seq, c_lo), 0, rw, 2 * rw, z.reshape(bsz, seq, n_rest), col_l, mu_rkv, mu_l, row(rwkv_w0), row(rwkv_a0), w2p, rwkv_a2, rwkv_g2,
                row(rwkv_k_k), row(rwkv_k_a), row(rwkv_r_k), row(rwkv_ln_w), row(rwkv_ln_b), rw,
                TILES['rwkv_rows'])
    qn, kn, vt = _qk_prep(z, col_q, dw, q_norm_g, k_norm_g, bsz, TILES['attn'])
    lam = (jnp.exp(jnp.sum(lam_q1 * lam_k1)) - jnp.exp(jnp.sum(lam_q2 * lam_k2)) + lambda_init)
    shp3 = (bsz, seq, dw)
    o_b = _diff_attn(qn.reshape(shp3), kn.reshape(shp3), vt, lam, subln_g, 1.0 - lambda_init)
    merged = _matmul([(o_a.reshape(t, rw), w_branch_a), (o_b.reshape(t, dw), w_branch_b)],
                     [(z, col_ga // tn), (z, col_gb // tn)], _ep_merge, d, BF16, tm, tn, "branch_merge")
    x2 = _matmul([(merged, w_out)], [(x2, 0)], _ep_residual, d, F32, tm, tn, "out_proj")
    h = _rmsnorm(x2, norm_ffn_g, nr)
    act = _ffn_in(h, w_ffn_in, ffn_conv_w, ffn_conv_b.reshape(1, 2 * dff), dff, seq, tm, TILES['ffn_in_tn'])
    x2 = _matmul_acc_residual(act, w_ffn_out.astype(BF16), x2, tm, TILES['ffn_out_tn'], dff // 2, "ffn_out")
    h = _rmsnorm(x2, norm_ple_g, nr)
    pe = p_i.reshape(t, -1).astype(BF16)
    x2 = _matmul([(h, w_ple_gate), (pe, w_ple_proj)], [(x2, 0)], _ep_ple, d, F32, tm, tn, "ple_gate")
    return x2.reshape(bsz, seq, d)


def kernel(x, p, norm_mix_g, w_in, rwkv_mu, rwkv_w0, rwkv_w2, rwkv_a0, rwkv_a2, rwkv_g2, rwkv_k_k, rwkv_k_a,
           rwkv_r_k, rwkv_ln_w, rwkv_ln_b, q_norm_g, k_norm_g, lam_q1, lam_k1, lam_q2, lam_k2, subln_g,
           w_branch_a, w_branch_b, w_out, norm_ffn_g, w_ffn_in, ffn_conv_w, ffn_conv_b, w_ffn_out,
           norm_ple_g, w_ple_gate, w_ple_proj):
    depth = p.shape[0]
    for i in range(depth):
        lambda_init = 0.8 - 0.6 * math.exp(-0.3 * i)
        x = _layer(x, p[i], lambda_init, norm_mix_g[i], w_in[i], rwkv_mu[i], rwkv_w0[i], rwkv_w2[i], rwkv_a0[i],
                   rwkv_a2[i], rwkv_g2[i], rwkv_k_k[i], rwkv_k_a[i], rwkv_r_k[i], rwkv_ln_w[i], rwkv_ln_b[i],
                   q_norm_g[i], k_norm_g[i], lam_q1[i], lam_k1[i], lam_q2[i], lam_k2[i], subln_g[i],
                   w_branch_a[i], w_branch_b[i], w_out[i], norm_ffn_g[i], w_ffn_in[i], ffn_conv_w[i],
                   ffn_conv_b[i], w_ffn_out[i], norm_ple_g[i], w_ple_gate[i], w_ple_proj[i])
    return x
```

```python
import functools
import math

import jax
import jax.numpy as jnp
import numpy as np
from jax import lax
from jax.experimental import pallas as pl
from jax.experimental.pallas import tpu as pltpu

F32 = jnp.float32
BF16 = jnp.bfloat16

NORM_EPS = 1e-6
RWKV_GN_EPS = 64e-5
HEAD = 64
CHUNK = 64
GROUP_W = 256
LANES = 128
VMEM_LIMIT = 56 * 1024 * 1024
TILES = dict(tm=1024, tn=512, rwkv_rows=512, attn=512, norm_rows=256, ffn_in_tn=256, ffn_out_tn=512)


def _cparams(sem):
    return pltpu.CompilerParams(dimension_semantics=sem, vmem_limit_bytes=VMEM_LIMIT)


def _dot(a, b, dims):
    return lax.dot_general(a, b, (dims, ((), ())), preferred_element_type=F32)


_NN = ((1,), (0,))
_NT = ((1,), (1,))
_TN = ((0,), (0,))


def _split(x):
    hi = x.astype(BF16)
    lo = (x - hi.astype(F32)).astype(BF16)
    return hi, lo


def _dot3(a, b, dims):
    ah, al = _split(a)
    bh, bl = _split(b)
    return _dot(ah, bh, dims) + (_dot(ah, bl, dims) + _dot(al, bh, dims))


def _dot2_exact_rhs(a, b_bf16, dims):
    ah, al = _split(a)
    return _dot(ah, b_bf16, dims) + _dot(al, b_bf16, dims)


def _rmsnorm_kernel(x_ref, g_ref, o_ref):
    x = x_ref[...]
    ms = jnp.mean(x * x, axis=-1, keepdims=True)
    o_ref[...] = (x * lax.rsqrt(ms + NORM_EPS) * g_ref[...]).astype(o_ref.dtype)


def _rmsnorm(x2d, g, tr=256):
    t, d = x2d.shape
    return pl.pallas_call(
        _rmsnorm_kernel,
        out_shape=jax.ShapeDtypeStruct((t, d), BF16),
        grid=(t // tr,),
        in_specs=[pl.BlockSpec((tr, d), lambda i: (i, 0)), pl.BlockSpec((1, d), lambda i: (0, 0))],
        out_specs=pl.BlockSpec((tr, d), lambda i: (i, 0)),
        compiler_params=_cparams(("parallel",)),
        name="rmsnorm",
    )(x2d, g.reshape(1, d))


def _regroup_kernel(a_ref, b_ref, o_ref, *, shift):
    x = jnp.concatenate([a_ref[...], b_ref[...]], axis=1)
    width = o_ref.shape[1]
    o_ref[...] = pltpu.roll(x, 2 * width - shift, 1)[:, :width].astype(o_ref.dtype)


def _regroup_weight(w, src_col, n_cols, tr=512, tc=512):
    k = w.shape[0]
    base, shift = divmod(src_col, tc)
    assert n_cols % tc == 0 and k % tr == 0 and 0 < shift and src_col + n_cols <= w.shape[1]
    return pl.pallas_call(
        functools.partial(_regroup_kernel, shift=shift),
        out_shape=jax.ShapeDtypeStruct((k, n_cols), BF16),
        grid=(k // tr, n_cols // tc),
        in_specs=[pl.BlockSpec((tr, tc), lambda i, o: (i, base + o)),
                  pl.BlockSpec((tr, tc), lambda i, o: (i, base + o + 1))],
        out_specs=pl.BlockSpec((tr, tc), lambda i, o: (i, o)),
        compiler_params=_cparams(("parallel", "parallel")),
        name="regroup_weight",
    )(w, w)


def _mm_kernel(*refs, n_dots, n_extra, epilogue):
    out_ref = refs[2 * n_dots + n_extra]
    res = []
    for d in range(n_dots):
        res.append(jnp.dot(refs[2 * d][...], refs[2 * d + 1][...], preferred_element_type=F32))
    extras = [refs[2 * n_dots + e][...] for e in range(n_extra)]
    out_ref[...] = epilogue(res, extras).astype(out_ref.dtype)


def _mm_ws_kernel(*refs, n_dots, n_extra, epilogue):
    out_ref = refs[2 * n_dots + n_extra]
    w_bf = refs[2 * n_dots + n_extra + 1:]

    @pl.when(pl.program_id(1) == 0)
    def _():
        for d in range(n_dots):
            w_bf[d][...] = refs[2 * d + 1][...].astype(BF16)

    res = [jnp.dot(refs[2 * d][...], w_bf[d][...], preferred_element_type=F32) for d in range(n_dots)]
    extras = [refs[2 * n_dots + e][...] for e in range(n_extra)]
    out_ref[...] = epilogue(res, extras).astype(out_ref.dtype)


def _matmul(dots, extras, epilogue, n_out, out_dtype, tm, tn, name):
    m = dots[0][0].shape[0]
    stationary = dots[0][1].dtype == F32
    if stationary:
        ij = lambda f: (lambda j, i: f(i, j))
        grid = (n_out // tn, m // tm)
        kern_fn = _mm_ws_kernel
        scratch = [pltpu.VMEM((a.shape[1], tn), BF16) for a, _ in dots]
        sem = ("arbitrary", "arbitrary")
    else:
        ij = lambda f: f
        grid = (m // tm, n_out // tn)
        kern_fn = _mm_kernel
        scratch = []
        sem = ("parallel", "arbitrary")
    in_specs, args = [], []
    for a, w in dots:
        k = a.shape[1]
        in_specs.append(pl.BlockSpec((tm, k), ij(lambda i, j: (i, 0))))
        in_specs.append(pl.BlockSpec((k, tn), ij(lambda i, j: (0, j))))
        args += [a, w]
    for arr, off in extras:
        in_specs.append(pl.BlockSpec((tm, tn), ij(lambda i, j, off=off: (i, j + off))))
        args.append(arr)
    kern = functools.partial(kern_fn, n_dots=len(dots), n_extra=len(extras), epilogue=epilogue)
    return pl.pallas_call(
        kern,
        out_shape=jax.ShapeDtypeStruct((m, n_out), out_dtype),
        grid=grid,
        in_specs=in_specs,
        out_specs=pl.BlockSpec((tm, tn), ij(lambda i, j: (i, j))),
        scratch_shapes=scratch,
        compiler_params=_cparams(sem),
        name=name,
    )(*args)


def _ep_identity(res, extras):
    return res[0]


def _ep_merge(res, extras):
    return jax.nn.sigmoid(extras[0]) * res[0] + jax.nn.sigmoid(extras[1]) * res[1]


def _ep_residual(res, extras):
    return extras[0] + res[0]


def _ep_ple(res, extras):
    return extras[0] + jax.nn.sigmoid(res[0]) * res[1]


def _mm_acc_kernel(a_ref, w_ref, r_ref, o_ref, acc_ref):
    kk = pl.program_id(2)

    @pl.when(kk == 0)
    def _():
        acc_ref[...] = jnp.zeros_like(acc_ref)

    acc_ref[...] += jnp.dot(a_ref[...], w_ref[...], preferred_element_type=F32)

    @pl.when(kk == pl.num_programs(2) - 1)
    def _():
        o_ref[...] = r_ref[...] + acc_ref[...]


def _matmul_acc_residual(a, w, resid, tm, tn, tk, name):
    m, k = a.shape
    n = w.shape[1]
    return pl.pallas_call(
        _mm_acc_kernel,
        out_shape=jax.ShapeDtypeStruct((m, n), F32),
        grid=(m // tm, n // tn, k // tk),
        in_specs=[pl.BlockSpec((tm, tk), lambda i, j, kk: (i, kk)),
                  pl.BlockSpec((tk, tn), lambda i, j, kk: (kk, j)),
                  pl.BlockSpec((tm, tn), lambda i, j, kk: (i, j))],
        out_specs=pl.BlockSpec((tm, tn), lambda i, j, kk: (i, j)),
        scratch_shapes=[pltpu.VMEM((tm, tn), F32)],
        compiler_params=_cparams(("parallel", "arbitrary", "arbitrary")),
        name=name,
    )(a, w, resid)


def _ffn_in_kernel(a_ref, wg_ref, wu_ref, cwg_ref, cwu_ref, cbg_ref, cbu_ref, o_ref, carry_ref, w_ref,
                   *, tiles_per_seq):
    i = pl.program_id(1)
    tn = wg_ref.shape[1]

    @pl.when(i == 0)
    def _():
        w_ref[:, 0:tn] = wg_ref[...].astype(BF16)
        w_ref[:, tn:2 * tn] = wu_ref[...].astype(BF16)

    @pl.when(i % tiles_per_seq == 0)
    def _():
        carry_ref[...] = jnp.zeros_like(carry_ref)

    a = a_ref[...]
    tm = a.shape[0]
    row = lax.broadcasted_iota(jnp.int32, (tm, 1), 0)

    def conv(u, carry, cw, cb):
        c1 = carry[7:8, :]
        c2 = carry[6:7, :]
        p1 = jnp.where(row == 0, c1, pltpu.roll(u, 1, 0))
        p2 = jnp.where(row == 0, c2, jnp.where(row == 1, c1, pltpu.roll(u, 2, 0)))
        return cb + p2 * cw[0:1, :] + p1 * cw[1:2, :] + u * cw[2:3, :]

    u = jnp.dot(a, w_ref[...], preferred_element_type=F32)
    ug = u[:, 0:tn]
    uu = u[:, tn:2 * tn]
    gate = conv(ug, carry_ref[0], cwg_ref[...], cbg_ref[...])
    up = conv(uu, carry_ref[1], cwu_ref[...], cbu_ref[...])
    carry_ref[0] = ug[tm - 8:tm, :]
    carry_ref[1] = uu[tm - 8:tm, :]
    o_ref[...] = (gate * jax.nn.sigmoid(gate) * up).astype(o_ref.dtype)


def _ffn_in(h, w, conv_w, conv_b, dff, seq, tm, tn):
    t, k = h.shape
    nj = dff // tn
    kern = functools.partial(_ffn_in_kernel, tiles_per_seq=seq // tm)
    return pl.pallas_call(
        kern,
        out_shape=jax.ShapeDtypeStruct((t, dff), BF16),
        grid=(nj, t // tm),
        in_specs=[pl.BlockSpec((tm, k), lambda j, i: (i, 0)),
                  pl.BlockSpec((k, tn), lambda j, i: (0, j)),
                  pl.BlockSpec((k, tn), lambda j, i: (0, j + nj)),
                  pl.BlockSpec((3, tn), lambda j, i: (0, j)),
                  pl.BlockSpec((3, tn), lambda j, i: (0, j + nj)),
                  pl.BlockSpec((1, tn), lambda j, i: (0, j)),
                  pl.BlockSpec((1, tn), lambda j, i: (0, j + nj))],
        out_specs=pl.BlockSpec((tm, tn), lambda j, i: (i, j)),
        scratch_shapes=[pltpu.VMEM((2, 8, tn), F32), pltpu.VMEM((k, 2 * tn), BF16)],
        compiler_params=_cparams(("arbitrary", "arbitrary")),
        name="ffn_in_conv_gate",
    )(h, w, w, conv_w, conv_w, conv_b, conv_b)


def _rwkv_consts():
    r = lax.broadcasted_iota(jnp.int32, (GROUP_W, GROUP_W), 0) // HEAD
    c = lax.broadcasted_iota(jnp.int32, (GROUP_W, GROUP_W), 1) // HEAD
    bd = r == c
    t = lax.broadcasted_iota(jnp.int32, (CHUNK, GROUP_W), 0)
    s = lax.broadcasted_iota(jnp.int32, (CHUNK, GROUP_W), 1) % HEAD
    tt = lax.broadcasted_iota(jnp.int32, (CHUNK, CHUNK), 0)
    ss = lax.broadcasted_iota(jnp.int32, (CHUNK, CHUNK), 1)
    lane_head = lax.broadcasted_iota(jnp.int32, (CHUNK, LANES), 1) // HEAD
    head_lanes = [lane_head == hh for hh in range(LANES // HEAD)]
    return dict(bd=bd, ones_bd=bd.astype(BF16), strict=s < t, incl=s <= t, head_lanes=head_lanes,
                eye=(s == t).astype(F32), ltri=(ss <= tt).astype(BF16))


def _block_diag(y, c):
    yb = y.astype(BF16)
    zero = jnp.zeros((CHUNK, LANES), BF16)
    heads_per_tile = LANES // HEAD
    rows = []
    for h in range(GROUP_W // HEAD):
        tile = h // heads_per_tile
        piece = jnp.where(c['head_lanes'][h % heads_per_tile], yb[:, tile * LANES:(tile + 1) * LANES], zero)
        rows.append(jnp.concatenate([piece if t == tile else zero for t in range(GROUP_W // LANES)], axis=1))
    return jnp.concatenate(rows, axis=0)


def _mm(a, b_bf16, dims):
    return _dot(a.astype(BF16), b_bf16, dims)


def _segsum(x, c):
    return _mm(x, c['ones_bd'], _NN)


def _cumsum_rows(l_bf16, x):
    xh, xl = _split(x)
    return _dot(l_bf16, xh, _NN) + _dot(l_bf16, xl, _NN)


def _rwkv_chunks(rs, kraws, vs, wpres, apres, gs, prm, ht, c):
    k_k, k_a, r_k, ln_w, ln_b = prm
    n = len(rs)
    rng = range(n)
    lds = [-np.float32(math.exp(-0.5)) * jax.nn.sigmoid(wpres[i]) for i in rng]
    avs = [jax.nn.sigmoid(apres[i]) for i in rng]
    kk0 = [kraws[i] * k_k for i in rng]
    ssq = [_segsum(kk0[i] * kk0[i], c) for i in rng]
    cums = [_cumsum_rows(c['ltri'], lds[i]) for i in rng]
    kks = [kk0[i] / jnp.maximum(jnp.sqrt(ssq[i]), 1e-12) for i in rng]
    ks = [kraws[i] * (1.0 + (avs[i] - 1.0) * k_a) for i in rng]
    bs = [kks[i] * avs[i] for i in rng]
    last = [cums[i][CHUNK - 1:CHUNK, :] for i in rng]
    dec_out = [jnp.exp(-cums[i]) for i in rng]
    to_end = [jnp.exp(last[i] - cums[i]) for i in rng]
    r_t = [rs[i] * jnp.exp(cums[i]) for i in rng]
    a_t = [-kks[i] * jnp.exp(cums[i] - lds[i]) for i in rng]
    ar = [jnp.concatenate([a_t[i], r_t[i]], axis=0).astype(BF16) for i in rng]
    sb = [_dot(ar[i], _block_diag(bs[i] * dec_out[i], c), _NT) for i in rng]
    sk = [_dot(ar[i], _block_diag(ks[i] * dec_out[i], c), _NT) for i in rng]
    s_ab = [jnp.where(c['strict'], sb[i][:CHUNK], 0.0) for i in rng]
    s_rb = [jnp.where(c['incl'], sb[i][CHUNK:], 0.0).astype(BF16) for i in rng]
    s_ak = [jnp.where(c['strict'], sk[i][:CHUNK], 0.0).astype(BF16) for i in rng]
    s_rk = [jnp.where(c['incl'], sk[i][CHUNK:], 0.0).astype(BF16) for i in rng]
    pw = s_ab
    pw_bd = [_block_diag(pw[i], c) for i in rng]
    inv = [c['eye'] + pw[i] for i in rng]
    for _ in range(5):
        pw = [_mm(pw[i], pw_bd[i], _NN) for i in rng]
        pw_bd = [_block_diag(pw[i], c) for i in rng]
        inv = [inv[i] + _mm(inv[i], pw_bd[i], _NN) for i in rng]
    inv = [inv[i].astype(BF16) for i in rng]
    v_bd = [_block_diag(vs[i], c) for i in rng]
    w = [_dot(inv[i], _block_diag(a_t[i], c), _NN) for i in rng]
    y = [_dot(s_ak[i], v_bd[i], _NN) for i in rng]
    uv = [_dot(inv[i], _block_diag(y[i], c), _NN) for i in rng]
    rw = [(r_t[i] + _dot(s_rb[i], _block_diag(w[i], c), _NN)).astype(BF16) for i in rng]
    ov = [_dot(s_rb[i], _block_diag(uv[i], c), _NN) + _dot(s_rk[i], v_bd[i], _NN) for i in rng]
    bk = [jnp.concatenate([bs[i] * to_end[i], ks[i] * to_end[i]], axis=0).astype(BF16) for i in rng]
    m_st = [jnp.where(c['bd'], _mm(w[i], bk[i][:CHUNK], _TN), 0.0).astype(BF16) for i in rng]
    c_st = [jnp.where(c['bd'], _mm(jnp.concatenate([uv[i], vs[i]], axis=0), bk[i], _TN), 0.0) for i in rng]
    bonus = [_segsum(rs[i] * ks[i] * r_k, c) * vs[i] for i in rng]
    outs = []
    for i in rng:
        hb = ht.astype(BF16)
        outs.append(_dot(rw[i], hb, _NT) + ov[i])
        ht = ht * jnp.exp(last[i]) + _dot(hb, m_st[i], _NN) + c_st[i]
    mean = [_segsum(outs[i], c) * (1.0 / HEAD) for i in rng]
    d = [outs[i] - mean[i] for i in rng]
    var = [_segsum(d[i] * d[i], c) * (1.0 / HEAD) for i in rng]
    res = [(d[i] * lax.rsqrt(var[i] + RWKV_GN_EPS) * ln_w + ln_b + bonus[i]) * gs[i] for i in rng]
    return res, ht


RWKV_GROUP = 4


def _rwkv_pipeline(rs, kraws, vs, wpres, apres, gs, prm, ht, c):
    k_k, k_a, r_k, ln_w, ln_b = prm
    n = len(rs)
    st = [dict() for _ in range(n)]

    def prepare_sums(i):
        s = st[i]
        s['ld'] = -np.float32(math.exp(-0.5)) * jax.nn.sigmoid(wpres[i])
        s['kk0'] = kraws[i] * k_k
        s['ssq'] = _segsum(s['kk0'] * s['kk0'], c)
        s['cum'] = _cumsum_rows(c['ltri'], s['ld'])

    def prepare(i):
        s = st[i]
        ld, kk0, ssq, cum = s['ld'], s['kk0'], s['ssq'], s['cum']
        av = jax.nn.sigmoid(apres[i])
        kk = kk0 / jnp.maximum(jnp.sqrt(ssq), 1e-12)
        k = kraws[i] * (1.0 + (av - 1.0) * k_a)
        b = kk * av
        last = cum[CHUNK - 1:CHUNK, :]
        dec_out = jnp.exp(-cum)
        to_end = jnp.exp(last - cum)
        r_t = rs[i] * jnp.exp(cum)
        a_t = -kk * jnp.exp(cum - ld)
        ar = jnp.concatenate([a_t, r_t], axis=0).astype(BF16)
        sb = _dot(ar, _block_diag(b * dec_out, c), _NT)
        sk = _dot(ar, _block_diag(k * dec_out, c), _NT)
        s['pw'] = jnp.where(c['strict'], sb[:CHUNK], 0.0)
        s['s_rb'] = jnp.where(c['incl'], sb[CHUNK:], 0.0).astype(BF16)
        s['s_ak'] = jnp.where(c['strict'], sk[:CHUNK], 0.0).astype(BF16)
        s['s_rk'] = jnp.where(c['incl'], sk[CHUNK:], 0.0).astype(BF16)
        s['bk'] = jnp.concatenate([b * to_end, k * to_end], axis=0).astype(BF16)
        s['bonus'] = _segsum(rs[i] * k * r_k, c) * vs[i]
        s['decay'] = jnp.exp(last)
        s['r_t'], s['a_t'] = r_t, a_t

    def inverse(idx):
        pw = [st[i]['pw'] for i in idx]
        pw_bd = [_block_diag(p_, c) for p_ in pw]
        inv = [c['eye'] + p_ for p_ in pw]
        for _ in range(5):
            pw = [_mm(pw[j], pw_bd[j], _NN) for j in range(len(idx))]
            pw_bd = [_block_diag(p_, c) for p_ in pw]
            inv = [inv[j] + _mm(inv[j], pw_bd[j], _NN) for j in range(len(idx))]
        for j, i in enumerate(idx):
            st[i]['inv'] = inv[j].astype(BF16)

    def apply(i):
        s = st[i]
        v_bd = _block_diag(vs[i], c)
        w = _dot(s['inv'], _block_diag(s['a_t'], c), _NN)
        y = _dot(s['s_ak'], v_bd, _NN)
        uv = _dot(s['inv'], _block_diag(y, c), _NN)
        s['rw'] = (s['r_t'] + _dot(s['s_rb'], _block_diag(w, c), _NN)).astype(BF16)
        s['ov'] = _dot(s['s_rb'], _block_diag(uv, c), _NN) + _dot(s['s_rk'], v_bd, _NN)
        s['m_st'] = jnp.where(c['bd'], _mm(w, s['bk'][:CHUNK], _TN), 0.0).astype(BF16)
        s['c_st'] = jnp.where(c['bd'], _mm(jnp.concatenate([uv, vs[i]], axis=0), s['bk'], _TN), 0.0)

    groups = [list(range(n))[j:j + RWKV_GROUP] for j in range(0, n, RWKV_GROUP)]
    for i in groups[0]:
        prepare_sums(i)
    for i in groups[0]:
        prepare(i)
    for gi, grp in enumerate(groups):
        nxt = groups[gi + 1] if gi + 1 < len(groups) else []
        for i in nxt:
            prepare_sums(i)
        inverse(grp)
        for i in nxt:
            prepare(i)
        for i in grp:
            apply(i)
    outs = []
    for i in range(n):
        s = st[i]
        hb = ht.astype(BF16)
        o = _dot(s['rw'], hb, _NT) + s['ov']
        ht = ht * s['decay'] + _dot(hb, s['m_st'], _NN) + s['c_st']
        d = o - _segsum(o, c) * (1.0 / HEAD)
        var = _segsum(d * d, c) * (1.0 / HEAD)
        outs.append((d * lax.rsqrt(var + RWKV_GN_EPS) * ln_w + ln_b + s['bonus']) * gs[i])
    return outs, ht


def _rwkv_kernel(zr_ref, zk_ref, zv_ref, zl_ref, mur_ref, muk_ref, muv_ref, mul_ref,
                 w0_ref, a0_ref, w2_ref, a2_ref, g2_ref, kk_ref, ka_ref, rk_ref, lnw_ref, lnb_ref,
                 o_ref, ht_ref, carry_ref, carryl_ref, *, n_chunks):
    step = pl.program_id(2)

    @pl.when(step == 0)
    def _():
        ht_ref[...] = jnp.zeros_like(ht_ref)
        carry_ref[...] = jnp.zeros_like(carry_ref)
        carryl_ref[...] = jnp.zeros_like(carryl_ref)

    rows = zr_ref.shape[0]
    row = lax.broadcasted_iota(jnp.int32, (rows, 1), 0)

    def shifted(z, prev_row, mu):
        prev = jnp.where(row == 0, prev_row, pltpu.roll(z, 1, 0))
        return z + (prev - z) * mu

    zr, zk, zv, zl = zr_ref[...], zk_ref[...], zv_ref[...], zl_ref[...]
    r_all = shifted(zr, carry_ref[0, 0:1, :], mur_ref[...])
    k_all = shifted(zk, carry_ref[1, 0:1, :], muk_ref[...])
    v_all = shifted(zv, carry_ref[2, 0:1, :], muv_ref[...])
    l_all = shifted(zl, carryl_ref[0:1, :], mul_ref[...])
    carry_ref[0] = jnp.broadcast_to(zr[rows - 1:rows, :], (8, GROUP_W))
    carry_ref[1] = jnp.broadcast_to(zk[rows - 1:rows, :], (8, GROUP_W))
    carry_ref[2] = jnp.broadcast_to(zv[rows - 1:rows, :], (8, GROUP_W))
    carryl_ref[...] = jnp.broadcast_to(zl[rows - 1:rows, :], carryl_ref.shape)

    wpre_all = w0_ref[...] + _mm(jnp.tanh(l_all[:, 0:128]), w2_ref[...].astype(BF16), _NN)
    apre_all = a0_ref[...] + _mm(l_all[:, 128:256], a2_ref[...].astype(BF16), _NN)
    g_all = _mm(jax.nn.sigmoid(l_all[:, 256:512]), g2_ref[...].astype(BF16), _NN)

    c = _rwkv_consts()
    prm = (kk_ref[...], ka_ref[...], rk_ref[...], lnw_ref[...], lnb_ref[...])
    sls = [slice(ci * CHUNK, (ci + 1) * CHUNK) for ci in range(n_chunks)]
    pick = lambda x: [x[sl] for sl in sls]
    outs, ht = _rwkv_chunks(pick(r_all), pick(k_all), pick(v_all), pick(wpre_all), pick(apre_all), pick(g_all),
                            prm, ht_ref[...], c)
    for sl, out in zip(sls, outs):
        o_ref[sl, :] = out.astype(o_ref.dtype)
    ht_ref[...] = ht


def _rwkv(z3, col_r, col_k, col_v, zl3, col_l, mu_rkv, mu_l, w0, a0, w2p, a2, g2, k_k, k_a, r_k, ln_w, ln_b,
          width, rows):
    bsz, seq, _ = z3.shape
    ng = width // GROUP_W
    gw = GROUP_W

    def zspec(col):
        return pl.BlockSpec((None, rows, gw), lambda b, g, s, o=col // gw: (b, s, o + g))

    def vspec(off=0):
        return pl.BlockSpec((1, gw), lambda b, g, s, o=off: (0, o + g))

    def mspec(k):
        return pl.BlockSpec((k, gw), lambda b, g, s: (0, g))

    in_specs = [zspec(col_r), zspec(col_k), zspec(col_v),
                pl.BlockSpec((None, rows, 512), lambda b, g, s, o=col_l // 512: (b, s, o)),
                vspec(0), vspec(ng), vspec(2 * ng),
                pl.BlockSpec((1, 512), lambda b, g, s: (0, 0)),
                vspec(), vspec(), mspec(128), mspec(128), mspec(256),
                vspec(), vspec(), vspec(), vspec(), vspec()]
    kern = functools.partial(_rwkv_kernel, n_chunks=rows // CHUNK)
    return pl.pallas_call(
        kern,
        out_shape=jax.ShapeDtypeStruct((bsz, seq, width), BF16),
        grid=(bsz, ng, seq // rows),
        in_specs=in_specs,
        out_specs=pl.BlockSpec((None, rows, gw), lambda b, g, s: (b, s, g)),
        scratch_shapes=[pltpu.VMEM((gw, gw), F32), pltpu.VMEM((3, 8, gw), F32), pltpu.VMEM((8, 512), F32)],
        compiler_params=_cparams(("parallel", "parallel", "arbitrary")),
        name="rwkv7_chunked",
    )(z3, z3, z3, zl3, mu_rkv, mu_rkv, mu_rkv, mu_l, w0, a0, w2p, a2, g2, k_k, k_a, r_k, ln_w, ln_b)


def _qk_prep_kernel(q_ref, k_ref, v_ref, qg_ref, kg_ref, qo_ref, ko_ref, vo_ref):
    r = lax.broadcasted_iota(jnp.int32, (GROUP_W, GROUP_W), 0) // HEAD
    c = lax.broadcasted_iota(jnp.int32, (GROUP_W, GROUP_W), 1) // HEAD
    ones_bd = (r == c).astype(BF16)

    def norm(x, g):
        ms = _dot2_exact_rhs(x * x, ones_bd, _NN) * (1.0 / HEAD)
        return x * lax.rsqrt(ms + NORM_EPS) * g

    scale = HEAD ** -0.5 * math.log2(math.e)
    qo_ref[...] = (norm(q_ref[...], qg_ref[...]) * scale).astype(BF16)
    ko_ref[...] = norm(k_ref[...], kg_ref[...]).astype(BF16)
    vt = jnp.transpose(v_ref[...]).astype(BF16)
    vo_ref[...] = vt.reshape(vo_ref.shape)


def _qk_prep(z2, col_q, width, q_g, k_g, bsz, tr):
    t = z2.shape[0]
    gw = GROUP_W
    hd = 2 * HEAD
    nw = width // gw
    tiles_per_seq = t // bsz // tr
    qg = jnp.tile(q_g, gw // HEAD).reshape(1, gw)
    kg = jnp.tile(k_g, gw // HEAD).reshape(1, gw)

    def zspec(col):
        return pl.BlockSpec((tr, gw), lambda i, j, o=col // gw: (i, o + j))

    ospec = pl.BlockSpec((tr, gw), lambda i, j: (i, j))
    vspec = pl.BlockSpec((None, gw // hd, None, hd, tr),
                         lambda i, j: (i // tiles_per_seq, j, i % tiles_per_seq, 0, 0))
    gspec = pl.BlockSpec((1, gw), lambda i, j: (0, 0))
    shp = jax.ShapeDtypeStruct((t, width), BF16)
    vshp = jax.ShapeDtypeStruct((bsz, width // hd, tiles_per_seq, hd, tr), BF16)
    return pl.pallas_call(
        _qk_prep_kernel,
        out_shape=(shp, shp, vshp),
        grid=(t // tr, nw),
        in_specs=[zspec(col_q), zspec(col_q + width), zspec(col_q + 2 * width), gspec, gspec],
        out_specs=(ospec, ospec, vspec),
        compiler_params=_cparams(("parallel", "parallel")),
        name="qk_norm_prep",
    )(z2, z2, z2, qg, kg)


ATTN_COLS = 512
ATTN_SUM_ROWS = 16


def _diff_attn_kernel(lam_ref, q_ref, k_ref, vt_ref, g_ref, o_ref, q2_ref, m_ref, acc_ref, *, tq, out_scale):
    i = pl.program_id(2)
    hd = 2 * HEAD
    q = q_ref[...]
    lane = lax.broadcasted_iota(jnp.int32, (tq, hd), 1)
    zero = jnp.zeros_like(q)
    q2_ref[0:tq, :] = jnp.where(lane < HEAD, q, zero)
    q2_ref[tq:2 * tq, :] = jnp.where(lane >= HEAD, q, zero)
    m_ref[...] = jnp.full_like(m_ref, -jnp.inf)
    acc_ref[...] = jnp.zeros_like(acc_ref)

    cols = min(ATTN_COLS, 2 * tq)
    n_col = 2 * tq // cols
    ones_rows = jnp.ones((ATTN_SUM_ROWS, tq), BF16)

    def run(blocks):
        ks = [k_ref[pl.ds(pl.multiple_of(j * tq, tq), tq), :] for j, _ in blocks]
        vts = [jnp.concatenate([vt_ref[j], ones_rows], axis=0) for j, _ in blocks]
        items = [(b, c) for b in range(len(blocks)) for c in range(n_col)]

        def scores(item):
            b, c = item
            return _dot(ks[b], q2_ref[c * cols:(c + 1) * cols, :], _NT)

        s_next = scores(items[0])
        for n, (b, c) in enumerate(items):
            cs = slice(c * cols, (c + 1) * cols)
            s = s_next
            if n + 1 < len(items):
                s_next = scores(items[n + 1])
            if blocks[b][1]:
                kpos = lax.broadcasted_iota(jnp.int32, s.shape, 0)
                qpos = (lax.broadcasted_iota(jnp.int32, s.shape, 1) + c * cols) % tq
                s = jnp.where(kpos // CHUNK <= qpos // CHUNK, s, -jnp.inf)
            m_old = m_ref[:, cs]
            m_new = jnp.maximum(m_old, jnp.max(s, axis=0, keepdims=True))
            p = jnp.exp2((s - m_new).astype(BF16))
            alpha = jnp.exp2(m_old - m_new)
            acc_ref[:, cs] = alpha * acc_ref[:, cs] + _dot(vts[b], p, _NN)
            m_ref[:, cs] = m_new

    def body(jj, carry):
        run([(2 * jj, False), (2 * jj + 1, False)])
        return carry

    lax.fori_loop(0, i // 2, body, 0)

    @pl.when(i % 2 == 1)
    def _():
        run([(i - 1, False), (i, True)])

    @pl.when(i % 2 == 0)
    def _():
        run([(i, True)])

    acc = acc_ref[0:hd, :] / acc_ref[hd:hd + 1, :]
    o = jnp.transpose(acc[:, 0:tq] - lam_ref[0] * acc[:, tq:2 * tq])
    ms = jnp.mean(o * o, axis=-1, keepdims=True)
    o_ref[...] = (o * lax.rsqrt(ms + NORM_EPS) * g_ref[...] * out_scale).astype(o_ref.dtype)


def _diff_attn(q3, k3, vt5, lam, subln_g, out_scale):
    bsz, seq, width = q3.shape
    hd = 2 * HEAD
    nh = width // hd
    nq, tq = vt5.shape[2], vt5.shape[4]
    kern = functools.partial(_diff_attn_kernel, tq=tq, out_scale=out_scale)
    return pl.pallas_call(
        kern,
        out_shape=jax.ShapeDtypeStruct((bsz, seq, width), BF16),
        grid=(bsz, nh, nq),
        in_specs=[pl.BlockSpec(memory_space=pltpu.SMEM),
                  pl.BlockSpec((None, tq, hd), lambda b, h, i: (b, i, h)),
                  pl.BlockSpec((None, seq, hd), lambda b, h, i: (b, 0, h)),
                  pl.BlockSpec((None, None, nq, hd, tq), lambda b, h, i: (b, h, 0, 0, 0)),
                  pl.BlockSpec((1, hd), lambda b, h, i: (0, 0))],
        out_specs=pl.BlockSpec((None, tq, hd), lambda b, h, i: (b, i, h)),
        scratch_shapes=[pltpu.VMEM((2 * tq, hd), BF16), pltpu.VMEM((1, 2 * tq), F32),
                        pltpu.VMEM((hd + ATTN_SUM_ROWS, 2 * tq), F32)],
        compiler_params=_cparams(("parallel", "parallel", "arbitrary")),
        name="diff_flash_attention",
    )(lam.astype(F32).reshape(1), q3, k3, vt5, subln_g.reshape(1, hd))


def _layer(x, p_i, lambda_init, norm_mix_g, w_in, rwkv_mu, rwkv_w0, rwkv_w2, rwkv_a0, rwkv_a2, rwkv_g2,
           rwkv_k_k, rwkv_k_a, rwkv_r_k, rwkv_ln_w, rwkv_ln_b, q_norm_g, k_norm_g,
           lam_q1, lam_k1, lam_q2, lam_k2, subln_g, w_branch_a, w_branch_b, w_out,
           norm_ffn_g, w_ffn_in, ffn_conv_w, ffn_conv_b, w_ffn_out, norm_ple_g, w_ple_gate, w_ple_proj):
    bsz, seq, d = x.shape
    t = bsz * seq
    rw = w_branch_a.shape[0]
    dw = w_branch_b.shape[0]
    n_w, n_a, n_g = rwkv_w2.shape[0], rwkv_a2.shape[0], rwkv_g2.shape[0]
    lora_pad = 128 - n_w
    assert n_w <= 128 and n_a == 128 and n_g == 256 and rw % GROUP_W == 0 and dw % GROUP_W == 0
    dff = w_ffn_out.shape[0]

    c_lo = 3 * rw
    c_diff = c_lo + n_w + n_a + n_g
    col_q = 0
    col_ga = col_q + 3 * dw
    col_gb = col_ga + d
    n_rest = col_gb + d
    w_rest = _regroup_weight(w_in, c_diff, n_rest)
    w_lora = jnp.concatenate([w_in[:, c_lo:c_lo + n_w], jnp.zeros((d, lora_pad), F32),
                              w_in[:, c_lo + n_w:c_diff]], axis=1)
    mu_rkv = rwkv_mu[:c_lo].reshape(1, c_lo)
    mu_l = jnp.concatenate([rwkv_mu[c_lo:c_lo + n_w], jnp.zeros((lora_pad,), F32),
                            rwkv_mu[c_lo + n_w:]]).reshape(1, 512)
    w2p = jnp.concatenate([rwkv_w2, jnp.zeros((lora_pad, rw), F32)], axis=0)

    tm, tn, nr = TILES['tm'], TILES['tn'], TILES['norm_rows']
    x2 = x.reshape(t, d)
    h = _rmsnorm(x2, norm_mix_g, nr)
    z_rkv = _matmul([(h, w_in)], [], _ep_identity, c_lo, F32, tm, tn, "in_proj_rkv")
    z = _matmul([(h, w_rest)], [], _ep_identity, n_rest, F32, tm, tn, "in_proj_rest")
    z_l = _matmul([(h, w_lora)], [], _ep_identity, 512, F32, tm, 512, "in_proj_lora")
    row = lambda v_: v_.reshape(1, -1)
    o_a = _rwkv(z_rkv.reshape(bsz, seq, c_lo), 0, rw, 2 * rw, z_l.reshape(bsz, seq, 512), 0, mu_rkv, mu_l, row(rwkv_w0), row(rwkv_a0), w2p, rwkv_a2, rwkv_g2,
                row(rwkv_k_k), row(rwkv_k_a), row(rwkv_r_k), row(rwkv_ln_w), row(rwkv_ln_b), rw,
                TILES['rwkv_rows'])
    qn, kn, vt = _qk_prep(z, col_q, dw, q_norm_g, k_norm_g, bsz, TILES['attn'])
    lam = (jnp.exp(jnp.sum(lam_q1 * lam_k1)) - jnp.exp(jnp.sum(lam_q2 * lam_k2)) + lambda_init)
    shp3 = (bsz, seq, dw)
    o_b = _diff_attn(qn.reshape(shp3), kn.reshape(shp3), vt, lam, subln_g, 1.0 - lambda_init)
    merged = _matmul([(o_a.reshape(t, rw), w_branch_a), (o_b.reshape(t, dw), w_branch_b)],
                     [(z, col_ga // tn), (z, col_gb // tn)], _ep_merge, d, BF16, tm, tn, "branch_merge")
    x2 = _matmul([(merged, w_out)], [(x2, 0)], _ep_residual, d, F32, tm, tn, "out_proj")
    h = _rmsnorm(x2, norm_ffn_g, nr)
    act = _ffn_in(h, w_ffn_in, ffn_conv_w, ffn_conv_b.reshape(1, 2 * dff), dff, seq, tm, TILES['ffn_in_tn'])
    x2 = _matmul_acc_residual(act, w_ffn_out.astype(BF16), x2, tm, TILES['ffn_out_tn'], dff // 2, "ffn_out")
    h = _rmsnorm(x2, norm_ple_g, nr)
    pe = p_i.reshape(t, -1).astype(BF16)
    x2 = _matmul([(h, w_ple_gate), (pe, w_ple_proj)], [(x2, 0)], _ep_ple, d, F32, tm, tn, "ple_gate")
    return x2.reshape(bsz, seq, d)


def kernel(x, p, norm_mix_g, w_in, rwkv_mu, rwkv_w0, rwkv_w2, rwkv_a0, rwkv_a2, rwkv_g2, rwkv_k_k, rwkv_k_a,
           rwkv_r_k, rwkv_ln_w, rwkv_ln_b, q_norm_g, k_norm_g, lam_q1, lam_k1, lam_q2, lam_k2, subln_g,
           w_branch_a, w_branch_b, w_out, norm_ffn_g, w_ffn_in, ffn_conv_w, ffn_conv_b, w_ffn_out,
           norm_ple_g, w_ple_gate, w_ple_proj):
    depth = p.shape[0]
    for i in range(depth):
        lambda_init = 0.8 - 0.6 * math.exp(-0.3 * i)
        x = _layer(x, p[i], lambda_init, norm_mix_g[i], w_in[i], rwkv_mu[i], rwkv_w0[i], rwkv_w2[i], rwkv_a0[i],
                   rwkv_a2[i], rwkv_g2[i], rwkv_k_k[i], rwkv_k_a[i], rwkv_r_k[i], rwkv_ln_w[i], rwkv_ln_b[i],
                   q_norm_g[i], k_norm_g[i], lam_q1[i], lam_k1[i], lam_q2[i], lam_k2[i], subln_g[i],
                   w_branch_a[i], w_branch_b[i], w_out[i], norm_ffn_g[i], w_ffn_in[i], ffn_conv_w[i],
                   ffn_conv_b[i], w_ffn_out[i], norm_ple_g[i], w_ple_gate[i], w_ple_proj[i])
    return x
```

```python
import functools
import math

import jax
import jax.numpy as jnp
import numpy as np
from jax import lax
from jax.experimental import pallas as pl
from jax.experimental.pallas import tpu as pltpu

F32 = jnp.float32
BF16 = jnp.bfloat16

NORM_EPS = 1e-6
RWKV_GN_EPS = 64e-5
HEAD = 64
CHUNK = 64
GROUP_W = 256
LANES = 128
VMEM_LIMIT = 56 * 1024 * 1024
TILES = dict(tm=1024, tn=512, rwkv_rows=512, attn=512, norm_rows=256, ffn_in_tn=256, ffn_out_tn=512)


def _cparams(sem):
    return pltpu.CompilerParams(dimension_semantics=sem, vmem_limit_bytes=VMEM_LIMIT)


def _dot(a, b, dims):
    return lax.dot_general(a, b, (dims, ((), ())), preferred_element_type=F32)


_NN = ((1,), (0,))
_NT = ((1,), (1,))
_TN = ((0,), (0,))


def _split(x):
    hi = x.astype(BF16)
    lo = (x - hi.astype(F32)).astype(BF16)
    return hi, lo


def _dot3(a, b, dims):
    ah, al = _split(a)
    bh, bl = _split(b)
    return _dot(ah, bh, dims) + (_dot(ah, bl, dims) + _dot(al, bh, dims))


def _dot2_exact_rhs(a, b_bf16, dims):
    ah, al = _split(a)
    return _dot(ah, b_bf16, dims) + _dot(al, b_bf16, dims)


def _rmsnorm_kernel(x_ref, g_ref, o_ref):
    x = x_ref[...]
    ms = jnp.mean(x * x, axis=-1, keepdims=True)
    o_ref[...] = (x * lax.rsqrt(ms + NORM_EPS) * g_ref[...]).astype(o_ref.dtype)


def _rmsnorm(x2d, g, tr=256):
    t, d = x2d.shape
    return pl.pallas_call(
        _rmsnorm_kernel,
        out_shape=jax.ShapeDtypeStruct((t, d), BF16),
        grid=(t // tr,),
        in_specs=[pl.BlockSpec((tr, d), lambda i: (i, 0)), pl.BlockSpec((1, d), lambda i: (0, 0))],
        out_specs=pl.BlockSpec((tr, d), lambda i: (i, 0)),
        compiler_params=_cparams(("parallel",)),
        name="rmsnorm",
    )(x2d, g.reshape(1, d))


def _regroup_kernel(a_ref, b_ref, o_ref, *, shift):
    x = jnp.concatenate([a_ref[...], b_ref[...]], axis=1)
    width = o_ref.shape[1]
    o_ref[...] = pltpu.roll(x, 2 * width - shift, 1)[:, :width].astype(o_ref.dtype)


def _regroup_weight(w, src_col, n_cols, tr=512, tc=512):
    k = w.shape[0]
    base, shift = divmod(src_col, tc)
    assert n_cols % tc == 0 and k % tr == 0 and 0 < shift and src_col + n_cols <= w.shape[1]
    return pl.pallas_call(
        functools.partial(_regroup_kernel, shift=shift),
        out_shape=jax.ShapeDtypeStruct((k, n_cols), BF16),
        grid=(k // tr, n_cols // tc),
        in_specs=[pl.BlockSpec((tr, tc), lambda i, o: (i, base + o)),
                  pl.BlockSpec((tr, tc), lambda i, o: (i, base + o + 1))],
        out_specs=pl.BlockSpec((tr, tc), lambda i, o: (i, o)),
        compiler_params=_cparams(("parallel", "parallel")),
        name="regroup_weight",
    )(w, w)


def _mm_kernel(*refs, n_dots, n_extra, epilogue):
    out_ref = refs[2 * n_dots + n_extra]
    res = []
    for d in range(n_dots):
        res.append(jnp.dot(refs[2 * d][...], refs[2 * d + 1][...], preferred_element_type=F32))
    extras = [refs[2 * n_dots + e][...] for e in range(n_extra)]
    out_ref[...] = epilogue(res, extras).astype(out_ref.dtype)


def _mm_ws_kernel(*refs, n_dots, n_extra, epilogue):
    out_ref = refs[2 * n_dots + n_extra]
    w_bf = refs[2 * n_dots + n_extra + 1:]

    @pl.when(pl.program_id(1) == 0)
    def _():
        for d in range(n_dots):
            w_bf[d][...] = refs[2 * d + 1][...].astype(BF16)

    res = [jnp.dot(refs[2 * d][...], w_bf[d][...], preferred_element_type=F32) for d in range(n_dots)]
    extras = [refs[2 * n_dots + e][...] for e in range(n_extra)]
    out_ref[...] = epilogue(res, extras).astype(out_ref.dtype)


def _matmul(dots, extras, epilogue, n_out, out_dtype, tm, tn, name):
    m = dots[0][0].shape[0]
    stationary = dots[0][1].dtype == F32
    if stationary:
        ij = lambda f: (lambda j, i: f(i, j))
        grid = (n_out // tn, m // tm)
        kern_fn = _mm_ws_kernel
        scratch = [pltpu.VMEM((a.shape[1], tn), BF16) for a, _ in dots]
        sem = ("arbitrary", "arbitrary")
    else:
        ij = lambda f: f
        grid = (m // tm, n_out // tn)
        kern_fn = _mm_kernel
        scratch = []
        sem = ("parallel", "arbitrary")
    in_specs, args = [], []
    for a, w in dots:
        k = a.shape[1]
        in_specs.append(pl.BlockSpec((tm, k), ij(lambda i, j: (i, 0))))
        in_specs.append(pl.BlockSpec((k, tn), ij(lambda i, j: (0, j))))
        args += [a, w]
    for arr, off in extras:
        in_specs.append(pl.BlockSpec((tm, tn), ij(lambda i, j, off=off: (i, j + off))))
        args.append(arr)
    kern = functools.partial(kern_fn, n_dots=len(dots), n_extra=len(extras), epilogue=epilogue)
    return pl.pallas_call(
        kern,
        out_shape=jax.ShapeDtypeStruct((m, n_out), out_dtype),
        grid=grid,
        in_specs=in_specs,
        out_specs=pl.BlockSpec((tm, tn), ij(lambda i, j: (i, j))),
        scratch_shapes=scratch,
        compiler_params=_cparams(sem),
        name=name,
    )(*args)


def _mm_wt_kernel(a_ref, wt_ref, o_ref, w_bf, *, pad_at, pad_rows):
    @pl.when(pl.program_id(1) == 0)
    def _():
        wt = wt_ref[...]
        if pad_rows:
            tn = wt.shape[0]
            wt = jnp.concatenate([wt[:pad_at], jnp.zeros((pad_rows, wt.shape[1]), F32),
                                  wt[pad_at:tn - pad_rows]], axis=0)
        w_bf[...] = jnp.transpose(wt).astype(BF16)

    o_ref[...] = jnp.dot(a_ref[...], w_bf[...], preferred_element_type=F32).astype(o_ref.dtype)


def _matmul_wt(a, wt, row_start, n_out, out_dtype, tm, tn, name, pad_at=0, pad_rows=0):
    m, k = a.shape
    return pl.pallas_call(
        functools.partial(_mm_wt_kernel, pad_at=pad_at, pad_rows=pad_rows),
        out_shape=jax.ShapeDtypeStruct((m, n_out), out_dtype),
        grid=(n_out // tn, m // tm),
        in_specs=[pl.BlockSpec((tm, k), lambda j, i: (i, 0)),
                  pl.BlockSpec((pl.Element(tn), pl.Element(k)), lambda j, i: (row_start(j), 0))],
        out_specs=pl.BlockSpec((tm, tn), lambda j, i: (i, j)),
        scratch_shapes=[pltpu.VMEM((k, tn), BF16)],
        compiler_params=_cparams(("arbitrary", "arbitrary")),
        name=name,
    )(a, wt)


def _ep_identity(res, extras):
    return res[0]


def _ep_merge(res, extras):
    return jax.nn.sigmoid(extras[0]) * res[0] + jax.nn.sigmoid(extras[1]) * res[1]


def _ep_residual(res, extras):
    return extras[0] + res[0]


def _ep_ple(res, extras):
    return extras[0] + jax.nn.sigmoid(res[0]) * res[1]


def _mm_acc_kernel(a_ref, w_ref, r_ref, o_ref, acc_ref):
    kk = pl.program_id(2)

    @pl.when(kk == 0)
    def _():
        acc_ref[...] = jnp.zeros_like(acc_ref)

    acc_ref[...] += jnp.dot(a_ref[...], w_ref[...], preferred_element_type=F32)

    @pl.when(kk == pl.num_programs(2) - 1)
    def _():
        o_ref[...] = r_ref[...] + acc_ref[...]


def _matmul_acc_residual(a, w, resid, tm, tn, tk, name):
    m, k = a.shape
    n = w.shape[1]
    return pl.pallas_call(
        _mm_acc_kernel,
        out_shape=jax.ShapeDtypeStruct((m, n), F32),
        grid=(m // tm, n // tn, k // tk),
        in_specs=[pl.BlockSpec((tm, tk), lambda i, j, kk: (i, kk)),
                  pl.BlockSpec((tk, tn), lambda i, j, kk: (kk, j)),
                  pl.BlockSpec((tm, tn), lambda i, j, kk: (i, j))],
        out_specs=pl.BlockSpec((tm, tn), lambda i, j, kk: (i, j)),
        scratch_shapes=[pltpu.VMEM((tm, tn), F32)],
        compiler_params=_cparams(("parallel", "arbitrary", "arbitrary")),
        name=name,
    )(a, w, resid)


def _ffn_in_kernel(a_ref, wg_ref, wu_ref, cwg_ref, cwu_ref, cbg_ref, cbu_ref, o_ref, carry_ref, w_ref,
                   *, tiles_per_seq):
    i = pl.program_id(1)
    tn = wg_ref.shape[1]

    @pl.when(i == 0)
    def _():
        w_ref[:, 0:tn] = wg_ref[...].astype(BF16)
        w_ref[:, tn:2 * tn] = wu_ref[...].astype(BF16)

    @pl.when(i % tiles_per_seq == 0)
    def _():
        carry_ref[...] = jnp.zeros_like(carry_ref)

    a = a_ref[...]
    tm = a.shape[0]
    row = lax.broadcasted_iota(jnp.int32, (tm, 1), 0)

    def conv(u, carry, cw, cb):
        c1 = carry[7:8, :]
        c2 = carry[6:7, :]
        p1 = jnp.where(row == 0, c1, pltpu.roll(u, 1, 0))
        p2 = jnp.where(row == 0, c2, jnp.where(row == 1, c1, pltpu.roll(u, 2, 0)))
        return cb + p2 * cw[0:1, :] + p1 * cw[1:2, :] + u * cw[2:3, :]

    u = jnp.dot(a, w_ref[...], preferred_element_type=F32)
    ug = u[:, 0:tn]
    uu = u[:, tn:2 * tn]
    gate = conv(ug, carry_ref[0], cwg_ref[...], cbg_ref[...])
    up = conv(uu, carry_ref[1], cwu_ref[...], cbu_ref[...])
    carry_ref[0] = ug[tm - 8:tm, :]
    carry_ref[1] = uu[tm - 8:tm, :]
    o_ref[...] = (gate * jax.nn.sigmoid(gate) * up).astype(o_ref.dtype)


def _ffn_in(h, w, conv_w, conv_b, dff, seq, tm, tn):
    t, k = h.shape
    nj = dff // tn
    kern = functools.partial(_ffn_in_kernel, tiles_per_seq=seq // tm)
    return pl.pallas_call(
        kern,
        out_shape=jax.ShapeDtypeStruct((t, dff), BF16),
        grid=(nj, t // tm),
        in_specs=[pl.BlockSpec((tm, k), lambda j, i: (i, 0)),
                  pl.BlockSpec((k, tn), lambda j, i: (0, j)),
                  pl.BlockSpec((k, tn), lambda j, i: (0, j + nj)),
                  pl.BlockSpec((3, tn), lambda j, i: (0, j)),
                  pl.BlockSpec((3, tn), lambda j, i: (0, j + nj)),
                  pl.BlockSpec((1, tn), lambda j, i: (0, j)),
                  pl.BlockSpec((1, tn), lambda j, i: (0, j + nj))],
        out_specs=pl.BlockSpec((tm, tn), lambda j, i: (i, j)),
        scratch_shapes=[pltpu.VMEM((2, 8, tn), F32), pltpu.VMEM((k, 2 * tn), BF16)],
        compiler_params=_cparams(("arbitrary", "arbitrary")),
        name="ffn_in_conv_gate",
    )(h, w, w, conv_w, conv_w, conv_b, conv_b)


def _rwkv_consts():
    r = lax.broadcasted_iota(jnp.int32, (GROUP_W, GROUP_W), 0) // HEAD
    c = lax.broadcasted_iota(jnp.int32, (GROUP_W, GROUP_W), 1) // HEAD
    bd = r == c
    t = lax.broadcasted_iota(jnp.int32, (CHUNK, GROUP_W), 0)
    s = lax.broadcasted_iota(jnp.int32, (CHUNK, GROUP_W), 1) % HEAD
    tt = lax.broadcasted_iota(jnp.int32, (CHUNK, CHUNK), 0)
    ss = lax.broadcasted_iota(jnp.int32, (CHUNK, CHUNK), 1)
    lane_head = lax.broadcasted_iota(jnp.int32, (CHUNK, LANES), 1) // HEAD
    head_lanes = [lane_head == hh for hh in range(LANES // HEAD)]
    return dict(bd=bd, ones_bd=bd.astype(BF16), strict=s < t, incl=s <= t, head_lanes=head_lanes,
                eye=(s == t).astype(F32), ltri=(ss <= tt).astype(BF16))


def _block_diag(y, c):
    yb = y.astype(BF16)
    zero = jnp.zeros((CHUNK, LANES), BF16)
    heads_per_tile = LANES // HEAD
    rows = []
    for h in range(GROUP_W // HEAD):
        tile = h // heads_per_tile
        piece = jnp.where(c['head_lanes'][h % heads_per_tile], yb[:, tile * LANES:(tile + 1) * LANES], zero)
        rows.append(jnp.concatenate([piece if t == tile else zero for t in range(GROUP_W // LANES)], axis=1))
    return jnp.concatenate(rows, axis=0)


def _mm(a, b_bf16, dims):
    return _dot(a.astype(BF16), b_bf16, dims)


def _segsum(x, c):
    return _mm(x, c['ones_bd'], _NN)


def _cumsum_rows(l_bf16, x):
    xh, xl = _split(x)
    return _dot(l_bf16, xh, _NN) + _dot(l_bf16, xl, _NN)


def _rwkv_chunks(rs, kraws, vs, wpres, apres, gs, prm, ht, c):
    k_k, k_a, r_k, ln_w, ln_b = prm
    n = len(rs)
    rng = range(n)
    lds = [-np.float32(math.exp(-0.5)) * jax.nn.sigmoid(wpres[i]) for i in rng]
    avs = [jax.nn.sigmoid(apres[i]) for i in rng]
    kk0 = [kraws[i] * k_k for i in rng]
    ssq = [_segsum(kk0[i] * kk0[i], c) for i in rng]
    cums = [_cumsum_rows(c['ltri'], lds[i]) for i in rng]
    kks = [kk0[i] / jnp.maximum(jnp.sqrt(ssq[i]), 1e-12) for i in rng]
    ks = [kraws[i] * (1.0 + (avs[i] - 1.0) * k_a) for i in rng]
    bs = [kks[i] * avs[i] for i in rng]
    last = [cums[i][CHUNK - 1:CHUNK, :] for i in rng]
    dec_out = [jnp.exp(-cums[i]) for i in rng]
    to_end = [jnp.exp(last[i] - cums[i]) for i in rng]
    r_t = [rs[i] * jnp.exp(cums[i]) for i in rng]
    a_t = [-kks[i] * jnp.exp(cums[i] - lds[i]) for i in rng]
    ar = [jnp.concatenate([a_t[i], r_t[i]], axis=0).astype(BF16) for i in rng]
    sb = [_dot(ar[i], _block_diag(bs[i] * dec_out[i], c), _NT) for i in rng]
    sk = [_dot(ar[i], _block_diag(ks[i] * dec_out[i], c), _NT) for i in rng]
    s_ab = [jnp.where(c['strict'], sb[i][:CHUNK], 0.0) for i in rng]
    s_rb = [jnp.where(c['incl'], sb[i][CHUNK:], 0.0).astype(BF16) for i in rng]
    s_ak = [jnp.where(c['strict'], sk[i][:CHUNK], 0.0).astype(BF16) for i in rng]
    s_rk = [jnp.where(c['incl'], sk[i][CHUNK:], 0.0).astype(BF16) for i in rng]
    pw = s_ab
    pw_bd = [_block_diag(pw[i], c) for i in rng]
    inv = [c['eye'] + pw[i] for i in rng]
    for _ in range(5):
        pw = [_mm(pw[i], pw_bd[i], _NN) for i in rng]
        pw_bd = [_block_diag(pw[i], c) for i in rng]
        inv = [inv[i] + _mm(inv[i], pw_bd[i], _NN) for i in rng]
    inv = [inv[i].astype(BF16) for i in rng]
    v_bd = [_block_diag(vs[i], c) for i in rng]
    w = [_dot(inv[i], _block_diag(a_t[i], c), _NN) for i in rng]
    y = [_dot(s_ak[i], v_bd[i], _NN) for i in rng]
    uv = [_dot(inv[i], _block_diag(y[i], c), _NN) for i in rng]
    rw = [(r_t[i] + _dot(s_rb[i], _block_diag(w[i], c), _NN)).astype(BF16) for i in rng]
    ov = [_dot(s_rb[i], _block_diag(uv[i], c), _NN) + _dot(s_rk[i], v_bd[i], _NN) for i in rng]
    bk = [jnp.concatenate([bs[i] * to_end[i], ks[i] * to_end[i]], axis=0).astype(BF16) for i in rng]
    m_st = [jnp.where(c['bd'], _mm(w[i], bk[i][:CHUNK], _TN), 0.0).astype(BF16) for i in rng]
    c_st = [jnp.where(c['bd'], _mm(jnp.concatenate([uv[i], vs[i]], axis=0), bk[i], _TN), 0.0) for i in rng]
    bonus = [_segsum(rs[i] * ks[i] * r_k, c) * vs[i] for i in rng]
    outs = []
    for i in rng:
        hb = ht.astype(BF16)
        outs.append(_dot(rw[i], hb, _NT) + ov[i])
        ht = ht * jnp.exp(last[i]) + _dot(hb, m_st[i], _NN) + c_st[i]
    mean = [_segsum(outs[i], c) * (1.0 / HEAD) for i in rng]
    d = [outs[i] - mean[i] for i in rng]
    var = [_segsum(d[i] * d[i], c) * (1.0 / HEAD) for i in rng]
    res = [(d[i] * lax.rsqrt(var[i] + RWKV_GN_EPS) * ln_w + ln_b + bonus[i]) * gs[i] for i in rng]
    return res, ht


RWKV_GROUP = 4


def _rwkv_pipeline(rs, kraws, vs, wpres, apres, gs, prm, ht, c):
    k_k, k_a, r_k, ln_w, ln_b = prm
    n = len(rs)
    st = [dict() for _ in range(n)]

    def prepare_sums(i):
        s = st[i]
        s['ld'] = -np.float32(math.exp(-0.5)) * jax.nn.sigmoid(wpres[i])
        s['kk0'] = kraws[i] * k_k
        s['ssq'] = _segsum(s['kk0'] * s['kk0'], c)
        s['cum'] = _cumsum_rows(c['ltri'], s['ld'])

    def prepare(i):
        s = st[i]
        ld, kk0, ssq, cum = s['ld'], s['kk0'], s['ssq'], s['cum']
        av = jax.nn.sigmoid(apres[i])
        kk = kk0 / jnp.maximum(jnp.sqrt(ssq), 1e-12)
        k = kraws[i] * (1.0 + (av - 1.0) * k_a)
        b = kk * av
        last = cum[CHUNK - 1:CHUNK, :]
        dec_out = jnp.exp(-cum)
        to_end = jnp.exp(last - cum)
        r_t = rs[i] * jnp.exp(cum)
        a_t = -kk * jnp.exp(cum - ld)
        ar = jnp.concatenate([a_t, r_t], axis=0).astype(BF16)
        sb = _dot(ar, _block_diag(b * dec_out, c), _NT)
        sk = _dot(ar, _block_diag(k * dec_out, c), _NT)
        s['pw'] = jnp.where(c['strict'], sb[:CHUNK], 0.0)
        s['s_rb'] = jnp.where(c['incl'], sb[CHUNK:], 0.0).astype(BF16)
        s['s_ak'] = jnp.where(c['strict'], sk[:CHUNK], 0.0).astype(BF16)
        s['s_rk'] = jnp.where(c['incl'], sk[CHUNK:], 0.0).astype(BF16)
        s['bk'] = jnp.concatenate([b * to_end, k * to_end], axis=0).astype(BF16)
        s['bonus'] = _segsum(rs[i] * k * r_k, c) * vs[i]
        s['decay'] = jnp.exp(last)
        s['r_t'], s['a_t'] = r_t, a_t

    def inverse(idx):
        pw = [st[i]['pw'] for i in idx]
        pw_bd = [_block_diag(p_, c) for p_ in pw]
        inv = [c['eye'] + p_ for p_ in pw]
        for _ in range(5):
            pw = [_mm(pw[j], pw_bd[j], _NN) for j in range(len(idx))]
            pw_bd = [_block_diag(p_, c) for p_ in pw]
            inv = [inv[j] + _mm(inv[j], pw_bd[j], _NN) for j in range(len(idx))]
        for j, i in enumerate(idx):
            st[i]['inv'] = inv[j].astype(BF16)

    def apply(i):
        s = st[i]
        v_bd = _block_diag(vs[i], c)
        w = _dot(s['inv'], _block_diag(s['a_t'], c), _NN)
        y = _dot(s['s_ak'], v_bd, _NN)
        uv = _dot(s['inv'], _block_diag(y, c), _NN)
        s['rw'] = (s['r_t'] + _dot(s['s_rb'], _block_diag(w, c), _NN)).astype(BF16)
        s['ov'] = _dot(s['s_rb'], _block_diag(uv, c), _NN) + _dot(s['s_rk'], v_bd, _NN)
        s['m_st'] = jnp.where(c['bd'], _mm(w, s['bk'][:CHUNK], _TN), 0.0).astype(BF16)
        s['c_st'] = jnp.where(c['bd'], _mm(jnp.concatenate([uv, vs[i]], axis=0), s['bk'], _TN), 0.0)

    groups = [list(range(n))[j:j + RWKV_GROUP] for j in range(0, n, RWKV_GROUP)]
    for i in groups[0]:
        prepare_sums(i)
    for i in groups[0]:
        prepare(i)
    for gi, grp in enumerate(groups):
        nxt = groups[gi + 1] if gi + 1 < len(groups) else []
        for i in nxt:
            prepare_sums(i)
        inverse(grp)
        for i in nxt:
            prepare(i)
        for i in grp:
            apply(i)
    outs = []
    for i in range(n):
        s = st[i]
        hb = ht.astype(BF16)
        o = _dot(s['rw'], hb, _NT) + s['ov']
        ht = ht * s['decay'] + _dot(hb, s['m_st'], _NN) + s['c_st']
        d = o - _segsum(o, c) * (1.0 / HEAD)
        var = _segsum(d * d, c) * (1.0 / HEAD)
        outs.append((d * lax.rsqrt(var + RWKV_GN_EPS) * ln_w + ln_b + s['bonus']) * gs[i])
    return outs, ht


def _rwkv_kernel(zr_ref, zk_ref, zv_ref, zl_ref, mur_ref, muk_ref, muv_ref, mul_ref,
                 w0_ref, a0_ref, w2_ref, a2_ref, g2_ref, kk_ref, ka_ref, rk_ref, lnw_ref, lnb_ref,
                 o_ref, ht_ref, carry_ref, carryl_ref, *, n_chunks):
    step = pl.program_id(2)

    @pl.when(step == 0)
    def _():
        ht_ref[...] = jnp.zeros_like(ht_ref)
        carry_ref[...] = jnp.zeros_like(carry_ref)
        carryl_ref[...] = jnp.zeros_like(carryl_ref)

    rows = zr_ref.shape[0]
    row = lax.broadcasted_iota(jnp.int32, (rows, 1), 0)

    def shifted(z, prev_row, mu):
        prev = jnp.where(row == 0, prev_row, pltpu.roll(z, 1, 0))
        return z + (prev - z) * mu

    zr, zk, zv, zl = zr_ref[...], zk_ref[...], zv_ref[...], zl_ref[...]
    r_all = shifted(zr, carry_ref[0, 0:1, :], mur_ref[...])
    k_all = shifted(zk, carry_ref[1, 0:1, :], muk_ref[...])
    v_all = shifted(zv, carry_ref[2, 0:1, :], muv_ref[...])
    l_all = shifted(zl, carryl_ref[0:1, :], mul_ref[...])
    carry_ref[0] = jnp.broadcast_to(zr[rows - 1:rows, :], (8, GROUP_W))
    carry_ref[1] = jnp.broadcast_to(zk[rows - 1:rows, :], (8, GROUP_W))
    carry_ref[2] = jnp.broadcast_to(zv[rows - 1:rows, :], (8, GROUP_W))
    carryl_ref[...] = jnp.broadcast_to(zl[rows - 1:rows, :], carryl_ref.shape)

    wpre_all = w0_ref[...] + _mm(jnp.tanh(l_all[:, 0:128]), w2_ref[...].astype(BF16), _NN)
    apre_all = a0_ref[...] + _mm(l_all[:, 128:256], a2_ref[...].astype(BF16), _NN)
    g_all = _mm(jax.nn.sigmoid(l_all[:, 256:512]), g2_ref[...].astype(BF16), _NN)

    c = _rwkv_consts()
    prm = (kk_ref[...], ka_ref[...], rk_ref[...], lnw_ref[...], lnb_ref[...])
    sls = [slice(ci * CHUNK, (ci + 1) * CHUNK) for ci in range(n_chunks)]
    pick = lambda x: [x[sl] for sl in sls]
    outs, ht = _rwkv_chunks(pick(r_all), pick(k_all), pick(v_all), pick(wpre_all), pick(apre_all), pick(g_all),
                            prm, ht_ref[...], c)
    for sl, out in zip(sls, outs):
        o_ref[sl, :] = out.astype(o_ref.dtype)
    ht_ref[...] = ht


def _rwkv(z3, col_r, col_k, col_v, zl3, col_l, mu_rkv, mu_l, w0, a0, w2p, a2, g2, k_k, k_a, r_k, ln_w, ln_b,
          width, rows):
    bsz, seq, _ = z3.shape
    ng = width // GROUP_W
    gw = GROUP_W

    def zspec(col):
        return pl.BlockSpec((None, rows, gw), lambda b, g, s, o=col // gw: (b, s, o + g))

    def vspec(off=0):
        return pl.BlockSpec((1, gw), lambda b, g, s, o=off: (0, o + g))

    def mspec(k):
        return pl.BlockSpec((k, gw), lambda b, g, s: (0, g))

    in_specs = [zspec(col_r), zspec(col_k), zspec(col_v),
                pl.BlockSpec((None, rows, 512), lambda b, g, s, o=col_l // 512: (b, s, o)),
                vspec(0), vspec(ng), vspec(2 * ng),
                pl.BlockSpec((1, 512), lambda b, g, s: (0, 0)),
                vspec(), vspec(), mspec(128), mspec(128), mspec(256),
                vspec(), vspec(), vspec(), vspec(), vspec()]
    kern = functools.partial(_rwkv_kernel, n_chunks=rows // CHUNK)
    return pl.pallas_call(
        kern,
        out_shape=jax.ShapeDtypeStruct((bsz, seq, width), BF16),
        grid=(bsz, ng, seq // rows),
        in_specs=in_specs,
        out_specs=pl.BlockSpec((None, rows, gw), lambda b, g, s: (b, s, g)),
        scratch_shapes=[pltpu.VMEM((gw, gw), F32), pltpu.VMEM((3, 8, gw), F32), pltpu.VMEM((8, 512), F32)],
        compiler_params=_cparams(("parallel", "parallel", "arbitrary")),
        name="rwkv7_chunked",
    )(z3, z3, z3, zl3, mu_rkv, mu_rkv, mu_rkv, mu_l, w0, a0, w2p, a2, g2, k_k, k_a, r_k, ln_w, ln_b)


def _qk_prep_kernel(q_ref, k_ref, v_ref, qg_ref, kg_ref, qo_ref, ko_ref, vo_ref):
    r = lax.broadcasted_iota(jnp.int32, (GROUP_W, GROUP_W), 0) // HEAD
    c = lax.broadcasted_iota(jnp.int32, (GROUP_W, GROUP_W), 1) // HEAD
    ones_bd = (r == c).astype(BF16)

    def norm(x, g):
        ms = _dot2_exact_rhs(x * x, ones_bd, _NN) * (1.0 / HEAD)
        return x * lax.rsqrt(ms + NORM_EPS) * g

    scale = HEAD ** -0.5 * math.log2(math.e)
    qo_ref[...] = (norm(q_ref[...], qg_ref[...]) * scale).astype(BF16)
    ko_ref[...] = norm(k_ref[...], kg_ref[...]).astype(BF16)
    vt = jnp.transpose(v_ref[...]).astype(BF16)
    vo_ref[...] = vt.reshape(vo_ref.shape)


def _qk_prep(z2, col_q, width, q_g, k_g, bsz, tr):
    t = z2.shape[0]
    gw = GROUP_W
    hd = 2 * HEAD
    nw = width // gw
    tiles_per_seq = t // bsz // tr
    qg = jnp.tile(q_g, gw // HEAD).reshape(1, gw)
    kg = jnp.tile(k_g, gw // HEAD).reshape(1, gw)

    def zspec(col):
        return pl.BlockSpec((tr, gw), lambda i, j, o=col // gw: (i, o + j))

    ospec = pl.BlockSpec((tr, gw), lambda i, j: (i, j))
    vspec = pl.BlockSpec((None, gw // hd, None, hd, tr),
                         lambda i, j: (i // tiles_per_seq, j, i % tiles_per_seq, 0, 0))
    gspec = pl.BlockSpec((1, gw), lambda i, j: (0, 0))
    shp = jax.ShapeDtypeStruct((t, width), BF16)
    vshp = jax.ShapeDtypeStruct((bsz, width // hd, tiles_per_seq, hd, tr), BF16)
    return pl.pallas_call(
        _qk_prep_kernel,
        out_shape=(shp, shp, vshp),
        grid=(t // tr, nw),
        in_specs=[zspec(col_q), zspec(col_q + width), zspec(col_q + 2 * width), gspec, gspec],
        out_specs=(ospec, ospec, vspec),
        compiler_params=_cparams(("parallel", "parallel")),
        name="qk_norm_prep",
    )(z2, z2, z2, qg, kg)


ATTN_COLS = 512
ATTN_SUM_ROWS = 16


def _diff_attn_kernel(lam_ref, q_ref, k_ref, vt_ref, g_ref, o_ref, q2_ref, m_ref, acc_ref, *, tq, out_scale):
    i = pl.program_id(2)
    hd = 2 * HEAD
    q = q_ref[...]
    lane = lax.broadcasted_iota(jnp.int32, (tq, hd), 1)
    zero = jnp.zeros_like(q)
    q2_ref[0:tq, :] = jnp.where(lane < HEAD, q, zero)
    q2_ref[tq:2 * tq, :] = jnp.where(lane >= HEAD, q, zero)
    m_ref[...] = jnp.full_like(m_ref, -jnp.inf)
    acc_ref[...] = jnp.zeros_like(acc_ref)

    cols = min(ATTN_COLS, 2 * tq)
    n_col = 2 * tq // cols
    ones_rows = jnp.ones((ATTN_SUM_ROWS, tq), BF16)

    def run(blocks):
        ks = [k_ref[pl.ds(pl.multiple_of(j * tq, tq), tq), :] for j, _ in blocks]
        vts = [jnp.concatenate([vt_ref[j], ones_rows], axis=0) for j, _ in blocks]
        items = [(b, c) for b in range(len(blocks)) for c in range(n_col)]

        def scores(item):
            b, c = item
            return _dot(ks[b], q2_ref[c * cols:(c + 1) * cols, :], _NT)

        s_next = scores(items[0])
        for n, (b, c) in enumerate(items):
            cs = slice(c * cols, (c + 1) * cols)
            s = s_next
            if n + 1 < len(items):
                s_next = scores(items[n + 1])
            if blocks[b][1]:
                kpos = lax.broadcasted_iota(jnp.int32, s.shape, 0)
                qpos = (lax.broadcasted_iota(jnp.int32, s.shape, 1) + c * cols) % tq
                s = jnp.where(kpos // CHUNK <= qpos // CHUNK, s, -jnp.inf)
            m_old = m_ref[:, cs]
            m_new = jnp.maximum(m_old, jnp.max(s, axis=0, keepdims=True))
            p = jnp.exp2((s - m_new).astype(BF16))
            alpha = jnp.exp2(m_old - m_new)
            acc_ref[:, cs] = alpha * acc_ref[:, cs] + _dot(vts[b], p, _NN)
            m_ref[:, cs] = m_new

    def body(jj, carry):
        run([(2 * jj, False), (2 * jj + 1, False)])
        return carry

    lax.fori_loop(0, i // 2, body, 0)

    @pl.when(i % 2 == 1)
    def _():
        run([(i - 1, False), (i, True)])

    @pl.when(i % 2 == 0)
    def _():
        run([(i, True)])

    acc = acc_ref[0:hd, :] / acc_ref[hd:hd + 1, :]
    o = jnp.transpose(acc[:, 0:tq] - lam_ref[0] * acc[:, tq:2 * tq])
    ms = jnp.mean(o * o, axis=-1, keepdims=True)
    o_ref[...] = (o * lax.rsqrt(ms + NORM_EPS) * g_ref[...] * out_scale).astype(o_ref.dtype)


def _diff_attn(q3, k3, vt5, lam, subln_g, out_scale):
    bsz, seq, width = q3.shape
    hd = 2 * HEAD
    nh = width // hd
    nq, tq = vt5.shape[2], vt5.shape[4]
    kern = functools.partial(_diff_attn_kernel, tq=tq, out_scale=out_scale)
    return pl.pallas_call(
        kern,
        out_shape=jax.ShapeDtypeStruct((bsz, seq, width), BF16),
        grid=(bsz, nh, nq),
        in_specs=[pl.BlockSpec(memory_space=pltpu.SMEM),
                  pl.BlockSpec((None, tq, hd), lambda b, h, i: (b, i, h)),
                  pl.BlockSpec((None, seq, hd), lambda b, h, i: (b, 0, h)),
                  pl.BlockSpec((None, None, nq, hd, tq), lambda b, h, i: (b, h, 0, 0, 0)),
                  pl.BlockSpec((1, hd), lambda b, h, i: (0, 0))],
        out_specs=pl.BlockSpec((None, tq, hd), lambda b, h, i: (b, i, h)),
        scratch_shapes=[pltpu.VMEM((2 * tq, hd), BF16), pltpu.VMEM((1, 2 * tq), F32),
                        pltpu.VMEM((hd + ATTN_SUM_ROWS, 2 * tq), F32)],
        compiler_params=_cparams(("parallel", "parallel", "arbitrary")),
        name="diff_flash_attention",
    )(lam.astype(F32).reshape(1), q3, k3, vt5, subln_g.reshape(1, hd))


def _layer(x, p_i, lambda_init, norm_mix_g, w_in, rwkv_mu, rwkv_w0, rwkv_w2, rwkv_a0, rwkv_a2, rwkv_g2,
           rwkv_k_k, rwkv_k_a, rwkv_r_k, rwkv_ln_w, rwkv_ln_b, q_norm_g, k_norm_g,
           lam_q1, lam_k1, lam_q2, lam_k2, subln_g, w_branch_a, w_branch_b, w_out,
           norm_ffn_g, w_ffn_in, ffn_conv_w, ffn_conv_b, w_ffn_out, norm_ple_g, w_ple_gate, w_ple_proj):
    bsz, seq, d = x.shape
    t = bsz * seq
    rw = w_branch_a.shape[0]
    dw = w_branch_b.shape[0]
    n_w, n_a, n_g = rwkv_w2.shape[0], rwkv_a2.shape[0], rwkv_g2.shape[0]
    lora_pad = 128 - n_w
    assert n_w <= 128 and n_a == 128 and n_g == 256 and rw % GROUP_W == 0 and dw % GROUP_W == 0
    dff = w_ffn_out.shape[0]

    c_lo = 3 * rw
    c_diff = c_lo + n_w + n_a + n_g
    col_q = c_lo
    col_ga = col_q + 3 * dw
    col_gb = col_ga + d
    n_main = col_gb + d
    w_in_t = w_in.T
    mu_rkv = rwkv_mu[:c_lo].reshape(1, c_lo)
    mu_l = jnp.concatenate([rwkv_mu[c_lo:c_lo + n_w], jnp.zeros((lora_pad,), F32),
                            rwkv_mu[c_lo + n_w:]]).reshape(1, 512)
    w2p = jnp.concatenate([rwkv_w2, jnp.zeros((lora_pad, rw), F32)], axis=0)

    tm, tn, nr = TILES['tm'], TILES['tn'], TILES['norm_rows']
    x2 = x.reshape(t, d)
    h = _rmsnorm(x2, norm_mix_g, nr)
    skip = c_diff - c_lo
    assert skip % 8 == 0
    z = _matmul_wt(h, w_in_t, lambda j: pl.multiple_of(jnp.where(j < c_lo // tn, j * tn, j * tn + skip), 8),
                   n_main, F32, tm, tn, "in_proj")
    z_l = _matmul_wt(h, w_in_t, lambda j: c_lo, 512, F32, tm, 512, "in_proj_lora", pad_at=n_w, pad_rows=lora_pad)
    row = lambda v_: v_.reshape(1, -1)
    o_a = _rwkv(z.reshape(bsz, seq, n_main), 0, rw, 2 * rw, z_l.reshape(bsz, seq, 512), 0, mu_rkv, mu_l, row(rwkv_w0), row(rwkv_a0), w2p, rwkv_a2, rwkv_g2,
                row(rwkv_k_k), row(rwkv_k_a), row(rwkv_r_k), row(rwkv_ln_w), row(rwkv_ln_b), rw,
                TILES['rwkv_rows'])
    qn, kn, vt = _qk_prep(z, col_q, dw, q_norm_g, k_norm_g, bsz, TILES['attn'])
    lam = (jnp.exp(jnp.sum(lam_q1 * lam_k1)) - jnp.exp(jnp.sum(lam_q2 * lam_k2)) + lambda_init)
    shp3 = (bsz, seq, dw)
    o_b = _diff_attn(qn.reshape(shp3), kn.reshape(shp3), vt, lam, subln_g, 1.0 - lambda_init)
    merged = _matmul([(o_a.reshape(t, rw), w_branch_a), (o_b.reshape(t, dw), w_branch_b)],
                     [(z, col_ga // tn), (z, col_gb // tn)], _ep_merge, d, BF16, tm, tn, "branch_merge")
    x2 = _matmul([(merged, w_out)], [(x2, 0)], _ep_residual, d, F32, tm, tn, "out_proj")
    h = _rmsnorm(x2, norm_ffn_g, nr)
    act = _ffn_in(h, w_ffn_in, ffn_conv_w, ffn_conv_b.reshape(1, 2 * dff), dff, seq, tm, TILES['ffn_in_tn'])
    x2 = _matmul_acc_residual(act, w_ffn_out.astype(BF16), x2, tm, TILES['ffn_out_tn'], dff // 2, "ffn_out")
    h = _rmsnorm(x2, norm_ple_g, nr)
    pe = p_i.reshape(t, -1).astype(BF16)
    x2 = _matmul([(h, w_ple_gate), (pe, w_ple_proj)], [(x2, 0)], _ep_ple, d, F32, tm, tn, "ple_gate")
    return x2.reshape(bsz, seq, d)


def kernel(x, p, norm_mix_g, w_in, rwkv_mu, rwkv_w0, rwkv_w2, rwkv_a0, rwkv_a2, rwkv_g2, rwkv_k_k, rwkv_k_a,
           rwkv_r_k, rwkv_ln_w, rwkv_ln_b, q_norm_g, k_norm_g, lam_q1, lam_k1, lam_q2, lam_k2, subln_g,
           w_branch_a, w_branch_b, w_out, norm_ffn_g, w_ffn_in, ffn_conv_w, ffn_conv_b, w_ffn_out,
           norm_ple_g, w_ple_gate, w_ple_proj):
    depth = p.shape[0]
    for i in range(depth):
        lambda_init = 0.8 - 0.6 * math.exp(-0.3 * i)
        x = _layer(x, p[i], lambda_init, norm_mix_g[i], w_in[i], rwkv_mu[i], rwkv_w0[i], rwkv_w2[i], rwkv_a0[i],
                   rwkv_a2[i], rwkv_g2[i], rwkv_k_k[i], rwkv_k_a[i], rwkv_r_k[i], rwkv_ln_w[i], rwkv_ln_b[i],
                   q_norm_g[i], k_norm_g[i], lam_q1[i], lam_k1[i], lam_q2[i], lam_k2[i], subln_g[i],
                   w_branch_a[i], w_branch_b[i], w_out[i], norm_ffn_g[i], w_ffn_in[i], ffn_conv_w[i],
                   ffn_conv_b[i], w_ffn_out[i], norm_ple_g[i], w_ple_gate[i], w_ple_proj[i])
    return x
```

```python
import functools
import math

import jax
import jax.numpy as jnp
import numpy as np
from jax import lax
from jax.experimental import pallas as pl
from jax.experimental.pallas import tpu as pltpu

F32 = jnp.float32
BF16 = jnp.bfloat16

NORM_EPS = 1e-6
RWKV_GN_EPS = 64e-5
HEAD = 64
CHUNK = 64
GROUP_W = 256
LANES = 128
VMEM_LIMIT = 56 * 1024 * 1024
TILES = dict(tm=1024, tn=512, rwkv_rows=512, attn=512, norm_rows=256, ffn_in_tn=256, ffn_out_tm=512,
             ffn_out_tn=512)


def _cparams(sem):
    return pltpu.CompilerParams(dimension_semantics=sem, vmem_limit_bytes=VMEM_LIMIT)


def _dot(a, b, dims):
    return lax.dot_general(a, b, (dims, ((), ())), preferred_element_type=F32)


_NN = ((1,), (0,))
_NT = ((1,), (1,))
_TN = ((0,), (0,))


def _split(x):
    hi = x.astype(BF16)
    lo = (x - hi.astype(F32)).astype(BF16)
    return hi, lo


def _dot2_exact_rhs(a, b_bf16, dims):
    ah, al = _split(a)
    return _dot(ah, b_bf16, dims) + _dot(al, b_bf16, dims)


def _rmsnorm_kernel(x_ref, g_ref, o_ref):
    x = x_ref[...]
    ms = jnp.mean(x * x, axis=-1, keepdims=True)
    o_ref[...] = (x * lax.rsqrt(ms + NORM_EPS) * g_ref[...]).astype(o_ref.dtype)


def _rmsnorm(x2d, g, tr=256):
    t, d = x2d.shape
    return pl.pallas_call(
        _rmsnorm_kernel,
        out_shape=jax.ShapeDtypeStruct((t, d), BF16),
        grid=(t // tr,),
        in_specs=[pl.BlockSpec((tr, d), lambda i: (i, 0)), pl.BlockSpec((1, d), lambda i: (0, 0))],
        out_specs=pl.BlockSpec((tr, d), lambda i: (i, 0)),
        compiler_params=_cparams(("parallel",)),
        name="rmsnorm",
    )(x2d, g.reshape(1, d))


def _mm_kernel(*refs, n_dots, n_extra, epilogue):
    out_ref = refs[2 * n_dots + n_extra]
    res = []
    for d in range(n_dots):
        res.append(jnp.dot(refs[2 * d][...], refs[2 * d + 1][...], preferred_element_type=F32))
    extras = [refs[2 * n_dots + e][...] for e in range(n_extra)]
    out_ref[...] = epilogue(res, extras).astype(out_ref.dtype)


def _mm_ws_kernel(*refs, n_dots, n_extra, epilogue):
    out_ref = refs[2 * n_dots + n_extra]
    w_bf = refs[2 * n_dots + n_extra + 1:]

    @pl.when(pl.program_id(1) == 0)
    def _():
        for d in range(n_dots):
            w_bf[d][...] = refs[2 * d + 1][...].astype(BF16)

    res = [jnp.dot(refs[2 * d][...], w_bf[d][...], preferred_element_type=F32) for d in range(n_dots)]
    extras = [refs[2 * n_dots + e][...] for e in range(n_extra)]
    out_ref[...] = epilogue(res, extras).astype(out_ref.dtype)


def _matmul(dots, extras, epilogue, n_out, out_dtype, tm, tn, name):
    m = dots[0][0].shape[0]
    stationary = dots[0][1].dtype == F32
    if stationary:
        ij = lambda f: (lambda j, i: f(i, j))
        grid = (n_out // tn, m // tm)
        kern_fn = _mm_ws_kernel
        scratch = [pltpu.VMEM((a.shape[1], tn), BF16) for a, _ in dots]
        sem = ("arbitrary", "arbitrary")
    else:
        ij = lambda f: f
        grid = (m // tm, n_out // tn)
        kern_fn = _mm_kernel
        scratch = []
        sem = ("parallel", "arbitrary")
    in_specs, args = [], []
    for a, w in dots:
        k = a.shape[1]
        in_specs.append(pl.BlockSpec((tm, k), ij(lambda i, j: (i, 0))))
        in_specs.append(pl.BlockSpec((k, tn), ij(lambda i, j: (0, j))))
        args += [a, w]
    for arr, off in extras:
        in_specs.append(pl.BlockSpec((tm, tn), ij(lambda i, j, off=off: (i, j + off))))
        args.append(arr)
    kern = functools.partial(kern_fn, n_dots=len(dots), n_extra=len(extras), epilogue=epilogue)
    return pl.pallas_call(
        kern,
        out_shape=jax.ShapeDtypeStruct((m, n_out), out_dtype),
        grid=grid,
        in_specs=in_specs,
        out_specs=pl.BlockSpec((tm, tn), ij(lambda i, j: (i, j))),
        scratch_shapes=scratch,
        compiler_params=_cparams(sem),
        name=name,
    )(*args)


def _mm_wt_kernel(a_ref, wt_ref, o_ref, w_bf, *, pad_at, pad_rows):
    @pl.when(pl.program_id(1) == 0)
    def _():
        wt = wt_ref[...]
        if pad_rows:
            tn = wt.shape[0]
            wt = jnp.concatenate([wt[:pad_at], jnp.zeros((pad_rows, wt.shape[1]), F32),
                                  wt[pad_at:tn - pad_rows]], axis=0)
        w_bf[...] = jnp.transpose(wt).astype(BF16)

    o_ref[...] = jnp.dot(a_ref[...], w_bf[...], preferred_element_type=F32).astype(o_ref.dtype)


def _matmul_wt(a, wt, row_start, n_out, out_dtype, tm, tn, name, pad_at=0, pad_rows=0):
    m, k = a.shape
    return pl.pallas_call(
        functools.partial(_mm_wt_kernel, pad_at=pad_at, pad_rows=pad_rows),
        out_shape=jax.ShapeDtypeStruct((m, n_out), out_dtype),
        grid=(n_out // tn, m // tm),
        in_specs=[pl.BlockSpec((tm, k), lambda j, i: (i, 0)),
                  pl.BlockSpec((pl.Element(tn), pl.Element(k)), lambda j, i: (row_start(j), 0))],
        out_specs=pl.BlockSpec((tm, tn), lambda j, i: (i, j)),
        scratch_shapes=[pltpu.VMEM((k, tn), BF16)],
        compiler_params=_cparams(("arbitrary", "arbitrary")),
        name=name,
    )(a, wt)


def _ep_identity(res, extras):
    return res[0]


def _ep_merge(res, extras):
    return jax.nn.sigmoid(extras[0]) * res[0] + jax.nn.sigmoid(extras[1]) * res[1]


def _ep_residual(res, extras):
    return extras[0] + res[0]


def _ep_ple(res, extras):
    return extras[0] + jax.nn.sigmoid(res[0]) * res[1]


def _mm_acc_kernel(a_ref, w_ref, r_ref, o_ref, acc_ref):
    kk = pl.program_id(2)

    @pl.when(kk == 0)
    def _():
        acc_ref[...] = jnp.zeros_like(acc_ref)

    acc_ref[...] += jnp.dot(a_ref[...], w_ref[...], preferred_element_type=F32)

    @pl.when(kk == pl.num_programs(2) - 1)
    def _():
        o_ref[...] = r_ref[...] + acc_ref[...]


def _matmul_acc_residual(a, w, resid, tm, tn, tk, name):
    m, k = a.shape
    n = w.shape[1]
    return pl.pallas_call(
        _mm_acc_kernel,
        out_shape=jax.ShapeDtypeStruct((m, n), F32),
        grid=(m // tm, n // tn, k // tk),
        in_specs=[pl.BlockSpec((tm, tk), lambda i, j, kk: (i, kk)),
                  pl.BlockSpec((tk, tn), lambda i, j, kk: (kk, j)),
                  pl.BlockSpec((tm, tn), lambda i, j, kk: (i, j))],
        out_specs=pl.BlockSpec((tm, tn), lambda i, j, kk: (i, j)),
        scratch_shapes=[pltpu.VMEM((tm, tn), F32)],
        compiler_params=_cparams(("parallel", "arbitrary", "arbitrary")),
        name=name,
    )(a, w, resid)


def _ffn_in_kernel(a_ref, wg_ref, wu_ref, cwg_ref, cwu_ref, cbg_ref, cbu_ref, o_ref, carry_ref, w_ref,
                   *, tiles_per_seq):
    i = pl.program_id(1)
    tn = wg_ref.shape[1]

    @pl.when(i == 0)
    def _():
        w_ref[:, 0:tn] = wg_ref[...].astype(BF16)
        w_ref[:, tn:2 * tn] = wu_ref[...].astype(BF16)

    @pl.when(i % tiles_per_seq == 0)
    def _():
        carry_ref[...] = jnp.zeros_like(carry_ref)

    a = a_ref[...]
    tm = a.shape[0]
    row = lax.broadcasted_iota(jnp.int32, (tm, 1), 0)

    def conv(u, carry, cw, cb):
        c1 = carry[7:8, :]
        c2 = carry[6:7, :]
        p1 = jnp.where(row == 0, c1, pltpu.roll(u, 1, 0))
        p2 = jnp.where(row == 0, c2, jnp.where(row == 1, c1, pltpu.roll(u, 2, 0)))
        return cb + p2 * cw[0:1, :] + p1 * cw[1:2, :] + u * cw[2:3, :]

    u = jnp.dot(a, w_ref[...], preferred_element_type=F32)
    ug = u[:, 0:tn]
    uu = u[:, tn:2 * tn]
    gate = conv(ug, carry_ref[0], cwg_ref[...], cbg_ref[...])
    up = conv(uu, carry_ref[1], cwu_ref[...], cbu_ref[...])
    carry_ref[0] = ug[tm - 8:tm, :]
    carry_ref[1] = uu[tm - 8:tm, :]
    o_ref[...] = (gate * jax.nn.sigmoid(gate) * up).astype(o_ref.dtype)


def _ffn_in(h, w, conv_w, conv_b, dff, seq, tm, tn):
    t, k = h.shape
    nj = dff // tn
    kern = functools.partial(_ffn_in_kernel, tiles_per_seq=seq // tm)
    return pl.pallas_call(
        kern,
        out_shape=jax.ShapeDtypeStruct((t, dff), BF16),
        grid=(nj, t // tm),
        in_specs=[pl.BlockSpec((tm, k), lambda j, i: (i, 0)),
                  pl.BlockSpec((k, tn), lambda j, i: (0, j)),
                  pl.BlockSpec((k, tn), lambda j, i: (0, j + nj)),
                  pl.BlockSpec((3, tn), lambda j, i: (0, j)),
                  pl.BlockSpec((3, tn), lambda j, i: (0, j + nj)),
                  pl.BlockSpec((1, tn), lambda j, i: (0, j)),
                  pl.BlockSpec((1, tn), lambda j, i: (0, j + nj))],
        out_specs=pl.BlockSpec((tm, tn), lambda j, i: (i, j)),
        scratch_shapes=[pltpu.VMEM((2, 8, tn), F32), pltpu.VMEM((k, 2 * tn), BF16)],
        compiler_params=_cparams(("arbitrary", "arbitrary")),
        name="ffn_in_conv_gate",
    )(h, w, w, conv_w, conv_w, conv_b, conv_b)


def _rwkv_consts():
    r = lax.broadcasted_iota(jnp.int32, (GROUP_W, GROUP_W), 0) // HEAD
    c = lax.broadcasted_iota(jnp.int32, (GROUP_W, GROUP_W), 1) // HEAD
    bd = r == c
    t = lax.broadcasted_iota(jnp.int32, (CHUNK, GROUP_W), 0)
    s = lax.broadcasted_iota(jnp.int32, (CHUNK, GROUP_W), 1) % HEAD
    tt = lax.broadcasted_iota(jnp.int32, (CHUNK, CHUNK), 0)
    ss = lax.broadcasted_iota(jnp.int32, (CHUNK, CHUNK), 1)
    lane_head = lax.broadcasted_iota(jnp.int32, (CHUNK, LANES), 1) // HEAD
    head_lanes = [lane_head == hh for hh in range(LANES // HEAD)]
    return dict(bd=bd, ones_bd=bd.astype(BF16), strict=s < t, incl=s <= t, head_lanes=head_lanes,
                eye=(s == t).astype(F32), ltri=(ss <= tt).astype(BF16))


def _block_diag(y, c):
    yb = y.astype(BF16)
    zero = jnp.zeros((CHUNK, LANES), BF16)
    heads_per_tile = LANES // HEAD
    rows = []
    for h in range(GROUP_W // HEAD):
        tile = h // heads_per_tile
        piece = jnp.where(c['head_lanes'][h % heads_per_tile], yb[:, tile * LANES:(tile + 1) * LANES], zero)
        rows.append(jnp.concatenate([piece if t == tile else zero for t in range(GROUP_W // LANES)], axis=1))
    return jnp.concatenate(rows, axis=0)


def _mm(a, b_bf16, dims):
    return _dot(a.astype(BF16), b_bf16, dims)


def _segsum(x, c):
    return _mm(x, c['ones_bd'], _NN)


def _cumsum_rows(l_bf16, x):
    xh, xl = _split(x)
    return _dot(l_bf16, xh, _NN) + _dot(l_bf16, xl, _NN)


def _rwkv_chunks(rs, kraws, vs, wpres, apres, gs, prm, ht, c):
    k_k, k_a, r_k, ln_w, ln_b = prm
    n = len(rs)
    rng = range(n)
    lds = [-np.float32(math.exp(-0.5)) * jax.nn.sigmoid(wpres[i]) for i in rng]
    avs = [jax.nn.sigmoid(apres[i]) for i in rng]
    kk0 = [kraws[i] * k_k for i in rng]
    ssq = [_segsum(kk0[i] * kk0[i], c) for i in rng]
    cums = [_cumsum_rows(c['ltri'], lds[i]) for i in rng]
    kks = [kk0[i] / jnp.maximum(jnp.sqrt(ssq[i]), 1e-12) for i in rng]
    ks = [kraws[i] * (1.0 + (avs[i] - 1.0) * k_a) for i in rng]
    bs = [kks[i] * avs[i] for i in rng]
    last = [cums[i][CHUNK - 1:CHUNK, :] for i in rng]
    dec_out = [jnp.exp(-cums[i]) for i in rng]
    to_end = [jnp.exp(last[i] - cums[i]) for i in rng]
    r_t = [rs[i] * jnp.exp(cums[i]) for i in rng]
    a_t = [-kks[i] * jnp.exp(cums[i] - lds[i]) for i in rng]
    ar = [jnp.concatenate([a_t[i], r_t[i]], axis=0).astype(BF16) for i in rng]
    sb = [_dot(ar[i], _block_diag(bs[i] * dec_out[i], c), _NT) for i in rng]
    sk = [_dot(ar[i], _block_diag(ks[i] * dec_out[i], c), _NT) for i in rng]
    s_ab = [jnp.where(c['strict'], sb[i][:CHUNK], 0.0) for i in rng]
    s_rb = [jnp.where(c['incl'], sb[i][CHUNK:], 0.0).astype(BF16) for i in rng]
    s_ak = [jnp.where(c['strict'], sk[i][:CHUNK], 0.0).astype(BF16) for i in rng]
    s_rk = [jnp.where(c['incl'], sk[i][CHUNK:], 0.0).astype(BF16) for i in rng]
    pw = s_ab
    pw_bd = [_block_diag(pw[i], c) for i in rng]
    inv = [c['eye'] + pw[i] for i in rng]
    for _ in range(5):
        pw = [_mm(pw[i], pw_bd[i], _NN) for i in rng]
        pw_bd = [_block_diag(pw[i], c) for i in rng]
        inv = [inv[i] + _mm(inv[i], pw_bd[i], _NN) for i in rng]
    inv = [inv[i].astype(BF16) for i in rng]
    v_bd = [_block_diag(vs[i], c) for i in rng]
    w = [_dot(inv[i], _block_diag(a_t[i], c), _NN) for i in rng]
    y = [_dot(s_ak[i], v_bd[i], _NN) for i in rng]
    uv = [_dot(inv[i], _block_diag(y[i], c), _NN) for i in rng]
    rw = [(r_t[i] + _dot(s_rb[i], _block_diag(w[i], c), _NN)).astype(BF16) for i in rng]
    ov = [_dot(s_rb[i], _block_diag(uv[i], c), _NN) + _dot(s_rk[i], v_bd[i], _NN) for i in rng]
    bk = [jnp.concatenate([bs[i] * to_end[i], ks[i] * to_end[i]], axis=0).astype(BF16) for i in rng]
    m_st = [jnp.where(c['bd'], _mm(w[i], bk[i][:CHUNK], _TN), 0.0).astype(BF16) for i in rng]
    c_st = [jnp.where(c['bd'], _mm(jnp.concatenate([uv[i], vs[i]], axis=0), bk[i], _TN), 0.0) for i in rng]
    bonus = [_segsum(rs[i] * ks[i] * r_k, c) * vs[i] for i in rng]
    outs = []
    for i in rng:
        hb = ht.astype(BF16)
        outs.append(_dot(rw[i], hb, _NT) + ov[i])
        ht = ht * jnp.exp(last[i]) + _dot(hb, m_st[i], _NN) + c_st[i]
    mean = [_segsum(outs[i], c) * (1.0 / HEAD) for i in rng]
    d = [outs[i] - mean[i] for i in rng]
    var = [_segsum(d[i] * d[i], c) * (1.0 / HEAD) for i in rng]
    res = [(d[i] * lax.rsqrt(var[i] + RWKV_GN_EPS) * ln_w + ln_b + bonus[i]) * gs[i] for i in rng]
    return res, ht


def _rwkv_kernel(zr_ref, zk_ref, zv_ref, zl_ref, mur_ref, muk_ref, muv_ref, mul_ref,
                 w0_ref, a0_ref, w2_ref, a2_ref, g2_ref, kk_ref, ka_ref, rk_ref, lnw_ref, lnb_ref,
                 o_ref, ht_ref, carry_ref, carryl_ref, *, n_chunks):
    step = pl.program_id(2)

    @pl.when(step == 0)
    def _():
        ht_ref[...] = jnp.zeros_like(ht_ref)
        carry_ref[...] = jnp.zeros_like(carry_ref)
        carryl_ref[...] = jnp.zeros_like(carryl_ref)

    rows = zr_ref.shape[0]
    row = lax.broadcasted_iota(jnp.int32, (rows, 1), 0)

    def shifted(z, prev_row, mu):
        prev = jnp.where(row == 0, prev_row, pltpu.roll(z, 1, 0))
        return z + (prev - z) * mu

    zr, zk, zv, zl = zr_ref[...], zk_ref[...], zv_ref[...], zl_ref[...]
    r_all = shifted(zr, carry_ref[0, 0:1, :], mur_ref[...])
    k_all = shifted(zk, carry_ref[1, 0:1, :], muk_ref[...])
    v_all = shifted(zv, carry_ref[2, 0:1, :], muv_ref[...])
    l_all = shifted(zl, carryl_ref[0:1, :], mul_ref[...])
    carry_ref[0] = jnp.broadcast_to(zr[rows - 1:rows, :], (8, GROUP_W))
    carry_ref[1] = jnp.broadcast_to(zk[rows - 1:rows, :], (8, GROUP_W))
    carry_ref[2] = jnp.broadcast_to(zv[rows - 1:rows, :], (8, GROUP_W))
    carryl_ref[...] = jnp.broadcast_to(zl[rows - 1:rows, :], carryl_ref.shape)

    wpre_all = w0_ref[...] + _mm(jnp.tanh(l_all[:, 0:128]), w2_ref[...].astype(BF16), _NN)
    apre_all = a0_ref[...] + _mm(l_all[:, 128:256], a2_ref[...].astype(BF16), _NN)
    g_all = _mm(jax.nn.sigmoid(l_all[:, 256:512]), g2_ref[...].astype(BF16), _NN)

    c = _rwkv_consts()
    prm = (kk_ref[...], ka_ref[...], rk_ref[...], lnw_ref[...], lnb_ref[...])
    sls = [slice(ci * CHUNK, (ci + 1) * CHUNK) for ci in range(n_chunks)]
    pick = lambda x: [x[sl] for sl in sls]
    outs, ht = _rwkv_chunks(pick(r_all), pick(k_all), pick(v_all), pick(wpre_all), pick(apre_all), pick(g_all),
                            prm, ht_ref[...], c)
    for sl, out in zip(sls, outs):
        o_ref[sl, :] = out.astype(o_ref.dtype)
    ht_ref[...] = ht


def _rwkv(z3, col_r, col_k, col_v, zl3, col_l, mu_rkv, mu_l, w0, a0, w2p, a2, g2, k_k, k_a, r_k, ln_w, ln_b,
          width, rows):
    bsz, seq, _ = z3.shape
    ng = width // GROUP_W
    gw = GROUP_W

    def zspec(col):
        return pl.BlockSpec((None, rows, gw), lambda b, g, s, o=col // gw: (b, s, o + g))

    def vspec(off=0):
        return pl.BlockSpec((1, gw), lambda b, g, s, o=off: (0, o + g))

    def mspec(k):
        return pl.BlockSpec((k, gw), lambda b, g, s: (0, g))

    in_specs = [zspec(col_r), zspec(col_k), zspec(col_v),
                pl.BlockSpec((None, rows, 512), lambda b, g, s, o=col_l // 512: (b, s, o)),
                vspec(0), vspec(ng), vspec(2 * ng),
                pl.BlockSpec((1, 512), lambda b, g, s: (0, 0)),
                vspec(), vspec(), mspec(128), mspec(128), mspec(256),
                vspec(), vspec(), vspec(), vspec(), vspec()]
    kern = functools.partial(_rwkv_kernel, n_chunks=rows // CHUNK)
    return pl.pallas_call(
        kern,
        out_shape=jax.ShapeDtypeStruct((bsz, seq, width), BF16),
        grid=(bsz, ng, seq // rows),
        in_specs=in_specs,
        out_specs=pl.BlockSpec((None, rows, gw), lambda b, g, s: (b, s, g)),
        scratch_shapes=[pltpu.VMEM((gw, gw), F32), pltpu.VMEM((3, 8, gw), F32), pltpu.VMEM((8, 512), F32)],
        compiler_params=_cparams(("parallel", "parallel", "arbitrary")),
        name="rwkv7_chunked",
    )(z3, z3, z3, zl3, mu_rkv, mu_rkv, mu_rkv, mu_l, w0, a0, w2p, a2, g2, k_k, k_a, r_k, ln_w, ln_b)


def _qk_prep_kernel(q_ref, k_ref, v_ref, qg_ref, kg_ref, qo_ref, ko_ref, vo_ref):
    r = lax.broadcasted_iota(jnp.int32, (GROUP_W, GROUP_W), 0) // HEAD
    c = lax.broadcasted_iota(jnp.int32, (GROUP_W, GROUP_W), 1) // HEAD
    ones_bd = (r == c).astype(BF16)

    def norm(x, g):
        ms = _dot2_exact_rhs(x * x, ones_bd, _NN) * (1.0 / HEAD)
        return x * lax.rsqrt(ms + NORM_EPS) * g

    scale = HEAD ** -0.5 * math.log2(math.e)
    qo_ref[...] = (norm(q_ref[...], qg_ref[...]) * scale).astype(BF16)
    ko_ref[...] = norm(k_ref[...], kg_ref[...]).astype(BF16)
    vt = jnp.transpose(v_ref[...]).astype(BF16)
    vo_ref[...] = vt.reshape(vo_ref.shape)


def _qk_prep(z2, col_q, width, q_g, k_g, bsz, tr):
    t = z2.shape[0]
    gw = GROUP_W
    hd = 2 * HEAD
    nw = width // gw
    tiles_per_seq = t // bsz // tr
    qg = jnp.tile(q_g, gw // HEAD).reshape(1, gw)
    kg = jnp.tile(k_g, gw // HEAD).reshape(1, gw)

    def zspec(col):
        return pl.BlockSpec((tr, gw), lambda i, j, o=col // gw: (i, o + j))

    ospec = pl.BlockSpec((tr, gw), lambda i, j: (i, j))
    vspec = pl.BlockSpec((None, gw // hd, None, hd, tr),
                         lambda i, j: (i // tiles_per_seq, j, i % tiles_per_seq, 0, 0))
    gspec = pl.BlockSpec((1, gw), lambda i, j: (0, 0))
    shp = jax.ShapeDtypeStruct((t, width), BF16)
    vshp = jax.ShapeDtypeStruct((bsz, width // hd, tiles_per_seq, hd, tr), BF16)
    return pl.pallas_call(
        _qk_prep_kernel,
        out_shape=(shp, shp, vshp),
        grid=(t // tr, nw),
        in_specs=[zspec(col_q), zspec(col_q + width), zspec(col_q + 2 * width), gspec, gspec],
        out_specs=(ospec, ospec, vspec),
        compiler_params=_cparams(("parallel", "parallel")),
        name="qk_norm_prep",
    )(z2, z2, z2, qg, kg)


ATTN_COLS = 512
ATTN_AHEAD = 2
ATTN_SUM_ROWS = 16


def _diff_attn_kernel(lam_ref, q_ref, k_ref, vt_ref, g_ref, o_ref, q2_ref, m_ref, acc_ref, *, tq, out_scale):
    i = pl.program_id(2)
    hd = 2 * HEAD
    q = q_ref[...]
    lane = lax.broadcasted_iota(jnp.int32, (tq, hd), 1)
    zero = jnp.zeros_like(q)
    q2_ref[0:tq, :] = jnp.where(lane < HEAD, q, zero)
    q2_ref[tq:2 * tq, :] = jnp.where(lane >= HEAD, q, zero)
    m_ref[...] = jnp.full_like(m_ref, -jnp.inf)
    acc_ref[...] = jnp.zeros_like(acc_ref)

    cols = min(ATTN_COLS, 2 * tq)
    n_col = 2 * tq // cols
    ones_rows = jnp.ones((ATTN_SUM_ROWS, tq), BF16)

    def run(blocks):
        ks = [k_ref[pl.ds(pl.multiple_of(j * tq, tq), tq), :] for j, _ in blocks]
        vts = [jnp.concatenate([vt_ref[j], ones_rows], axis=0) for j, _ in blocks]
        items = [(b, c) for b in range(len(blocks)) for c in range(n_col)]

        def scores(item):
            b, c = item
            return _dot(ks[b], q2_ref[c * cols:(c + 1) * cols, :], _NT)

        ahead = [scores(it) for it in items[:ATTN_AHEAD]]
        for n, (b, c) in enumerate(items):
            cs = slice(c * cols, (c + 1) * cols)
            s = ahead.pop(0)
            if n + ATTN_AHEAD < len(items):
                ahead.append(scores(items[n + ATTN_AHEAD]))
            if blocks[b][1]:
                kpos = lax.broadcasted_iota(jnp.int32, s.shape, 0)
                qpos = (lax.broadcasted_iota(jnp.int32, s.shape, 1) + c * cols) % tq
                s = jnp.where(kpos // CHUNK <= qpos // CHUNK, s, -jnp.inf)
            m_old = m_ref[:, cs]
            m_new = jnp.maximum(m_old, jnp.max(s, axis=0, keepdims=True))
            p = jnp.exp2((s - m_new).astype(BF16))
            alpha = jnp.exp2(m_old - m_new)
            acc_ref[:, cs] = alpha * acc_ref[:, cs] + _dot(vts[b], p, _NN)
            m_ref[:, cs] = m_new

    def body(jj, carry):
        run([(2 * jj, False), (2 * jj + 1, False)])
        return carry

    lax.fori_loop(0, i // 2, body, 0)

    @pl.when(i % 2 == 1)
    def _():
        run([(i - 1, False), (i, True)])

    @pl.when(i % 2 == 0)
    def _():
        run([(i, True)])

    acc = acc_ref[0:hd, :] / acc_ref[hd:hd + 1, :]
    o = jnp.transpose(acc[:, 0:tq] - lam_ref[0] * acc[:, tq:2 * tq])
    ms = jnp.mean(o * o, axis=-1, keepdims=True)
    o_ref[...] = (o * lax.rsqrt(ms + NORM_EPS) * g_ref[...] * out_scale).astype(o_ref.dtype)


def _diff_attn(q3, k3, vt5, lam, subln_g, out_scale):
    bsz, seq, width = q3.shape
    hd = 2 * HEAD
    nh = width // hd
    nq, tq = vt5.shape[2], vt5.shape[4]
    kern = functools.partial(_diff_attn_kernel, tq=tq, out_scale=out_scale)
    return pl.pallas_call(
        kern,
        out_shape=jax.ShapeDtypeStruct((bsz, seq, width), BF16),
        grid=(bsz, nh, nq),
        in_specs=[pl.BlockSpec(memory_space=pltpu.SMEM),
                  pl.BlockSpec((None, tq, hd), lambda b, h, i: (b, i, h)),
                  pl.BlockSpec((None, seq, hd), lambda b, h, i: (b, 0, h)),
                  pl.BlockSpec((None, None, nq, hd, tq), lambda b, h, i: (b, h, 0, 0, 0)),
                  pl.BlockSpec((1, hd), lambda b, h, i: (0, 0))],
        out_specs=pl.BlockSpec((None, tq, hd), lambda b, h, i: (b, i, h)),
        scratch_shapes=[pltpu.VMEM((2 * tq, hd), BF16), pltpu.VMEM((1, 2 * tq), F32),
                        pltpu.VMEM((hd + ATTN_SUM_ROWS, 2 * tq), F32)],
        compiler_params=_cparams(("parallel", "parallel", "arbitrary")),
        name="diff_flash_attention",
    )(lam.astype(F32).reshape(1), q3, k3, vt5, subln_g.reshape(1, hd))


def _layer(x, p_i, lambda_init, norm_mix_g, w_in, rwkv_mu, rwkv_w0, rwkv_w2, rwkv_a0, rwkv_a2, rwkv_g2,
           rwkv_k_k, rwkv_k_a, rwkv_r_k, rwkv_ln_w, rwkv_ln_b, q_norm_g, k_norm_g,
           lam_q1, lam_k1, lam_q2, lam_k2, subln_g, w_branch_a, w_branch_b, w_out,
           norm_ffn_g, w_ffn_in, ffn_conv_w, ffn_conv_b, w_ffn_out, norm_ple_g, w_ple_gate, w_ple_proj):
    bsz, seq, d = x.shape
    t = bsz * seq
    rw = w_branch_a.shape[0]
    dw = w_branch_b.shape[0]
    n_w, n_a, n_g = rwkv_w2.shape[0], rwkv_a2.shape[0], rwkv_g2.shape[0]
    lora_pad = 128 - n_w
    assert n_w <= 128 and n_a == 128 and n_g == 256 and rw % GROUP_W == 0 and dw % GROUP_W == 0
    dff = w_ffn_out.shape[0]

    c_lo = 3 * rw
    c_diff = c_lo + n_w + n_a + n_g
    col_q = c_lo
    col_ga = col_q + 3 * dw
    col_gb = col_ga + d
    n_main = col_gb + d
    w_in_t = w_in.T
    mu_rkv = rwkv_mu[:c_lo].reshape(1, c_lo)
    mu_l = jnp.concatenate([rwkv_mu[c_lo:c_lo + n_w], jnp.zeros((lora_pad,), F32),
                            rwkv_mu[c_lo + n_w:]]).reshape(1, 512)
    w2p = jnp.concatenate([rwkv_w2, jnp.zeros((lora_pad, rw), F32)], axis=0)

    tm, tn, nr = TILES['tm'], TILES['tn'], TILES['norm_rows']
    x2 = x.reshape(t, d)
    h = _rmsnorm(x2, norm_mix_g, nr)
    skip = c_diff - c_lo
    assert skip % 8 == 0
    z = _matmul_wt(h, w_in_t, lambda j: pl.multiple_of(jnp.where(j < c_lo // tn, j * tn, j * tn + skip), 8),
                   n_main, F32, tm, tn, "in_proj")
    z_l = _matmul_wt(h, w_in_t, lambda j: c_lo, 512, F32, tm, 512, "in_proj_lora", pad_at=n_w, pad_rows=lora_pad)
    row = lambda v_: v_.reshape(1, -1)
    o_a = _rwkv(z.reshape(bsz, seq, n_main), 0, rw, 2 * rw, z_l.reshape(bsz, seq, 512), 0, mu_rkv, mu_l, row(rwkv_w0), row(rwkv_a0), w2p, rwkv_a2, rwkv_g2,
                row(rwkv_k_k), row(rwkv_k_a), row(rwkv_r_k), row(rwkv_ln_w), row(rwkv_ln_b), rw,
                TILES['rwkv_rows'])
    qn, kn, vt = _qk_prep(z, col_q, dw, q_norm_g, k_norm_g, bsz, TILES['attn'])
    lam = (jnp.exp(jnp.sum(lam_q1 * lam_k1)) - jnp.exp(jnp.sum(lam_q2 * lam_k2)) + lambda_init)
    shp3 = (bsz, seq, dw)
    o_b = _diff_attn(qn.reshape(shp3), kn.reshape(shp3), vt, lam, subln_g, 1.0 - lambda_init)
    merged = _matmul([(o_a.reshape(t, rw), w_branch_a), (o_b.reshape(t, dw), w_branch_b)],
                     [(z, col_ga // tn), (z, col_gb // tn)], _ep_merge, d, BF16, tm, tn, "branch_merge")
    x2 = _matmul([(merged, w_out)], [(x2, 0)], _ep_residual, d, F32, tm, tn, "out_proj")
    h = _rmsnorm(x2, norm_ffn_g, nr)
    act = _ffn_in(h, w_ffn_in, ffn_conv_w, ffn_conv_b.reshape(1, 2 * dff), dff, seq, tm, TILES['ffn_in_tn'])
    x2 = _matmul([(act, w_ffn_out.astype(BF16))], [(x2, 0)], _ep_residual, d, F32, TILES['ffn_out_tm'],
                 TILES['ffn_out_tn'], "ffn_out")
    h = _rmsnorm(x2, norm_ple_g, nr)
    pe = p_i.reshape(t, -1).astype(BF16)
    x2 = _matmul([(h, w_ple_gate), (pe, w_ple_proj)], [(x2, 0)], _ep_ple, d, F32, tm, tn, "ple_gate")
    return x2.reshape(bsz, seq, d)


def kernel(x, p, norm_mix_g, w_in, rwkv_mu, rwkv_w0, rwkv_w2, rwkv_a0, rwkv_a2, rwkv_g2, rwkv_k_k, rwkv_k_a,
           rwkv_r_k, rwkv_ln_w, rwkv_ln_b, q_norm_g, k_norm_g, lam_q1, lam_k1, lam_q2, lam_k2, subln_g,
           w_branch_a, w_branch_b, w_out, norm_ffn_g, w_ffn_in, ffn_conv_w, ffn_conv_b, w_ffn_out,
           norm_ple_g, w_ple_gate, w_ple_proj):
    depth = p.shape[0]
    for i in range(depth):
        lambda_init = 0.8 - 0.6 * math.exp(-0.3 * i)
        x = _layer(x, p[i], lambda_init, norm_mix_g[i], w_in[i], rwkv_mu[i], rwkv_w0[i], rwkv_w2[i], rwkv_a0[i],
                   rwkv_a2[i], rwkv_g2[i], rwkv_k_k[i], rwkv_k_a[i], rwkv_r_k[i], rwkv_ln_w[i], rwkv_ln_b[i],
                   q_norm_g[i], k_norm_g[i], lam_q1[i], lam_k1[i], lam_q2[i], lam_k2[i], subln_g[i],
                   w_branch_a[i], w_branch_b[i], w_out[i], norm_ffn_g[i], w_ffn_in[i], ffn_conv_w[i],
                   ffn_conv_b[i], w_ffn_out[i], norm_ple_g[i], w_ple_gate[i], w_ple_proj[i])
    return x
```

```python
import functools
import math

import jax
import jax.numpy as jnp
import numpy as np
from jax import lax
from jax.experimental import pallas as pl
from jax.experimental.pallas import tpu as pltpu

F32 = jnp.float32
BF16 = jnp.bfloat16

NORM_EPS = 1e-6
RWKV_GN_EPS = 64e-5
HEAD = 64
CHUNK = 64
GROUP_W = 256
LANES = 128
VMEM_LIMIT = 56 * 1024 * 1024
TILES = dict(tm=1024, tn=512, rwkv_rows=512, attn=512, norm_rows=256, ffn_in_tn=256, ffn_out_tm=512,
             ffn_out_tn=512)


def _cparams(sem):
    return pltpu.CompilerParams(dimension_semantics=sem, vmem_limit_bytes=VMEM_LIMIT)


def _dot(a, b, dims):
    return lax.dot_general(a, b, (dims, ((), ())), preferred_element_type=F32)


_NN = ((1,), (0,))
_NT = ((1,), (1,))
_TN = ((0,), (0,))


def _split(x):
    hi = x.astype(BF16)
    lo = (x - hi.astype(F32)).astype(BF16)
    return hi, lo


def _dot2_exact_rhs(a, b_bf16, dims):
    ah, al = _split(a)
    return _dot(ah, b_bf16, dims) + _dot(al, b_bf16, dims)


def _rmsnorm_kernel(x_ref, g_ref, o_ref):
    x = x_ref[...]
    ms = jnp.mean(x * x, axis=-1, keepdims=True)
    o_ref[...] = (x * lax.rsqrt(ms + NORM_EPS) * g_ref[...]).astype(o_ref.dtype)


def _rmsnorm(x2d, g, tr=256):
    t, d = x2d.shape
    return pl.pallas_call(
        _rmsnorm_kernel,
        out_shape=jax.ShapeDtypeStruct((t, d), BF16),
        grid=(t // tr,),
        in_specs=[pl.BlockSpec((tr, d), lambda i: (i, 0)), pl.BlockSpec((1, d), lambda i: (0, 0))],
        out_specs=pl.BlockSpec((tr, d), lambda i: (i, 0)),
        compiler_params=_cparams(("parallel",)),
        name="rmsnorm",
    )(x2d, g.reshape(1, d))


def _mm_kernel(*refs, n_dots, n_extra, epilogue):
    out_ref = refs[2 * n_dots + n_extra]
    res = []
    for d in range(n_dots):
        res.append(jnp.dot(refs[2 * d][...], refs[2 * d + 1][...], preferred_element_type=F32))
    extras = [refs[2 * n_dots + e][...] for e in range(n_extra)]
    out_ref[...] = epilogue(res, extras).astype(out_ref.dtype)


def _mm_ws_kernel(*refs, n_dots, n_extra, epilogue):
    out_ref = refs[2 * n_dots + n_extra]
    w_bf = refs[2 * n_dots + n_extra + 1:]

    @pl.when(pl.program_id(1) == 0)
    def _():
        for d in range(n_dots):
            w_bf[d][...] = refs[2 * d + 1][...].astype(BF16)

    res = [jnp.dot(refs[2 * d][...], w_bf[d][...], preferred_element_type=F32) for d in range(n_dots)]
    extras = [refs[2 * n_dots + e][...] for e in range(n_extra)]
    out_ref[...] = epilogue(res, extras).astype(out_ref.dtype)


def _matmul(dots, extras, epilogue, n_out, out_dtype, tm, tn, name):
    m = dots[0][0].shape[0]
    stationary = dots[0][1].dtype == F32
    if stationary:
        ij = lambda f: (lambda j, i: f(i, j))
        grid = (n_out // tn, m // tm)
        kern_fn = _mm_ws_kernel
        scratch = [pltpu.VMEM((a.shape[1], tn), BF16) for a, _ in dots]
        sem = ("arbitrary", "arbitrary")
    else:
        ij = lambda f: f
        grid = (m // tm, n_out // tn)
        kern_fn = _mm_kernel
        scratch = []
        sem = ("parallel", "arbitrary")
    in_specs, args = [], []
    for a, w in dots:
        k = a.shape[1]
        in_specs.append(pl.BlockSpec((tm, k), ij(lambda i, j: (i, 0))))
        in_specs.append(pl.BlockSpec((k, tn), ij(lambda i, j: (0, j))))
        args += [a, w]
    for arr, off in extras:
        in_specs.append(pl.BlockSpec((tm, tn), ij(lambda i, j, off=off: (i, j + off))))
        args.append(arr)
    kern = functools.partial(kern_fn, n_dots=len(dots), n_extra=len(extras), epilogue=epilogue)
    return pl.pallas_call(
        kern,
        out_shape=jax.ShapeDtypeStruct((m, n_out), out_dtype),
        grid=grid,
        in_specs=in_specs,
        out_specs=pl.BlockSpec((tm, tn), ij(lambda i, j: (i, j))),
        scratch_shapes=scratch,
        compiler_params=_cparams(sem),
        name=name,
    )(*args)


def _mm_wt_kernel(a_ref, wt_ref, o_ref, w_bf, *, pad_at, pad_rows):
    @pl.when(pl.program_id(1) == 0)
    def _():
        wt = wt_ref[...]
        if pad_rows:
            tn = wt.shape[0]
            wt = jnp.concatenate([wt[:pad_at], jnp.zeros((pad_rows, wt.shape[1]), F32),
                                  wt[pad_at:tn - pad_rows]], axis=0)
        w_bf[...] = jnp.transpose(wt).astype(BF16)

    o_ref[...] = jnp.dot(a_ref[...], w_bf[...], preferred_element_type=F32).astype(o_ref.dtype)


def _matmul_wt(a, wt, row_start, n_out, out_dtype, tm, tn, name, pad_at=0, pad_rows=0):
    m, k = a.shape
    return pl.pallas_call(
        functools.partial(_mm_wt_kernel, pad_at=pad_at, pad_rows=pad_rows),
        out_shape=jax.ShapeDtypeStruct((m, n_out), out_dtype),
        grid=(n_out // tn, m // tm),
        in_specs=[pl.BlockSpec((tm, k), lambda j, i: (i, 0)),
                  pl.BlockSpec((pl.Element(tn), pl.Element(k)), lambda j, i: (row_start(j), 0))],
        out_specs=pl.BlockSpec((tm, tn), lambda j, i: (i, j)),
        scratch_shapes=[pltpu.VMEM((k, tn), BF16)],
        compiler_params=_cparams(("arbitrary", "arbitrary")),
        name=name,
    )(a, wt)


def _ep_identity(res, extras):
    return res[0]


def _ep_merge(res, extras):
    return jax.nn.sigmoid(extras[0]) * res[0] + jax.nn.sigmoid(extras[1]) * res[1]


def _ep_residual(res, extras):
    return extras[0] + res[0]


def _ep_ple(res, extras):
    return extras[0] + jax.nn.sigmoid(res[0]) * res[1]


def _ffn_in_kernel(a_ref, wg_ref, wu_ref, cwg_ref, cwu_ref, cbg_ref, cbu_ref, o_ref, carry_ref, w_ref,
                   *, tiles_per_seq):
    i = pl.program_id(1)
    tn = wg_ref.shape[1]

    @pl.when(i == 0)
    def _():
        w_ref[:, 0:tn] = wg_ref[...].astype(BF16)
        w_ref[:, tn:2 * tn] = wu_ref[...].astype(BF16)

    @pl.when(i % tiles_per_seq == 0)
    def _():
        carry_ref[...] = jnp.zeros_like(carry_ref)

    a = a_ref[...]
    tm = a.shape[0]
    row = lax.broadcasted_iota(jnp.int32, (tm, 1), 0)

    def conv(u, carry, cw, cb):
        c1 = carry[7:8, :]
        c2 = carry[6:7, :]
        p1 = jnp.where(row == 0, c1, pltpu.roll(u, 1, 0))
        p2 = jnp.where(row == 0, c2, jnp.where(row == 1, c1, pltpu.roll(u, 2, 0)))
        return cb + p2 * cw[0:1, :] + p1 * cw[1:2, :] + u * cw[2:3, :]

    u = jnp.dot(a, w_ref[...], preferred_element_type=F32)
    ug = u[:, 0:tn]
    uu = u[:, tn:2 * tn]
    gate = conv(ug, carry_ref[0], cwg_ref[...], cbg_ref[...])
    up = conv(uu, carry_ref[1], cwu_ref[...], cbu_ref[...])
    carry_ref[0] = ug[tm - 8:tm, :]
    carry_ref[1] = uu[tm - 8:tm, :]
    o_ref[...] = (gate * jax.nn.sigmoid(gate) * up).astype(o_ref.dtype)


def _ffn_in(h, w, conv_w, conv_b, dff, seq, tm, tn):
    t, k = h.shape
    nj = dff // tn
    kern = functools.partial(_ffn_in_kernel, tiles_per_seq=seq // tm)
    return pl.pallas_call(
        kern,
        out_shape=jax.ShapeDtypeStruct((t, dff), BF16),
        grid=(nj, t // tm),
        in_specs=[pl.BlockSpec((tm, k), lambda j, i: (i, 0)),
                  pl.BlockSpec((k, tn), lambda j, i: (0, j)),
                  pl.BlockSpec((k, tn), lambda j, i: (0, j + nj)),
                  pl.BlockSpec((3, tn), lambda j, i: (0, j)),
                  pl.BlockSpec((3, tn), lambda j, i: (0, j + nj)),
                  pl.BlockSpec((1, tn), lambda j, i: (0, j)),
                  pl.BlockSpec((1, tn), lambda j, i: (0, j + nj))],
        out_specs=pl.BlockSpec((tm, tn), lambda j, i: (i, j)),
        scratch_shapes=[pltpu.VMEM((2, 8, tn), F32), pltpu.VMEM((k, 2 * tn), BF16)],
        compiler_params=_cparams(("arbitrary", "arbitrary")),
        name="ffn_in_conv_gate",
    )(h, w, w, conv_w, conv_w, conv_b, conv_b)


def _rwkv_consts():
    r = lax.broadcasted_iota(jnp.int32, (GROUP_W, GROUP_W), 0) // HEAD
    c = lax.broadcasted_iota(jnp.int32, (GROUP_W, GROUP_W), 1) // HEAD
    bd = r == c
    t = lax.broadcasted_iota(jnp.int32, (CHUNK, GROUP_W), 0)
    s = lax.broadcasted_iota(jnp.int32, (CHUNK, GROUP_W), 1) % HEAD
    tt = lax.broadcasted_iota(jnp.int32, (CHUNK, CHUNK), 0)
    ss = lax.broadcasted_iota(jnp.int32, (CHUNK, CHUNK), 1)
    lane_head = lax.broadcasted_iota(jnp.int32, (CHUNK, LANES), 1) // HEAD
    head_lanes = [lane_head == hh for hh in range(LANES // HEAD)]
    return dict(bd=bd, ones_bd=bd.astype(BF16), strict=s < t, incl=s <= t, head_lanes=head_lanes,
                eye=(s == t).astype(F32), ltri=(ss <= tt).astype(BF16))


def _block_diag(y, c):
    yb = y.astype(BF16)
    zero = jnp.zeros((CHUNK, LANES), BF16)
    heads_per_tile = LANES // HEAD
    rows = []
    for h in range(GROUP_W // HEAD):
        tile = h // heads_per_tile
        piece = jnp.where(c['head_lanes'][h % heads_per_tile], yb[:, tile * LANES:(tile + 1) * LANES], zero)
        rows.append(jnp.concatenate([piece if t == tile else zero for t in range(GROUP_W // LANES)], axis=1))
    return jnp.concatenate(rows, axis=0)


def _mm(a, b_bf16, dims):
    return _dot(a.astype(BF16), b_bf16, dims)


def _segsum(x, c):
    return _mm(x, c['ones_bd'], _NN)


def _cumsum_rows(l_bf16, x):
    xh, xl = _split(x)
    return _dot(l_bf16, xh, _NN) + _dot(l_bf16, xl, _NN)


def _rwkv_chunks(rs, kraws, vs, wpres, apres, gs, prm, ht, c):
    k_k, k_a, r_k, ln_w, ln_b = prm
    n = len(rs)
    rng = range(n)
    lds = [-np.float32(math.exp(-0.5)) * jax.nn.sigmoid(wpres[i]) for i in rng]
    avs = [jax.nn.sigmoid(apres[i]) for i in rng]
    kk0 = [kraws[i] * k_k for i in rng]
    ssq = [_segsum(kk0[i] * kk0[i], c) for i in rng]
    cums = [_cumsum_rows(c['ltri'], lds[i]) for i in rng]
    kks = [kk0[i] / jnp.maximum(jnp.sqrt(ssq[i]), 1e-12) for i in rng]
    ks = [kraws[i] * (1.0 + (avs[i] - 1.0) * k_a) for i in rng]
    bs = [kks[i] * avs[i] for i in rng]
    last = [cums[i][CHUNK - 1:CHUNK, :] for i in rng]
    dec_out = [jnp.exp(-cums[i]) for i in rng]
    to_end = [jnp.exp(last[i] - cums[i]) for i in rng]
    r_t = [rs[i] * jnp.exp(cums[i]) for i in rng]
    a_t = [-kks[i] * jnp.exp(cums[i] - lds[i]) for i in rng]
    ar = [jnp.concatenate([a_t[i], r_t[i]], axis=0).astype(BF16) for i in rng]
    sb = [_dot(ar[i], _block_diag(bs[i] * dec_out[i], c), _NT) for i in rng]
    sk = [_dot(ar[i], _block_diag(ks[i] * dec_out[i], c), _NT) for i in rng]
    s_ab = [jnp.where(c['strict'], sb[i][:CHUNK], 0.0) for i in rng]
    s_rb = [jnp.where(c['incl'], sb[i][CHUNK:], 0.0).astype(BF16) for i in rng]
    s_ak = [jnp.where(c['strict'], sk[i][:CHUNK], 0.0).astype(BF16) for i in rng]
    s_rk = [jnp.where(c['incl'], sk[i][CHUNK:], 0.0).astype(BF16) for i in rng]
    pw = s_ab
    pw_bd = [_block_diag(pw[i], c) for i in rng]
    inv = [c['eye'] + pw[i] for i in rng]
    for _ in range(5):
        pw = [_mm(pw[i], pw_bd[i], _NN) for i in rng]
        pw_bd = [_block_diag(pw[i], c) for i in rng]
        inv = [inv[i] + _mm(inv[i], pw_bd[i], _NN) for i in rng]
    inv = [inv[i].astype(BF16) for i in rng]
    v_bd = [_block_diag(vs[i], c) for i in rng]
    w = [_dot(inv[i], _block_diag(a_t[i], c), _NN) for i in rng]
    y = [_dot(s_ak[i], v_bd[i], _NN) for i in rng]
    uv = [_dot(inv[i], _block_diag(y[i], c), _NN) for i in rng]
    rw = [(r_t[i] + _dot(s_rb[i], _block_diag(w[i], c), _NN)).astype(BF16) for i in rng]
    ov = [_dot(s_rb[i], _block_diag(uv[i], c), _NN) + _dot(s_rk[i], v_bd[i], _NN) for i in rng]
    bk = [jnp.concatenate([bs[i] * to_end[i], ks[i] * to_end[i]], axis=0).astype(BF16) for i in rng]
    m_st = [jnp.where(c['bd'], _mm(w[i], bk[i][:CHUNK], _TN), 0.0).astype(BF16) for i in rng]
    c_st = [jnp.where(c['bd'], _mm(jnp.concatenate([uv[i], vs[i]], axis=0), bk[i], _TN), 0.0) for i in rng]
    bonus = [_segsum(rs[i] * ks[i] * r_k, c) * vs[i] for i in rng]
    outs = []
    for i in rng:
        hb = ht.astype(BF16)
        outs.append(_dot(rw[i], hb, _NT) + ov[i])
        ht = ht * jnp.exp(last[i]) + _dot(hb, m_st[i], _NN) + c_st[i]
    mean = [_segsum(outs[i], c) * (1.0 / HEAD) for i in rng]
    d = [outs[i] - mean[i] for i in rng]
    var = [_segsum(d[i] * d[i], c) * (1.0 / HEAD) for i in rng]
    res = [(d[i] * lax.rsqrt(var[i] + RWKV_GN_EPS) * ln_w + ln_b + bonus[i]) * gs[i] for i in rng]
    return res, ht


def _rwkv_kernel(zr_ref, zk_ref, zv_ref, zl_ref, mur_ref, muk_ref, muv_ref, mul_ref,
                 w0_ref, a0_ref, w2_ref, a2_ref, g2_ref, kk_ref, ka_ref, rk_ref, lnw_ref, lnb_ref,
                 o_ref, ht_ref, carry_ref, carryl_ref, *, n_chunks):
    step = pl.program_id(2)

    @pl.when(step == 0)
    def _():
        ht_ref[...] = jnp.zeros_like(ht_ref)
        carry_ref[...] = jnp.zeros_like(carry_ref)
        carryl_ref[...] = jnp.zeros_like(carryl_ref)

    rows = zr_ref.shape[0]
    row = lax.broadcasted_iota(jnp.int32, (rows, 1), 0)

    def shifted(z, prev_row, mu):
        prev = jnp.where(row == 0, prev_row, pltpu.roll(z, 1, 0))
        return z + (prev - z) * mu

    zr, zk, zv, zl = zr_ref[...], zk_ref[...], zv_ref[...], zl_ref[...]
    r_all = shifted(zr, carry_ref[0, 0:1, :], mur_ref[...])
    k_all = shifted(zk, carry_ref[1, 0:1, :], muk_ref[...])
    v_all = shifted(zv, carry_ref[2, 0:1, :], muv_ref[...])
    l_all = shifted(zl, carryl_ref[0:1, :], mul_ref[...])
    carry_ref[0] = jnp.broadcast_to(zr[rows - 1:rows, :], (8, GROUP_W))
    carry_ref[1] = jnp.broadcast_to(zk[rows - 1:rows, :], (8, GROUP_W))
    carry_ref[2] = jnp.broadcast_to(zv[rows - 1:rows, :], (8, GROUP_W))
    carryl_ref[...] = jnp.broadcast_to(zl[rows - 1:rows, :], carryl_ref.shape)

    wpre_all = w0_ref[...] + _mm(jnp.tanh(l_all[:, 0:128]), w2_ref[...].astype(BF16), _NN)
    apre_all = a0_ref[...] + _mm(l_all[:, 128:256], a2_ref[...].astype(BF16), _NN)
    g_all = _mm(jax.nn.sigmoid(l_all[:, 256:512]), g2_ref[...].astype(BF16), _NN)

    c = _rwkv_consts()
    prm = (kk_ref[...], ka_ref[...], rk_ref[...], lnw_ref[...], lnb_ref[...])
    sls = [slice(ci * CHUNK, (ci + 1) * CHUNK) for ci in range(n_chunks)]
    pick = lambda x: [x[sl] for sl in sls]
    outs, ht = _rwkv_chunks(pick(r_all), pick(k_all), pick(v_all), pick(wpre_all), pick(apre_all), pick(g_all),
                            prm, ht_ref[...], c)
    for sl, out in zip(sls, outs):
        o_ref[sl, :] = out.astype(o_ref.dtype)
    ht_ref[...] = ht


def _rwkv(z3, col_r, col_k, col_v, zl3, col_l, mu_rkv, mu_l, w0, a0, w2p, a2, g2, k_k, k_a, r_k, ln_w, ln_b,
          width, rows):
    bsz, seq, _ = z3.shape
    ng = width // GROUP_W
    gw = GROUP_W

    def zspec(col):
        return pl.BlockSpec((None, rows, gw), lambda b, g, s, o=col // gw: (b, s, o + g))

    def vspec(off=0):
        return pl.BlockSpec((1, gw), lambda b, g, s, o=off: (0, o + g))

    def mspec(k):
        return pl.BlockSpec((k, gw), lambda b, g, s: (0, g))

    in_specs = [zspec(col_r), zspec(col_k), zspec(col_v),
                pl.BlockSpec((None, rows, 512), lambda b, g, s, o=col_l // 512: (b, s, o)),
                vspec(0), vspec(ng), vspec(2 * ng),
                pl.BlockSpec((1, 512), lambda b, g, s: (0, 0)),
                vspec(), vspec(), mspec(128), mspec(128), mspec(256),
                vspec(), vspec(), vspec(), vspec(), vspec()]
    kern = functools.partial(_rwkv_kernel, n_chunks=rows // CHUNK)
    return pl.pallas_call(
        kern,
        out_shape=jax.ShapeDtypeStruct((bsz, seq, width), BF16),
        grid=(bsz, ng, seq // rows),
        in_specs=in_specs,
        out_specs=pl.BlockSpec((None, rows, gw), lambda b, g, s: (b, s, g)),
        scratch_shapes=[pltpu.VMEM((gw, gw), F32), pltpu.VMEM((3, 8, gw), F32), pltpu.VMEM((8, 512), F32)],
        compiler_params=_cparams(("parallel", "parallel", "arbitrary")),
        name="rwkv7_chunked",
    )(z3, z3, z3, zl3, mu_rkv, mu_rkv, mu_rkv, mu_l, w0, a0, w2p, a2, g2, k_k, k_a, r_k, ln_w, ln_b)


ATTN_COLS = 512
ATTN_AHEAD = 2
ATTN_SUM_ROWS = 16


def _diff_attn_kernel(lam_ref, q_ref, k_ref, v_ref, qg_ref, kg_ref, g_ref, o_ref, kn_ref, vt_ref, q2_ref, m_ref,
                      acc_ref, *, tq, out_scale):
    i = pl.program_id(2)
    hd = 2 * HEAD
    nq = vt_ref.shape[0]
    rr = lax.broadcasted_iota(jnp.int32, (hd, hd), 0) // HEAD
    cc = lax.broadcasted_iota(jnp.int32, (hd, hd), 1) // HEAD
    ones_bd = (rr == cc).astype(BF16)

    def qk_norm(x, g):
        ms = _dot2_exact_rhs(x * x, ones_bd, _NN) * (1.0 / HEAD)
        return x * lax.rsqrt(ms + NORM_EPS) * g

    @pl.when(i == 0)
    def _():
        for blk in range(nq):
            rows = slice(blk * tq, (blk + 1) * tq)
            kn_ref[rows, :] = qk_norm(k_ref[rows, :], kg_ref[...]).astype(BF16)
            vt_ref[blk, 0:hd, :] = jnp.transpose(v_ref[rows, :]).astype(BF16)
            vt_ref[blk, hd:hd + ATTN_SUM_ROWS, :] = jnp.ones((ATTN_SUM_ROWS, tq), BF16)

    scale = HEAD ** -0.5 * math.log2(math.e)
    q = (qk_norm(q_ref[...], qg_ref[...]) * scale).astype(BF16)
    lane = lax.broadcasted_iota(jnp.int32, (tq, hd), 1)
    zero = jnp.zeros_like(q)
    q2_ref[0:tq, :] = jnp.where(lane < HEAD, q, zero)
    q2_ref[tq:2 * tq, :] = jnp.where(lane >= HEAD, q, zero)
    m_ref[...] = jnp.full_like(m_ref, -jnp.inf)
    acc_ref[...] = jnp.zeros_like(acc_ref)

    cols = min(ATTN_COLS, 2 * tq)
    n_col = 2 * tq // cols

    def run(blocks):
        ks = [kn_ref[pl.ds(pl.multiple_of(j * tq, tq), tq), :] for j, _ in blocks]
        vts = [vt_ref[j] for j, _ in blocks]
        items = [(b, c) for b in range(len(blocks)) for c in range(n_col)]

        def scores(item):
            b, c = item
            return _dot(ks[b], q2_ref[c * cols:(c + 1) * cols, :], _NT)

        ahead = [scores(it) for it in items[:ATTN_AHEAD]]
        for n, (b, c) in enumerate(items):
            cs = slice(c * cols, (c + 1) * cols)
            s = ahead.pop(0)
            if n + ATTN_AHEAD < len(items):
                ahead.append(scores(items[n + ATTN_AHEAD]))
            if blocks[b][1]:
                kpos = lax.broadcasted_iota(jnp.int32, s.shape, 0)
                qpos = (lax.broadcasted_iota(jnp.int32, s.shape, 1) + c * cols) % tq
                s = jnp.where(kpos // CHUNK <= qpos // CHUNK, s, -jnp.inf)
            m_old = m_ref[:, cs]
            m_new = jnp.maximum(m_old, jnp.max(s, axis=0, keepdims=True))
            p = jnp.exp2((s - m_new).astype(BF16))
            alpha = jnp.exp2(m_old - m_new)
            acc_ref[:, cs] = alpha * acc_ref[:, cs] + _dot(vts[b], p, _NN)
            m_ref[:, cs] = m_new

    def body(jj, carry):
        run([(2 * jj, False), (2 * jj + 1, False)])
        return carry

    lax.fori_loop(0, i // 2, body, 0)

    @pl.when(i % 2 == 1)
    def _():
        run([(i - 1, False), (i, True)])

    @pl.when(i % 2 == 0)
    def _():
        run([(i, True)])

    acc = acc_ref[0:hd, :] / acc_ref[hd:hd + 1, :]
    o = jnp.transpose(acc[:, 0:tq] - lam_ref[0] * acc[:, tq:2 * tq])
    ms = jnp.mean(o * o, axis=-1, keepdims=True)
    o_ref[...] = (o * lax.rsqrt(ms + NORM_EPS) * g_ref[...] * out_scale).astype(o_ref.dtype)


def _diff_attn(z3, col_q, width, q_g, k_g, lam, subln_g, out_scale, tq):
    bsz, seq, _ = z3.shape
    hd = 2 * HEAD
    nh = width // hd
    nq = seq // tq
    oq, ok, ov = (col_q // hd, (col_q + width) // hd, (col_q + 2 * width) // hd)
    gain = lambda g: jnp.tile(g, hd // HEAD).reshape(1, hd)
    vec = pl.BlockSpec((1, hd), lambda b, h, i: (0, 0))
    kern = functools.partial(_diff_attn_kernel, tq=tq, out_scale=out_scale)
    return pl.pallas_call(
        kern,
        out_shape=jax.ShapeDtypeStruct((bsz, seq, width), BF16),
        grid=(bsz, nh, nq),
        in_specs=[pl.BlockSpec(memory_space=pltpu.SMEM),
                  pl.BlockSpec((None, tq, hd), lambda b, h, i: (b, i, oq + h)),
                  pl.BlockSpec((None, seq, hd), lambda b, h, i: (b, 0, ok + h)),
                  pl.BlockSpec((None, seq, hd), lambda b, h, i: (b, 0, ov + h)),
                  vec, vec, vec],
        out_specs=pl.BlockSpec((None, tq, hd), lambda b, h, i: (b, i, h)),
        scratch_shapes=[pltpu.VMEM((seq, hd), BF16), pltpu.VMEM((nq, hd + ATTN_SUM_ROWS, tq), BF16),
                        pltpu.VMEM((2 * tq, hd), BF16), pltpu.VMEM((1, 2 * tq), F32),
                        pltpu.VMEM((hd + ATTN_SUM_ROWS, 2 * tq), F32)],
        compiler_params=_cparams(("parallel", "parallel", "arbitrary")),
        name="diff_flash_attention",
    )(lam.astype(F32).reshape(1), z3, z3, z3, gain(q_g), gain(k_g), subln_g.reshape(1, hd))


def _layer(x, p_i, lambda_init, norm_mix_g, w_in, rwkv_mu, rwkv_w0, rwkv_w2, rwkv_a0, rwkv_a2, rwkv_g2,
           rwkv_k_k, rwkv_k_a, rwkv_r_k, rwkv_ln_w, rwkv_ln_b, q_norm_g, k_norm_g,
           lam_q1, lam_k1, lam_q2, lam_k2, subln_g, w_branch_a, w_branch_b, w_out,
           norm_ffn_g, w_ffn_in, ffn_conv_w, ffn_conv_b, w_ffn_out, norm_ple_g, w_ple_gate, w_ple_proj):
    bsz, seq, d = x.shape
    t = bsz * seq
    rw = w_branch_a.shape[0]
    dw = w_branch_b.shape[0]
    n_w, n_a, n_g = rwkv_w2.shape[0], rwkv_a2.shape[0], rwkv_g2.shape[0]
    lora_pad = 128 - n_w
    assert n_w <= 128 and n_a == 128 and n_g == 256 and rw % GROUP_W == 0 and dw % GROUP_W == 0
    dff = w_ffn_out.shape[0]

    c_lo = 3 * rw
    c_diff = c_lo + n_w + n_a + n_g
    col_q = c_lo
    col_ga = col_q + 3 * dw
    col_gb = col_ga + d
    n_main = col_gb + d
    w_in_t = w_in.T
    mu_rkv = rwkv_mu[:c_lo].reshape(1, c_lo)
    mu_l = jnp.concatenate([rwkv_mu[c_lo:c_lo + n_w], jnp.zeros((lora_pad,), F32),
                            rwkv_mu[c_lo + n_w:]]).reshape(1, 512)
    w2p = jnp.concatenate([rwkv_w2, jnp.zeros((lora_pad, rw), F32)], axis=0)

    tm, tn, nr = TILES['tm'], TILES['tn'], TILES['norm_rows']
    x2 = x.reshape(t, d)
    h = _rmsnorm(x2, norm_mix_g, nr)
    skip = c_diff - c_lo
    assert skip % 8 == 0
    z = _matmul_wt(h, w_in_t, lambda j: pl.multiple_of(jnp.where(j < c_lo // tn, j * tn, j * tn + skip), 8),
                   n_main, F32, tm, tn, "in_proj")
    z_l = _matmul_wt(h, w_in_t, lambda j: c_lo, 512, F32, tm, 512, "in_proj_lora", pad_at=n_w, pad_rows=lora_pad)
    row = lambda v_: v_.reshape(1, -1)
    o_a = _rwkv(z.reshape(bsz, seq, n_main), 0, rw, 2 * rw, z_l.reshape(bsz, seq, 512), 0, mu_rkv, mu_l, row(rwkv_w0), row(rwkv_a0), w2p, rwkv_a2, rwkv_g2,
                row(rwkv_k_k), row(rwkv_k_a), row(rwkv_r_k), row(rwkv_ln_w), row(rwkv_ln_b), rw,
                TILES['rwkv_rows'])
    lam = (jnp.exp(jnp.sum(lam_q1 * lam_k1)) - jnp.exp(jnp.sum(lam_q2 * lam_k2)) + lambda_init)
    o_b = _diff_attn(z.reshape(bsz, seq, n_main), col_q, dw, q_norm_g, k_norm_g, lam, subln_g, 1.0 - lambda_init,
                     TILES['attn'])
    merged = _matmul([(o_a.reshape(t, rw), w_branch_a), (o_b.reshape(t, dw), w_branch_b)],
                     [(z, col_ga // tn), (z, col_gb // tn)], _ep_merge, d, BF16, tm, tn, "branch_merge")
    x2 = _matmul([(merged, w_out)], [(x2, 0)], _ep_residual, d, F32, tm, tn, "out_proj")
    h = _rmsnorm(x2, norm_ffn_g, nr)
    act = _ffn_in(h, w_ffn_in, ffn_conv_w, ffn_conv_b.reshape(1, 2 * dff), dff, seq, tm, TILES['ffn_in_tn'])
    x2 = _matmul([(act, w_ffn_out.astype(BF16))], [(x2, 0)], _ep_residual, d, F32, TILES['ffn_out_tm'],
                 TILES['ffn_out_tn'], "ffn_out")
    h = _rmsnorm(x2, norm_ple_g, nr)
    pe = p_i.reshape(t, -1).astype(BF16)
    x2 = _matmul([(h, w_ple_gate), (pe, w_ple_proj)], [(x2, 0)], _ep_ple, d, F32, tm, tn, "ple_gate")
    return x2.reshape(bsz, seq, d)


def kernel(x, p, norm_mix_g, w_in, rwkv_mu, rwkv_w0, rwkv_w2, rwkv_a0, rwkv_a2, rwkv_g2, rwkv_k_k, rwkv_k_a,
           rwkv_r_k, rwkv_ln_w, rwkv_ln_b, q_norm_g, k_norm_g, lam_q1, lam_k1, lam_q2, lam_k2, subln_g,
           w_branch_a, w_branch_b, w_out, norm_ffn_g, w_ffn_in, ffn_conv_w, ffn_conv_b, w_ffn_out,
           norm_ple_g, w_ple_gate, w_ple_proj):
    depth = p.shape[0]
    for i in range(depth):
        lambda_init = 0.8 - 0.6 * math.exp(-0.3 * i)
        x = _layer(x, p[i], lambda_init, norm_mix_g[i], w_in[i], rwkv_mu[i], rwkv_w0[i], rwkv_w2[i], rwkv_a0[i],
                   rwkv_a2[i], rwkv_g2[i], rwkv_k_k[i], rwkv_k_a[i], rwkv_r_k[i], rwkv_ln_w[i], rwkv_ln_b[i],
                   q_norm_g[i], k_norm_g[i], lam_q1[i], lam_k1[i], lam_q2[i], lam_k2[i], subln_g[i],
                   w_branch_a[i], w_branch_b[i], w_out[i], norm_ffn_g[i], w_ffn_in[i], ffn_conv_w[i],
                   ffn_conv_b[i], w_ffn_out[i], norm_ple_g[i], w_ple_gate[i], w_ple_proj[i])
    return x
```

```python
import functools
import math

import jax
import jax.numpy as jnp
import numpy as np
from jax import lax
from jax.experimental import pallas as pl
from jax.experimental.pallas import tpu as pltpu

F32 = jnp.float32
BF16 = jnp.bfloat16

NORM_EPS = 1e-6
RWKV_GN_EPS = 64e-5
HEAD = 64
CHUNK = 64
GROUP_W = 256
LANES = 128
VMEM_LIMIT = 56 * 1024 * 1024
TILES = dict(tm=1024, tn=512, rwkv_rows=512, attn=512, norm_rows=256, ffn_in_tn=256, ffn_out_tm=512,
             ffn_out_tn=512)


def _cparams(sem):
    return pltpu.CompilerParams(dimension_semantics=sem, vmem_limit_bytes=VMEM_LIMIT)


def _dot(a, b, dims):
    return lax.dot_general(a, b, (dims, ((), ())), preferred_element_type=F32)


_NN = ((1,), (0,))
_NT = ((1,), (1,))
_TN = ((0,), (0,))


def _split(x):
    hi = x.astype(BF16)
    lo = (x - hi.astype(F32)).astype(BF16)
    return hi, lo


def _dot2_exact_rhs(a, b_bf16, dims):
    ah, al = _split(a)
    return _dot(ah, b_bf16, dims) + _dot(al, b_bf16, dims)


def _rmsnorm_kernel(x_ref, g_ref, o_ref):
    x = x_ref[...]
    ms = jnp.mean(x * x, axis=-1, keepdims=True)
    o_ref[...] = (x * lax.rsqrt(ms + NORM_EPS) * g_ref[...]).astype(o_ref.dtype)


def _rmsnorm(x2d, g, tr=256):
    t, d = x2d.shape
    return pl.pallas_call(
        _rmsnorm_kernel,
        out_shape=jax.ShapeDtypeStruct((t, d), BF16),
        grid=(t // tr,),
        in_specs=[pl.BlockSpec((tr, d), lambda i: (i, 0)), pl.BlockSpec((1, d), lambda i: (0, 0))],
        out_specs=pl.BlockSpec((tr, d), lambda i: (i, 0)),
        compiler_params=_cparams(("parallel",)),
        name="rmsnorm",
    )(x2d, g.reshape(1, d))


def _mm_kernel(*refs, n_dots, n_extra, epilogue):
    out_ref = refs[2 * n_dots + n_extra]
    res = []
    for d in range(n_dots):
        res.append(jnp.dot(refs[2 * d][...], refs[2 * d + 1][...], preferred_element_type=F32))
    extras = [refs[2 * n_dots + e][...] for e in range(n_extra)]
    out_ref[...] = epilogue(res, extras).astype(out_ref.dtype)


def _mm_ws_kernel(*refs, n_dots, n_extra, epilogue):
    out_ref = refs[2 * n_dots + n_extra]
    w_bf = refs[2 * n_dots + n_extra + 1:]

    @pl.when(pl.program_id(1) == 0)
    def _():
        for d in range(n_dots):
            w_bf[d][...] = refs[2 * d + 1][...].astype(BF16)

    res = [jnp.dot(refs[2 * d][...], w_bf[d][...], preferred_element_type=F32) for d in range(n_dots)]
    extras = [refs[2 * n_dots + e][...] for e in range(n_extra)]
    out_ref[...] = epilogue(res, extras).astype(out_ref.dtype)


def _matmul(dots, extras, epilogue, n_out, out_dtype, tm, tn, name):
    m = dots[0][0].shape[0]
    stationary = dots[0][1].dtype == F32
    if stationary:
        ij = lambda f: (lambda j, i: f(i, j))
        grid = (n_out // tn, m // tm)
        kern_fn = _mm_ws_kernel
        scratch = [pltpu.VMEM((a.shape[1], tn), BF16) for a, _ in dots]
        sem = ("arbitrary", "arbitrary")
    else:
        ij = lambda f: f
        grid = (m // tm, n_out // tn)
        kern_fn = _mm_kernel
        scratch = []
        sem = ("parallel", "arbitrary")
    in_specs, args = [], []
    for a, w in dots:
        k = a.shape[1]
        in_specs.append(pl.BlockSpec((tm, k), ij(lambda i, j: (i, 0))))
        in_specs.append(pl.BlockSpec((k, tn), ij(lambda i, j: (0, j))))
        args += [a, w]
    for arr, off in extras:
        in_specs.append(pl.BlockSpec((tm, tn), ij(lambda i, j, off=off: (i, j + off))))
        args.append(arr)
    kern = functools.partial(kern_fn, n_dots=len(dots), n_extra=len(extras), epilogue=epilogue)
    return pl.pallas_call(
        kern,
        out_shape=jax.ShapeDtypeStruct((m, n_out), out_dtype),
        grid=grid,
        in_specs=in_specs,
        out_specs=pl.BlockSpec((tm, tn), ij(lambda i, j: (i, j))),
        scratch_shapes=scratch,
        compiler_params=_cparams(sem),
        name=name,
    )(*args)


def _mm_wt_kernel(a_ref, wt_ref, o_ref, w_bf, *, pad_at, pad_rows):
    @pl.when(pl.program_id(1) == 0)
    def _():
        wt = wt_ref[...]
        if pad_rows:
            tn = wt.shape[0]
            wt = jnp.concatenate([wt[:pad_at], jnp.zeros((pad_rows, wt.shape[1]), F32),
                                  wt[pad_at:tn - pad_rows]], axis=0)
        w_bf[...] = jnp.transpose(wt).astype(BF16)

    o_ref[...] = jnp.dot(a_ref[...], w_bf[...], preferred_element_type=F32).astype(o_ref.dtype)


def _matmul_wt(a, wt, row_start, n_out, out_dtype, tm, tn, name, pad_at=0, pad_rows=0):
    m, k = a.shape
    return pl.pallas_call(
        functools.partial(_mm_wt_kernel, pad_at=pad_at, pad_rows=pad_rows),
        out_shape=jax.ShapeDtypeStruct((m, n_out), out_dtype),
        grid=(n_out // tn, m // tm),
        in_specs=[pl.BlockSpec((tm, k), lambda j, i: (i, 0)),
                  pl.BlockSpec((pl.Element(tn), pl.Element(k)), lambda j, i: (row_start(j), 0))],
        out_specs=pl.BlockSpec((tm, tn), lambda j, i: (i, j)),
        scratch_shapes=[pltpu.VMEM((k, tn), BF16)],
        compiler_params=_cparams(("arbitrary", "arbitrary")),
        name=name,
    )(a, wt)


def _ep_identity(res, extras):
    return res[0]


def _ep_merge(res, extras):
    return jax.nn.sigmoid(extras[0]) * res[0] + jax.nn.sigmoid(extras[1]) * res[1]


def _ep_residual(res, extras):
    return extras[0] + res[0]


def _ep_ple(res, extras):
    return extras[0] + jax.nn.sigmoid(res[0]) * res[1]


def _ffn_in_kernel(a_ref, wg_ref, wu_ref, cwg_ref, cwu_ref, cbg_ref, cbu_ref, o_ref, carry_ref, w_ref,
                   *, tiles_per_seq):
    i = pl.program_id(1)
    tn = wg_ref.shape[1]

    @pl.when(i == 0)
    def _():
        w_ref[:, 0:tn] = wg_ref[...].astype(BF16)
        w_ref[:, tn:2 * tn] = wu_ref[...].astype(BF16)

    @pl.when(i % tiles_per_seq == 0)
    def _():
        carry_ref[...] = jnp.zeros_like(carry_ref)

    a = a_ref[...]
    tm = a.shape[0]
    row = lax.broadcasted_iota(jnp.int32, (tm, 1), 0)

    def conv(u, carry, cw, cb):
        c1 = carry[7:8, :]
        c2 = carry[6:7, :]
        p1 = jnp.where(row == 0, c1, pltpu.roll(u, 1, 0))
        p2 = jnp.where(row == 0, c2, jnp.where(row == 1, c1, pltpu.roll(u, 2, 0)))
        return cb + p2 * cw[0:1, :] + p1 * cw[1:2, :] + u * cw[2:3, :]

    u = jnp.dot(a, w_ref[...], preferred_element_type=F32)
    ug = u[:, 0:tn]
    uu = u[:, tn:2 * tn]
    gate = conv(ug, carry_ref[0], cwg_ref[...], cbg_ref[...])
    up = conv(uu, carry_ref[1], cwu_ref[...], cbu_ref[...])
    carry_ref[0] = ug[tm - 8:tm, :]
    carry_ref[1] = uu[tm - 8:tm, :]
    o_ref[...] = (gate * jax.nn.sigmoid(gate) * up).astype(o_ref.dtype)


def _ffn_in(h, w, conv_w, conv_b, dff, seq, tm, tn):
    t, k = h.shape
    nj = dff // tn
    kern = functools.partial(_ffn_in_kernel, tiles_per_seq=seq // tm)
    return pl.pallas_call(
        kern,
        out_shape=jax.ShapeDtypeStruct((t, dff), BF16),
        grid=(nj, t // tm),
        in_specs=[pl.BlockSpec((tm, k), lambda j, i: (i, 0)),
                  pl.BlockSpec((k, tn), lambda j, i: (0, j)),
                  pl.BlockSpec((k, tn), lambda j, i: (0, j + nj)),
                  pl.BlockSpec((3, tn), lambda j, i: (0, j)),
                  pl.BlockSpec((3, tn), lambda j, i: (0, j + nj)),
                  pl.BlockSpec((1, tn), lambda j, i: (0, j)),
                  pl.BlockSpec((1, tn), lambda j, i: (0, j + nj))],
        out_specs=pl.BlockSpec((tm, tn), lambda j, i: (i, j)),
        scratch_shapes=[pltpu.VMEM((2, 8, tn), F32), pltpu.VMEM((k, 2 * tn), BF16)],
        compiler_params=_cparams(("arbitrary", "arbitrary")),
        name="ffn_in_conv_gate",
    )(h, w, w, conv_w, conv_w, conv_b, conv_b)


def _rwkv_consts():
    r = lax.broadcasted_iota(jnp.int32, (GROUP_W, GROUP_W), 0) // HEAD
    c = lax.broadcasted_iota(jnp.int32, (GROUP_W, GROUP_W), 1) // HEAD
    bd = r == c
    t = lax.broadcasted_iota(jnp.int32, (CHUNK, GROUP_W), 0)
    s = lax.broadcasted_iota(jnp.int32, (CHUNK, GROUP_W), 1) % HEAD
    tt = lax.broadcasted_iota(jnp.int32, (CHUNK, CHUNK), 0)
    ss = lax.broadcasted_iota(jnp.int32, (CHUNK, CHUNK), 1)
    lane_head = lax.broadcasted_iota(jnp.int32, (CHUNK, LANES), 1) // HEAD
    head_lanes = [lane_head == hh for hh in range(LANES // HEAD)]
    return dict(bd=bd, ones_bd=bd.astype(BF16), strict=s < t, incl=s <= t, head_lanes=head_lanes,
                eye=(s == t).astype(F32), ltri=(ss <= tt).astype(BF16))


def _block_diag(y, c):
    yb = y.astype(BF16)
    zero = jnp.zeros((CHUNK, LANES), BF16)
    heads_per_tile = LANES // HEAD
    rows = []
    for h in range(GROUP_W // HEAD):
        tile = h // heads_per_tile
        piece = jnp.where(c['head_lanes'][h % heads_per_tile], yb[:, tile * LANES:(tile + 1) * LANES], zero)
        rows.append(jnp.concatenate([piece if t == tile else zero for t in range(GROUP_W // LANES)], axis=1))
    return jnp.concatenate(rows, axis=0)


def _mm(a, b_bf16, dims):
    return _dot(a.astype(BF16), b_bf16, dims)


def _segsum(x, c):
    return _mm(x, c['ones_bd'], _NN)


def _cumsum_rows(l_bf16, x):
    xh, xl = _split(x)
    return _dot(l_bf16, xh, _NN) + _dot(l_bf16, xl, _NN)


def _rwkv_chunks(rs, kraws, vs, wpres, apres, gs, prm, ht, c):
    k_k, k_a, r_k, ln_w, ln_b = prm
    n = len(rs)
    rng = range(n)
    lds = [-np.float32(math.exp(-0.5)) * jax.nn.sigmoid(wpres[i]) for i in rng]
    avs = [jax.nn.sigmoid(apres[i]) for i in rng]
    kk0 = [kraws[i] * k_k for i in rng]
    ssq = [_segsum(kk0[i] * kk0[i], c) for i in rng]
    cums = [_cumsum_rows(c['ltri'], lds[i]) for i in rng]
    kks = [kk0[i] * lax.rsqrt(jnp.maximum(ssq[i], 1e-24)) for i in rng]
    ks = [kraws[i] * (1.0 + (avs[i] - 1.0) * k_a) for i in rng]
    bs = [kks[i] * avs[i] for i in rng]
    last = [cums[i][CHUNK - 1:CHUNK, :] for i in rng]
    dec_out = [jnp.exp(-cums[i]) for i in rng]
    dec_all = [jnp.exp(last[i]) for i in rng]
    r_t = [rs[i] * jnp.exp(cums[i]) for i in rng]
    a_t = [-kks[i] * jnp.exp(cums[i] - lds[i]) for i in rng]
    b_t = [bs[i] * dec_out[i] for i in rng]
    k_t = [ks[i] * dec_out[i] for i in rng]
    ar = [jnp.concatenate([a_t[i], r_t[i]], axis=0).astype(BF16) for i in rng]
    sb = [_dot(ar[i], _block_diag(b_t[i], c), _NT) for i in rng]
    sk = [_dot(ar[i], _block_diag(k_t[i], c), _NT) for i in rng]
    s_ab = [jnp.where(c['strict'], sb[i][:CHUNK], 0.0) for i in rng]
    s_rb = [jnp.where(c['incl'], sb[i][CHUNK:], 0.0).astype(BF16) for i in rng]
    s_ak = [jnp.where(c['strict'], sk[i][:CHUNK], 0.0).astype(BF16) for i in rng]
    s_rk = [jnp.where(c['incl'], sk[i][CHUNK:], 0.0).astype(BF16) for i in rng]
    pw = s_ab
    pw_bd = [_block_diag(pw[i], c) for i in rng]
    inv = [c['eye'] + pw[i] for i in rng]
    for _ in range(5):
        pw = [_mm(pw[i], pw_bd[i], _NN) for i in rng]
        pw_bd = [_block_diag(pw[i], c) for i in rng]
        inv = [inv[i] + _mm(inv[i], pw_bd[i], _NN) for i in rng]
    inv = [inv[i].astype(BF16) for i in rng]
    v_bd = [_block_diag(vs[i], c) for i in rng]
    w = [_dot(inv[i], _block_diag(a_t[i], c), _NN) for i in rng]
    y = [_dot(s_ak[i], v_bd[i], _NN) for i in rng]
    uv = [_dot(inv[i], _block_diag(y[i], c), _NN) for i in rng]
    rw = [(r_t[i] + _dot(s_rb[i], _block_diag(w[i], c), _NN)).astype(BF16) for i in rng]
    ov = [_dot(s_rb[i], _block_diag(uv[i], c), _NN) + _dot(s_rk[i], v_bd[i], _NN) for i in rng]
    bk = [(jnp.concatenate([b_t[i], k_t[i]], axis=0) * dec_all[i]).astype(BF16) for i in rng]
    m_st = [jnp.where(c['bd'], _mm(w[i], bk[i][:CHUNK], _TN), 0.0).astype(BF16) for i in rng]
    c_st = [jnp.where(c['bd'], _mm(jnp.concatenate([uv[i], vs[i]], axis=0), bk[i], _TN), 0.0) for i in rng]
    bonus = [_segsum(rs[i] * ks[i] * r_k, c) * vs[i] for i in rng]
    outs = []
    for i in rng:
        hb = ht.astype(BF16)
        outs.append(_dot(rw[i], hb, _NT) + ov[i])
        ht = ht * dec_all[i] + _dot(hb, m_st[i], _NN) + c_st[i]
    mean = [_segsum(outs[i], c) * (1.0 / HEAD) for i in rng]
    d = [outs[i] - mean[i] for i in rng]
    var = [_segsum(d[i] * d[i], c) * (1.0 / HEAD) for i in rng]
    res = [(d[i] * lax.rsqrt(var[i] + RWKV_GN_EPS) * ln_w + ln_b + bonus[i]) * gs[i] for i in rng]
    return res, ht


def _rwkv_kernel(zr_ref, zk_ref, zv_ref, zl_ref, mur_ref, muk_ref, muv_ref, mul_ref,
                 w0_ref, a0_ref, w2_ref, a2_ref, g2_ref, kk_ref, ka_ref, rk_ref, lnw_ref, lnb_ref,
                 o_ref, ht_ref, carry_ref, carryl_ref, *, n_chunks):
    step = pl.program_id(2)

    @pl.when(step == 0)
    def _():
        ht_ref[...] = jnp.zeros_like(ht_ref)
        carry_ref[...] = jnp.zeros_like(carry_ref)
        carryl_ref[...] = jnp.zeros_like(carryl_ref)

    rows = zr_ref.shape[0]
    row = lax.broadcasted_iota(jnp.int32, (rows, 1), 0)

    def shifted(z, prev_row, mu):
        prev = jnp.where(row == 0, prev_row, pltpu.roll(z, 1, 0))
        return z + (prev - z) * mu

    zr, zk, zv, zl = zr_ref[...], zk_ref[...], zv_ref[...], zl_ref[...]
    r_all = shifted(zr, carry_ref[0, 0:1, :], mur_ref[...])
    k_all = shifted(zk, carry_ref[1, 0:1, :], muk_ref[...])
    v_all = shifted(zv, carry_ref[2, 0:1, :], muv_ref[...])
    l_all = shifted(zl, carryl_ref[0:1, :], mul_ref[...])
    carry_ref[0] = jnp.broadcast_to(zr[rows - 1:rows, :], (8, GROUP_W))
    carry_ref[1] = jnp.broadcast_to(zk[rows - 1:rows, :], (8, GROUP_W))
    carry_ref[2] = jnp.broadcast_to(zv[rows - 1:rows, :], (8, GROUP_W))
    carryl_ref[...] = jnp.broadcast_to(zl[rows - 1:rows, :], carryl_ref.shape)

    wpre_all = w0_ref[...] + _mm(jnp.tanh(l_all[:, 0:128]), w2_ref[...].astype(BF16), _NN)
    apre_all = a0_ref[...] + _mm(l_all[:, 128:256], a2_ref[...].astype(BF16), _NN)
    g_all = _mm(jax.nn.sigmoid(l_all[:, 256:512]), g2_ref[...].astype(BF16), _NN)

    c = _rwkv_consts()
    prm = (kk_ref[...], ka_ref[...], rk_ref[...], lnw_ref[...], lnb_ref[...])
    sls = [slice(ci * CHUNK, (ci + 1) * CHUNK) for ci in range(n_chunks)]
    pick = lambda x: [x[sl] for sl in sls]
    outs, ht = _rwkv_chunks(pick(r_all), pick(k_all), pick(v_all), pick(wpre_all), pick(apre_all), pick(g_all),
                            prm, ht_ref[...], c)
    for sl, out in zip(sls, outs):
        o_ref[sl, :] = out.astype(o_ref.dtype)
    ht_ref[...] = ht


def _rwkv(z3, col_r, col_k, col_v, zl3, col_l, mu_rkv, mu_l, w0, a0, w2p, a2, g2, k_k, k_a, r_k, ln_w, ln_b,
          width, rows):
    bsz, seq, _ = z3.shape
    ng = width // GROUP_W
    gw = GROUP_W

    def zspec(col):
        return pl.BlockSpec((None, rows, gw), lambda b, g, s, o=col // gw: (b, s, o + g))

    def vspec(off=0):
        return pl.BlockSpec((1, gw), lambda b, g, s, o=off: (0, o + g))

    def mspec(k):
        return pl.BlockSpec((k, gw), lambda b, g, s: (0, g))

    in_specs = [zspec(col_r), zspec(col_k), zspec(col_v),
                pl.BlockSpec((None, rows, 512), lambda b, g, s, o=col_l // 512: (b, s, o)),
                vspec(0), vspec(ng), vspec(2 * ng),
                pl.BlockSpec((1, 512), lambda b, g, s: (0, 0)),
                vspec(), vspec(), mspec(128), mspec(128), mspec(256),
                vspec(), vspec(), vspec(), vspec(), vspec()]
    kern = functools.partial(_rwkv_kernel, n_chunks=rows // CHUNK)
    return pl.pallas_call(
        kern,
        out_shape=jax.ShapeDtypeStruct((bsz, seq, width), BF16),
        grid=(bsz, ng, seq // rows),
        in_specs=in_specs,
        out_specs=pl.BlockSpec((None, rows, gw), lambda b, g, s: (b, s, g)),
        scratch_shapes=[pltpu.VMEM((gw, gw), F32), pltpu.VMEM((3, 8, gw), F32), pltpu.VMEM((8, 512), F32)],
        compiler_params=_cparams(("parallel", "parallel", "arbitrary")),
        name="rwkv7_chunked",
    )(z3, z3, z3, zl3, mu_rkv, mu_rkv, mu_rkv, mu_l, w0, a0, w2p, a2, g2, k_k, k_a, r_k, ln_w, ln_b)


ATTN_COLS = 512
ATTN_AHEAD = 2
ATTN_SUM_ROWS = 16


def _diff_attn_kernel(lam_ref, q_ref, k_ref, v_ref, qg_ref, kg_ref, g_ref, o_ref, kn_ref, vt_ref, q2_ref, m_ref,
                      acc_ref, *, tq, out_scale):
    i = pl.program_id(2)
    hd = 2 * HEAD
    nq = vt_ref.shape[0]
    rr = lax.broadcasted_iota(jnp.int32, (hd, hd), 0) // HEAD
    cc = lax.broadcasted_iota(jnp.int32, (hd, hd), 1) // HEAD
    ones_bd = (rr == cc).astype(BF16)

    def qk_norm(x, g):
        ms = _dot2_exact_rhs(x * x, ones_bd, _NN) * (1.0 / HEAD)
        return x * lax.rsqrt(ms + NORM_EPS) * g

    @pl.when(i == 0)
    def _():
        for blk in range(nq):
            rows = slice(blk * tq, (blk + 1) * tq)
            kn_ref[rows, :] = qk_norm(k_ref[rows, :], kg_ref[...]).astype(BF16)
            vt_ref[blk, 0:hd, :] = jnp.transpose(v_ref[rows, :]).astype(BF16)
            vt_ref[blk, hd:hd + ATTN_SUM_ROWS, :] = jnp.ones((ATTN_SUM_ROWS, tq), BF16)

    scale = HEAD ** -0.5 * math.log2(math.e)
    q = (qk_norm(q_ref[...], qg_ref[...]) * scale).astype(BF16)
    lane = lax.broadcasted_iota(jnp.int32, (tq, hd), 1)
    zero = jnp.zeros_like(q)
    q2_ref[0:tq, :] = jnp.where(lane < HEAD, q, zero)
    q2_ref[tq:2 * tq, :] = jnp.where(lane >= HEAD, q, zero)
    m_ref[...] = jnp.full_like(m_ref, -jnp.inf)
    acc_ref[...] = jnp.zeros_like(acc_ref)

    cols = min(ATTN_COLS, 2 * tq)
    n_col = 2 * tq // cols

    def run(blocks):
        ks = [kn_ref[pl.ds(pl.multiple_of(j * tq, tq), tq), :] for j, _ in blocks]
        vts = [vt_ref[j] for j, _ in blocks]
        items = [(b, c) for b in range(len(blocks)) for c in range(n_col)]

        def scores(item):
            b, c = item
            return _dot(ks[b], q2_ref[c * cols:(c + 1) * cols, :], _NT)

        ahead = [scores(it) for it in items[:ATTN_AHEAD]]
        for n, (b, c) in enumerate(items):
            cs = slice(c * cols, (c + 1) * cols)
            s = ahead.pop(0)
            if n + ATTN_AHEAD < len(items):
                ahead.append(scores(items[n + ATTN_AHEAD]))
            if blocks[b][1]:
                kpos = lax.broadcasted_iota(jnp.int32, s.shape, 0)
                qpos = (lax.broadcasted_iota(jnp.int32, s.shape, 1) + c * cols) % tq
                s = jnp.where(kpos // CHUNK <= qpos // CHUNK, s, -jnp.inf)
            m_old = m_ref[:, cs]
            m_new = jnp.maximum(m_old, jnp.max(s, axis=0, keepdims=True))
            p = jnp.exp2((s - m_new).astype(BF16))
            alpha = jnp.exp2(m_old - m_new)
            acc_ref[:, cs] = alpha * acc_ref[:, cs] + _dot(vts[b], p, _NN)
            m_ref[:, cs] = m_new

    def body(jj, carry):
        run([(2 * jj, False), (2 * jj + 1, False)])
        return carry

    lax.fori_loop(0, i // 2, body, 0)

    @pl.when(i % 2 == 1)
    def _():
        run([(i - 1, False), (i, True)])

    @pl.when(i % 2 == 0)
    def _():
        run([(i, True)])

    acc = acc_ref[0:hd, :] / acc_ref[hd:hd + 1, :]
    o = jnp.transpose(acc[:, 0:tq] - lam_ref[0] * acc[:, tq:2 * tq])
    ms = jnp.mean(o * o, axis=-1, keepdims=True)
    o_ref[...] = (o * lax.rsqrt(ms + NORM_EPS) * g_ref[...] * out_scale).astype(o_ref.dtype)


def _diff_attn(z3, col_q, width, q_g, k_g, lam, subln_g, out_scale, tq):
    bsz, seq, _ = z3.shape
    hd = 2 * HEAD
    nh = width // hd
    nq = seq // tq
    oq, ok, ov = (col_q // hd, (col_q + width) // hd, (col_q + 2 * width) // hd)
    gain = lambda g: jnp.tile(g, hd // HEAD).reshape(1, hd)
    vec = pl.BlockSpec((1, hd), lambda b, h, i: (0, 0))
    kern = functools.partial(_diff_attn_kernel, tq=tq, out_scale=out_scale)
    return pl.pallas_call(
        kern,
        out_shape=jax.ShapeDtypeStruct((bsz, seq, width), BF16),
        grid=(bsz, nh, nq),
        in_specs=[pl.BlockSpec(memory_space=pltpu.SMEM),
                  pl.BlockSpec((None, tq, hd), lambda b, h, i: (b, i, oq + h)),
                  pl.BlockSpec((None, seq, hd), lambda b, h, i: (b, 0, ok + h)),
                  pl.BlockSpec((None, seq, hd), lambda b, h, i: (b, 0, ov + h)),
                  vec, vec, vec],
        out_specs=pl.BlockSpec((None, tq, hd), lambda b, h, i: (b, i, h)),
        scratch_shapes=[pltpu.VMEM((seq, hd), BF16), pltpu.VMEM((nq, hd + ATTN_SUM_ROWS, tq), BF16),
                        pltpu.VMEM((2 * tq, hd), BF16), pltpu.VMEM((1, 2 * tq), F32),
                        pltpu.VMEM((hd + ATTN_SUM_ROWS, 2 * tq), F32)],
        compiler_params=_cparams(("parallel", "parallel", "arbitrary")),
        name="diff_flash_attention",
    )(lam.astype(F32).reshape(1), z3, z3, z3, gain(q_g), gain(k_g), subln_g.reshape(1, hd))


def _layer(x, p_i, lambda_init, norm_mix_g, w_in, rwkv_mu, rwkv_w0, rwkv_w2, rwkv_a0, rwkv_a2, rwkv_g2,
           rwkv_k_k, rwkv_k_a, rwkv_r_k, rwkv_ln_w, rwkv_ln_b, q_norm_g, k_norm_g,
           lam_q1, lam_k1, lam_q2, lam_k2, subln_g, w_branch_a, w_branch_b, w_out,
           norm_ffn_g, w_ffn_in, ffn_conv_w, ffn_conv_b, w_ffn_out, norm_ple_g, w_ple_gate, w_ple_proj):
    bsz, seq, d = x.shape
    t = bsz * seq
    rw = w_branch_a.shape[0]
    dw = w_branch_b.shape[0]
    n_w, n_a, n_g = rwkv_w2.shape[0], rwkv_a2.shape[0], rwkv_g2.shape[0]
    lora_pad = 128 - n_w
    assert n_w <= 128 and n_a == 128 and n_g == 256 and rw % GROUP_W == 0 and dw % GROUP_W == 0
    dff = w_ffn_out.shape[0]

    c_lo = 3 * rw
    c_diff = c_lo + n_w + n_a + n_g
    col_q = c_lo
    col_ga = col_q + 3 * dw
    col_gb = col_ga + d
    n_main = col_gb + d
    w_in_t = w_in.T
    mu_rkv = rwkv_mu[:c_lo].reshape(1, c_lo)
    mu_l = jnp.concatenate([rwkv_mu[c_lo:c_lo + n_w], jnp.zeros((lora_pad,), F32),
                            rwkv_mu[c_lo + n_w:]]).reshape(1, 512)
    w2p = jnp.concatenate([rwkv_w2, jnp.zeros((lora_pad, rw), F32)], axis=0)

    tm, tn, nr = TILES['tm'], TILES['tn'], TILES['norm_rows']
    x2 = x.reshape(t, d)
    h = _rmsnorm(x2, norm_mix_g, nr)
    skip = c_diff - c_lo
    assert skip % 8 == 0
    z = _matmul_wt(h, w_in_t, lambda j: pl.multiple_of(jnp.where(j < c_lo // tn, j * tn, j * tn + skip), 8),
                   n_main, F32, tm, tn, "in_proj")
    z_l = _matmul_wt(h, w_in_t, lambda j: c_lo, 512, F32, tm, 512, "in_proj_lora", pad_at=n_w, pad_rows=lora_pad)
    row = lambda v_: v_.reshape(1, -1)
    o_a = _rwkv(z.reshape(bsz, seq, n_main), 0, rw, 2 * rw, z_l.reshape(bsz, seq, 512), 0, mu_rkv, mu_l, row(rwkv_w0), row(rwkv_a0), w2p, rwkv_a2, rwkv_g2,
                row(rwkv_k_k), row(rwkv_k_a), row(rwkv_r_k), row(rwkv_ln_w), row(rwkv_ln_b), rw,
                TILES['rwkv_rows'])
    lam = (jnp.exp(jnp.sum(lam_q1 * lam_k1)) - jnp.exp(jnp.sum(lam_q2 * lam_k2)) + lambda_init)
    o_b = _diff_attn(z.reshape(bsz, seq, n_main), col_q, dw, q_norm_g, k_norm_g, lam, subln_g, 1.0 - lambda_init,
                     TILES['attn'])
    merged = _matmul([(o_a.reshape(t, rw), w_branch_a), (o_b.reshape(t, dw), w_branch_b)],
                     [(z, col_ga // tn), (z, col_gb // tn)], _ep_merge, d, BF16, tm, tn, "branch_merge")
    x2 = _matmul([(merged, w_out)], [(x2, 0)], _ep_residual, d, F32, tm, tn, "out_proj")
    h = _rmsnorm(x2, norm_ffn_g, nr)
    act = _ffn_in(h, w_ffn_in, ffn_conv_w, ffn_conv_b.reshape(1, 2 * dff), dff, seq, tm, TILES['ffn_in_tn'])
    x2 = _matmul([(act, w_ffn_out.astype(BF16))], [(x2, 0)], _ep_residual, d, F32, TILES['ffn_out_tm'],
                 TILES['ffn_out_tn'], "ffn_out")
    h = _rmsnorm(x2, norm_ple_g, nr)
    pe = p_i.reshape(t, -1).astype(BF16)
    x2 = _matmul([(h, w_ple_gate), (pe, w_ple_proj)], [(x2, 0)], _ep_ple, d, F32, tm, tn, "ple_gate")
    return x2.reshape(bsz, seq, d)


def kernel(x, p, norm_mix_g, w_in, rwkv_mu, rwkv_w0, rwkv_w2, rwkv_a0, rwkv_a2, rwkv_g2, rwkv_k_k, rwkv_k_a,
           rwkv_r_k, rwkv_ln_w, rwkv_ln_b, q_norm_g, k_norm_g, lam_q1, lam_k1, lam_q2, lam_k2, subln_g,
           w_branch_a, w_branch_b, w_out, norm_ffn_g, w_ffn_in, ffn_conv_w, ffn_conv_b, w_ffn_out,
           norm_ple_g, w_ple_gate, w_ple_proj):
    depth = p.shape[0]
    for i in range(depth):
        lambda_init = 0.8 - 0.6 * math.exp(-0.3 * i)
        x = _layer(x, p[i], lambda_init, norm_mix_g[i], w_in[i], rwkv_mu[i], rwkv_w0[i], rwkv_w2[i], rwkv_a0[i],
                   rwkv_a2[i], rwkv_g2[i], rwkv_k_k[i], rwkv_k_a[i], rwkv_r_k[i], rwkv_ln_w[i], rwkv_ln_b[i],
                   q_norm_g[i], k_norm_g[i], lam_q1[i], lam_k1[i], lam_q2[i], lam_k2[i], subln_g[i],
                   w_branch_a[i], w_branch_b[i], w_out[i], norm_ffn_g[i], w_ffn_in[i], ffn_conv_w[i],
                   ffn_conv_b[i], w_ffn_out[i], norm_ple_g[i], w_ple_gate[i], w_ple_proj[i])
    return x
```

```python
import functools
import math

import jax
import jax.numpy as jnp
import numpy as np
from jax import lax
from jax.experimental import pallas as pl
from jax.experimental.pallas import tpu as pltpu

F32 = jnp.float32
BF16 = jnp.bfloat16

NORM_EPS = 1e-6
RWKV_GN_EPS = 64e-5
HEAD = 64
CHUNK = 64
GROUP_W = 256
LANES = 128
VMEM_LIMIT = 56 * 1024 * 1024
TILES = dict(tm=1024, tn=512, rwkv_rows=512, attn=512, norm_rows=256, ffn_in_tn=256, ffn_out_tm=512,
             ffn_out_tn=512)


def _cparams(sem):
    return pltpu.CompilerParams(dimension_semantics=sem, vmem_limit_bytes=VMEM_LIMIT)


def _dot(a, b, dims):
    return lax.dot_general(a, b, (dims, ((), ())), preferred_element_type=F32)


_NN = ((1,), (0,))
_NT = ((1,), (1,))
_TN = ((0,), (0,))


def _split(x):
    hi = x.astype(BF16)
    lo = (x - hi.astype(F32)).astype(BF16)
    return hi, lo


def _dot2_exact_rhs(a, b_bf16, dims):
    ah, al = _split(a)
    return _dot(ah, b_bf16, dims) + _dot(al, b_bf16, dims)


def _rmsnorm_kernel(x_ref, g_ref, o_ref):
    x = x_ref[...]
    ms = jnp.mean(x * x, axis=-1, keepdims=True)
    o_ref[...] = (x * lax.rsqrt(ms + NORM_EPS) * g_ref[...]).astype(o_ref.dtype)


def _rmsnorm(x2d, g, tr=256):
    t, d = x2d.shape
    return pl.pallas_call(
        _rmsnorm_kernel,
        out_shape=jax.ShapeDtypeStruct((t, d), BF16),
        grid=(t // tr,),
        in_specs=[pl.BlockSpec((tr, d), lambda i: (i, 0)), pl.BlockSpec((1, d), lambda i: (0, 0))],
        out_specs=pl.BlockSpec((tr, d), lambda i: (i, 0)),
        compiler_params=_cparams(("parallel",)),
        name="rmsnorm",
    )(x2d, g.reshape(1, d))


def _mm_kernel(*refs, n_dots, n_extra, epilogue):
    out_ref = refs[2 * n_dots + n_extra]
    res = []
    for d in range(n_dots):
        res.append(jnp.dot(refs[2 * d][...], refs[2 * d + 1][...], preferred_element_type=F32))
    extras = [refs[2 * n_dots + e][...] for e in range(n_extra)]
    out_ref[...] = epilogue(res, extras).astype(out_ref.dtype)


def _mm_ws_kernel(*refs, n_dots, n_extra, epilogue):
    out_ref = refs[2 * n_dots + n_extra]
    w_bf = refs[2 * n_dots + n_extra + 1:]

    @pl.when(pl.program_id(1) == 0)
    def _():
        for d in range(n_dots):
            w_bf[d][...] = refs[2 * d + 1][...].astype(BF16)

    res = [jnp.dot(refs[2 * d][...], w_bf[d][...], preferred_element_type=F32) for d in range(n_dots)]
    extras = [refs[2 * n_dots + e][...] for e in range(n_extra)]
    out_ref[...] = epilogue(res, extras).astype(out_ref.dtype)


def _matmul(dots, extras, epilogue, n_out, out_dtype, tm, tn, name):
    m = dots[0][0].shape[0]
    stationary = dots[0][1].dtype == F32
    if stationary:
        ij = lambda f: (lambda j, i: f(i, j))
        grid = (n_out // tn, m // tm)
        kern_fn = _mm_ws_kernel
        scratch = [pltpu.VMEM((a.shape[1], tn), BF16) for a, _ in dots]
        sem = ("arbitrary", "arbitrary")
    else:
        ij = lambda f: f
        grid = (m // tm, n_out // tn)
        kern_fn = _mm_kernel
        scratch = []
        sem = ("parallel", "arbitrary")
    in_specs, args = [], []
    for a, w in dots:
        k = a.shape[1]
        in_specs.append(pl.BlockSpec((tm, k), ij(lambda i, j: (i, 0))))
        in_specs.append(pl.BlockSpec((k, tn), ij(lambda i, j: (0, j))))
        args += [a, w]
    for arr, off in extras:
        in_specs.append(pl.BlockSpec((tm, tn), ij(lambda i, j, off=off: (i, j + off))))
        args.append(arr)
    kern = functools.partial(kern_fn, n_dots=len(dots), n_extra=len(extras), epilogue=epilogue)
    return pl.pallas_call(
        kern,
        out_shape=jax.ShapeDtypeStruct((m, n_out), out_dtype),
        grid=grid,
        in_specs=in_specs,
        out_specs=pl.BlockSpec((tm, tn), ij(lambda i, j: (i, j))),
        scratch_shapes=scratch,
        compiler_params=_cparams(sem),
        name=name,
    )(*args)


def _mm_wt_kernel(a_ref, wt_ref, o_ref, w_bf, *, pad_at, pad_rows):
    @pl.when(pl.program_id(1) == 0)
    def _():
        wt = wt_ref[...]
        if pad_rows:
            tn = wt.shape[0]
            wt = jnp.concatenate([wt[:pad_at], jnp.zeros((pad_rows, wt.shape[1]), F32),
                                  wt[pad_at:tn - pad_rows]], axis=0)
        w_bf[...] = jnp.transpose(wt).astype(BF16)

    o_ref[...] = jnp.dot(a_ref[...], w_bf[...], preferred_element_type=F32).astype(o_ref.dtype)


def _matmul_wt(a, wt, row_start, n_out, out_dtype, tm, tn, name, pad_at=0, pad_rows=0):
    m, k = a.shape
    return pl.pallas_call(
        functools.partial(_mm_wt_kernel, pad_at=pad_at, pad_rows=pad_rows),
        out_shape=jax.ShapeDtypeStruct((m, n_out), out_dtype),
        grid=(n_out // tn, m // tm),
        in_specs=[pl.BlockSpec((tm, k), lambda j, i: (i, 0)),
                  pl.BlockSpec((pl.Element(tn), pl.Element(k)), lambda j, i: (row_start(j), 0))],
        out_specs=pl.BlockSpec((tm, tn), lambda j, i: (i, j)),
        scratch_shapes=[pltpu.VMEM((k, tn), BF16)],
        compiler_params=_cparams(("arbitrary", "arbitrary")),
        name=name,
    )(a, wt)


def _ep_identity(res, extras):
    return res[0]


def _ep_merge(res, extras):
    return jax.nn.sigmoid(extras[0]) * res[0] + jax.nn.sigmoid(extras[1]) * res[1]


def _ep_residual(res, extras):
    return extras[0] + res[0]


def _ep_ple(res, extras):
    return extras[0] + jax.nn.sigmoid(res[0]) * res[1]


def _ffn_in_kernel(a_ref, wg_ref, wu_ref, cwg_ref, cwu_ref, cbg_ref, cbu_ref, o_ref, carry_ref, w_ref,
                   *, tiles_per_seq):
    i = pl.program_id(1)
    tn = wg_ref.shape[1]

    @pl.when(i == 0)
    def _():
        w_ref[:, 0:tn] = wg_ref[...].astype(BF16)
        w_ref[:, tn:2 * tn] = wu_ref[...].astype(BF16)

    @pl.when(i % tiles_per_seq == 0)
    def _():
        carry_ref[...] = jnp.zeros_like(carry_ref)

    a = a_ref[...]
    tm = a.shape[0]
    row = lax.broadcasted_iota(jnp.int32, (tm, 1), 0)

    def conv(u, carry, cw, cb):
        c1 = carry[7:8, :]
        c2 = carry[6:7, :]
        p1 = jnp.where(row == 0, c1, pltpu.roll(u, 1, 0))
        p2 = jnp.where(row == 0, c2, jnp.where(row == 1, c1, pltpu.roll(u, 2, 0)))
        return cb + p2 * cw[0:1, :] + p1 * cw[1:2, :] + u * cw[2:3, :]

    u = jnp.dot(a, w_ref[...], preferred_element_type=F32)
    ug = u[:, 0:tn]
    uu = u[:, tn:2 * tn]
    gate = conv(ug, carry_ref[0], cwg_ref[...], cbg_ref[...])
    up = conv(uu, carry_ref[1], cwu_ref[...], cbu_ref[...])
    carry_ref[0] = ug[tm - 8:tm, :]
    carry_ref[1] = uu[tm - 8:tm, :]
    o_ref[...] = (gate * jax.nn.sigmoid(gate) * up).astype(o_ref.dtype)


def _ffn_in(h, w, conv_w, conv_b, dff, seq, tm, tn):
    t, k = h.shape
    nj = dff // tn
    kern = functools.partial(_ffn_in_kernel, tiles_per_seq=seq // tm)
    return pl.pallas_call(
        kern,
        out_shape=jax.ShapeDtypeStruct((t, dff), BF16),
        grid=(nj, t // tm),
        in_specs=[pl.BlockSpec((tm, k), lambda j, i: (i, 0)),
                  pl.BlockSpec((k, tn), lambda j, i: (0, j)),
                  pl.BlockSpec((k, tn), lambda j, i: (0, j + nj)),
                  pl.BlockSpec((3, tn), lambda j, i: (0, j)),
                  pl.BlockSpec((3, tn), lambda j, i: (0, j + nj)),
                  pl.BlockSpec((1, tn), lambda j, i: (0, j)),
                  pl.BlockSpec((1, tn), lambda j, i: (0, j + nj))],
        out_specs=pl.BlockSpec((tm, tn), lambda j, i: (i, j)),
        scratch_shapes=[pltpu.VMEM((2, 8, tn), F32), pltpu.VMEM((k, 2 * tn), BF16)],
        compiler_params=_cparams(("arbitrary", "arbitrary")),
        name="ffn_in_conv_gate",
    )(h, w, w, conv_w, conv_w, conv_b, conv_b)


def _rwkv_consts():
    r = lax.broadcasted_iota(jnp.int32, (GROUP_W, GROUP_W), 0) // HEAD
    c = lax.broadcasted_iota(jnp.int32, (GROUP_W, GROUP_W), 1) // HEAD
    bd = r == c
    t = lax.broadcasted_iota(jnp.int32, (CHUNK, GROUP_W), 0)
    s = lax.broadcasted_iota(jnp.int32, (CHUNK, GROUP_W), 1) % HEAD
    tt = lax.broadcasted_iota(jnp.int32, (CHUNK, CHUNK), 0)
    ss = lax.broadcasted_iota(jnp.int32, (CHUNK, CHUNK), 1)
    lane_head = lax.broadcasted_iota(jnp.int32, (CHUNK, LANES), 1) // HEAD
    head_lanes = [lane_head == hh for hh in range(LANES // HEAD)]
    return dict(bd=bd, ones_bd=bd.astype(BF16), strict=s < t, incl=s <= t, head_lanes=head_lanes,
                eye=(s == t).astype(F32), ltri=(ss <= tt).astype(BF16))


def _block_diag(y, c):
    yb = y.astype(BF16)
    zero = jnp.zeros((CHUNK, LANES), BF16)
    heads_per_tile = LANES // HEAD
    rows = []
    for h in range(GROUP_W // HEAD):
        tile = h // heads_per_tile
        piece = jnp.where(c['head_lanes'][h % heads_per_tile], yb[:, tile * LANES:(tile + 1) * LANES], zero)
        rows.append(jnp.concatenate([piece if t == tile else zero for t in range(GROUP_W // LANES)], axis=1))
    return jnp.concatenate(rows, axis=0)


def _mm(a, b_bf16, dims):
    return _dot(a.astype(BF16), b_bf16, dims)


def _segsum(x, c):
    return _mm(x, c['ones_bd'], _NN)


def _cumsum_rows(l_bf16, x):
    xh, xl = _split(x)
    return _dot(l_bf16, xh, _NN) + _dot(l_bf16, xl, _NN)


def _rwkv_chunks(rs, kraws, vs, wpres, apres, gs, prm, ht, c):
    k_k, k_a, r_k, ln_w, ln_b = prm
    n = len(rs)
    rng = range(n)
    lds = [-np.float32(math.exp(-0.5)) * jax.nn.sigmoid(wpres[i]) for i in rng]
    avs = [jax.nn.sigmoid(apres[i]) for i in rng]
    kk0 = [kraws[i] * k_k for i in rng]
    ssq = [_segsum(kk0[i] * kk0[i], c) for i in rng]
    cums = [_cumsum_rows(c['ltri'], lds[i]) for i in rng]
    kks = [kk0[i] * lax.rsqrt(jnp.maximum(ssq[i], 1e-24)) for i in rng]
    ks = [kraws[i] * (1.0 + (avs[i] - 1.0) * k_a) for i in rng]
    bs = [kks[i] * avs[i] for i in rng]
    last = [cums[i][CHUNK - 1:CHUNK, :] for i in rng]
    dec_out = [jnp.exp(-cums[i]) for i in rng]
    dec_all = [jnp.exp(last[i]) for i in rng]
    r_t = [rs[i] * jnp.exp(cums[i]) for i in rng]
    a_t = [-kks[i] * jnp.exp(cums[i] - lds[i]) for i in rng]
    b_t = [bs[i] * dec_out[i] for i in rng]
    k_t = [ks[i] * dec_out[i] for i in rng]
    ar = [jnp.concatenate([a_t[i], r_t[i]], axis=0).astype(BF16) for i in rng]
    sb = [_dot(ar[i], _block_diag(b_t[i], c), _NT) for i in rng]
    sk = [_dot(ar[i], _block_diag(k_t[i], c), _NT) for i in rng]
    s_ab = [jnp.where(c['strict'], sb[i][:CHUNK], 0.0) for i in rng]
    s_rb = [jnp.where(c['incl'], sb[i][CHUNK:], 0.0).astype(BF16) for i in rng]
    s_ak = [jnp.where(c['strict'], sk[i][:CHUNK], 0.0).astype(BF16) for i in rng]
    s_rk = [jnp.where(c['incl'], sk[i][CHUNK:], 0.0).astype(BF16) for i in rng]
    pw = s_ab
    pw_bd = [_block_diag(pw[i], c) for i in rng]
    inv = [c['eye'] + pw[i] for i in rng]
    for _ in range(5):
        pw = [_mm(pw[i], pw_bd[i], _NN) for i in rng]
        pw_bd = [_block_diag(pw[i], c) for i in rng]
        inv = [inv[i] + _mm(inv[i], pw_bd[i], _NN) for i in rng]
    inv = [inv[i].astype(BF16) for i in rng]
    v_bd = [_block_diag(vs[i], c) for i in rng]
    w = [_dot(inv[i], _block_diag(a_t[i], c), _NN) for i in rng]
    y = [_dot(s_ak[i], v_bd[i], _NN) for i in rng]
    uv = [_dot(inv[i], _block_diag(y[i], c), _NN) for i in rng]
    rw = [(r_t[i] + _dot(s_rb[i], _block_diag(w[i], c), _NN)).astype(BF16) for i in rng]
    ov = [_dot(s_rb[i], _block_diag(uv[i], c), _NN) + _dot(s_rk[i], v_bd[i], _NN) for i in rng]
    bk = [(jnp.concatenate([b_t[i], k_t[i]], axis=0) * dec_all[i]).astype(BF16) for i in rng]
    m_st = [jnp.where(c['bd'], _mm(w[i], bk[i][:CHUNK], _TN), 0.0).astype(BF16) for i in rng]
    c_st = [jnp.where(c['bd'], _mm(jnp.concatenate([uv[i], vs[i]], axis=0), bk[i], _TN), 0.0) for i in rng]
    bonus = [_segsum(rs[i] * ks[i] * r_k, c) * vs[i] for i in rng]
    outs = []
    for i in rng:
        hb = ht.astype(BF16)
        outs.append(_dot(rw[i], hb, _NT) + ov[i])
        ht = ht * dec_all[i] + _dot(hb, m_st[i], _NN) + c_st[i]
    mean = [_segsum(outs[i], c) * (1.0 / HEAD) for i in rng]
    d = [outs[i] - mean[i] for i in rng]
    var = [_segsum(d[i] * d[i], c) * (1.0 / HEAD) for i in rng]
    res = [(d[i] * lax.rsqrt(var[i] + RWKV_GN_EPS) * ln_w + ln_b + bonus[i]) * gs[i] for i in rng]
    return res, ht


def _rwkv_kernel(zr_ref, zk_ref, zv_ref, zl_ref, mur_ref, muk_ref, muv_ref, mul_ref,
                 w0_ref, a0_ref, w2_ref, a2_ref, g2_ref, kk_ref, ka_ref, rk_ref, lnw_ref, lnb_ref,
                 o_ref, ht_ref, carry_ref, carryl_ref, *, n_chunks):
    step = pl.program_id(2)

    @pl.when(step == 0)
    def _():
        ht_ref[...] = jnp.zeros_like(ht_ref)
        carry_ref[...] = jnp.zeros_like(carry_ref)
        carryl_ref[...] = jnp.zeros_like(carryl_ref)

    rows = zr_ref.shape[0]
    row = lax.broadcasted_iota(jnp.int32, (rows, 1), 0)

    def shifted(z, prev_row, mu):
        prev = jnp.where(row == 0, prev_row, pltpu.roll(z, 1, 0))
        return z + (prev - z) * mu

    zr, zk, zv, zl = zr_ref[...], zk_ref[...], zv_ref[...], zl_ref[...]
    r_all = shifted(zr, carry_ref[0, 0:1, :], mur_ref[...])
    k_all = shifted(zk, carry_ref[1, 0:1, :], muk_ref[...])
    v_all = shifted(zv, carry_ref[2, 0:1, :], muv_ref[...])
    l_all = shifted(zl, carryl_ref[0:1, :], mul_ref[...])
    carry_ref[0] = jnp.broadcast_to(zr[rows - 1:rows, :], (8, GROUP_W))
    carry_ref[1] = jnp.broadcast_to(zk[rows - 1:rows, :], (8, GROUP_W))
    carry_ref[2] = jnp.broadcast_to(zv[rows - 1:rows, :], (8, GROUP_W))
    carryl_ref[...] = jnp.broadcast_to(zl[rows - 1:rows, :], carryl_ref.shape)

    wpre_all = w0_ref[...] + _mm(jnp.tanh(l_all[:, 0:128]), w2_ref[...].astype(BF16), _NN)
    apre_all = a0_ref[...] + _mm(l_all[:, 128:256], a2_ref[...].astype(BF16), _NN)
    g_all = _mm(jax.nn.sigmoid(l_all[:, 256:512]), g2_ref[...].astype(BF16), _NN)

    c = _rwkv_consts()
    prm = (kk_ref[...], ka_ref[...], rk_ref[...], lnw_ref[...], lnb_ref[...])
    sls = [slice(ci * CHUNK, (ci + 1) * CHUNK) for ci in range(n_chunks)]
    pick = lambda x: [x[sl] for sl in sls]
    outs, ht = _rwkv_chunks(pick(r_all), pick(k_all), pick(v_all), pick(wpre_all), pick(apre_all), pick(g_all),
                            prm, ht_ref[...], c)
    for sl, out in zip(sls, outs):
        o_ref[sl, :] = out.astype(o_ref.dtype)
    ht_ref[...] = ht


def _rwkv(z3, col_r, col_k, col_v, zl3, col_l, mu_rkv, mu_l, w0, a0, w2p, a2, g2, k_k, k_a, r_k, ln_w, ln_b,
          width, rows):
    bsz, seq, _ = z3.shape
    ng = width // GROUP_W
    gw = GROUP_W

    def zspec(col):
        return pl.BlockSpec((None, rows, gw), lambda b, g, s, o=col // gw: (b, s, o + g))

    def vspec(off=0):
        return pl.BlockSpec((1, gw), lambda b, g, s, o=off: (0, o + g))

    def mspec(k):
        return pl.BlockSpec((k, gw), lambda b, g, s: (0, g))

    in_specs = [zspec(col_r), zspec(col_k), zspec(col_v),
                pl.BlockSpec((None, rows, 512), lambda b, g, s, o=col_l // 512: (b, s, o)),
                vspec(0), vspec(ng), vspec(2 * ng),
                pl.BlockSpec((1, 512), lambda b, g, s: (0, 0)),
                vspec(), vspec(), mspec(128), mspec(128), mspec(256),
                vspec(), vspec(), vspec(), vspec(), vspec()]
    kern = functools.partial(_rwkv_kernel, n_chunks=rows // CHUNK)
    return pl.pallas_call(
        kern,
        out_shape=jax.ShapeDtypeStruct((bsz, seq, width), BF16),
        grid=(bsz, ng, seq // rows),
        in_specs=in_specs,
        out_specs=pl.BlockSpec((None, rows, gw), lambda b, g, s: (b, s, g)),
        scratch_shapes=[pltpu.VMEM((gw, gw), F32), pltpu.VMEM((3, 8, gw), F32), pltpu.VMEM((8, 512), F32)],
        compiler_params=_cparams(("parallel", "parallel", "arbitrary")),
        name="rwkv7_chunked",
    )(z3, z3, z3, zl3, mu_rkv, mu_rkv, mu_rkv, mu_l, w0, a0, w2p, a2, g2, k_k, k_a, r_k, ln_w, ln_b)


ATTN_COLS = 512
ATTN_AHEAD = 2
ATTN_UNROLL = 4
ATTN_SUM_ROWS = 16


def _diff_attn_kernel(lam_ref, q_ref, k_ref, v_ref, qg_ref, kg_ref, g_ref, o_ref, kn_ref, vt_ref, q2_ref, m_ref,
                      acc_ref, *, tq, out_scale):
    i = pl.program_id(2)
    hd = 2 * HEAD
    nq = vt_ref.shape[0]
    rr = lax.broadcasted_iota(jnp.int32, (hd, hd), 0) // HEAD
    cc = lax.broadcasted_iota(jnp.int32, (hd, hd), 1) // HEAD
    ones_bd = (rr == cc).astype(BF16)

    def qk_norm(x, g):
        ms = _dot2_exact_rhs(x * x, ones_bd, _NN) * (1.0 / HEAD)
        return x * lax.rsqrt(ms + NORM_EPS) * g

    @pl.when(i == 0)
    def _():
        for blk in range(nq):
            rows = slice(blk * tq, (blk + 1) * tq)
            kn_ref[rows, :] = qk_norm(k_ref[rows, :], kg_ref[...]).astype(BF16)
            vt_ref[blk, 0:hd, :] = jnp.transpose(v_ref[rows, :]).astype(BF16)
            vt_ref[blk, hd:hd + ATTN_SUM_ROWS, :] = jnp.ones((ATTN_SUM_ROWS, tq), BF16)

    scale = HEAD ** -0.5 * math.log2(math.e)
    q = (qk_norm(q_ref[...], qg_ref[...]) * scale).astype(BF16)
    lane = lax.broadcasted_iota(jnp.int32, (tq, hd), 1)
    zero = jnp.zeros_like(q)
    q2_ref[0:tq, :] = jnp.where(lane < HEAD, q, zero)
    q2_ref[tq:2 * tq, :] = jnp.where(lane >= HEAD, q, zero)
    m_ref[...] = jnp.full_like(m_ref, -jnp.inf)
    acc_ref[...] = jnp.zeros_like(acc_ref)

    cols = min(ATTN_COLS, 2 * tq)
    n_col = 2 * tq // cols

    def run(blocks):
        ks = [kn_ref[pl.ds(pl.multiple_of(j * tq, tq), tq), :] for j, _ in blocks]
        vts = [vt_ref[j] for j, _ in blocks]
        items = [(b, c) for b in range(len(blocks)) for c in range(n_col)]

        def scores(item):
            b, c = item
            return _dot(ks[b], q2_ref[c * cols:(c + 1) * cols, :], _NT)

        ahead = [scores(it) for it in items[:ATTN_AHEAD]]
        for n, (b, c) in enumerate(items):
            cs = slice(c * cols, (c + 1) * cols)
            s = ahead.pop(0)
            if n + ATTN_AHEAD < len(items):
                ahead.append(scores(items[n + ATTN_AHEAD]))
            if blocks[b][1]:
                kpos = lax.broadcasted_iota(jnp.int32, s.shape, 0)
                qpos = (lax.broadcasted_iota(jnp.int32, s.shape, 1) + c * cols) % tq
                s = jnp.where(kpos // CHUNK <= qpos // CHUNK, s, -jnp.inf)
            m_old = m_ref[:, cs]
            m_new = jnp.maximum(m_old, jnp.max(s, axis=0, keepdims=True))
            p = jnp.exp2((s - m_new).astype(BF16))
            alpha = jnp.exp2(m_old - m_new)
            acc_ref[:, cs] = alpha * acc_ref[:, cs] + _dot(vts[b], p, _NN)
            m_ref[:, cs] = m_new

    def body(jj, carry):
        run([(ATTN_UNROLL * jj + t, False) for t in range(ATTN_UNROLL)])
        return carry

    lax.fori_loop(0, i // ATTN_UNROLL, body, 0)
    base = (i // ATTN_UNROLL) * ATTN_UNROLL
    for rem in range(ATTN_UNROLL):

        @pl.when(i % ATTN_UNROLL == rem)
        def _():
            run([(base + t, False) for t in range(rem)] + [(i, True)])

    acc = acc_ref[0:hd, :] / acc_ref[hd:hd + 1, :]
    o = jnp.transpose(acc[:, 0:tq] - lam_ref[0] * acc[:, tq:2 * tq])
    ms = jnp.mean(o * o, axis=-1, keepdims=True)
    o_ref[...] = (o * lax.rsqrt(ms + NORM_EPS) * g_ref[...] * out_scale).astype(o_ref.dtype)


def _diff_attn(z3, col_q, width, q_g, k_g, lam, subln_g, out_scale, tq):
    bsz, seq, _ = z3.shape
    hd = 2 * HEAD
    nh = width // hd
    nq = seq // tq
    oq, ok, ov = (col_q // hd, (col_q + width) // hd, (col_q + 2 * width) // hd)
    gain = lambda g: jnp.tile(g, hd // HEAD).reshape(1, hd)
    vec = pl.BlockSpec((1, hd), lambda b, h, i: (0, 0))
    kern = functools.partial(_diff_attn_kernel, tq=tq, out_scale=out_scale)
    return pl.pallas_call(
        kern,
        out_shape=jax.ShapeDtypeStruct((bsz, seq, width), BF16),
        grid=(bsz, nh, nq),
        in_specs=[pl.BlockSpec(memory_space=pltpu.SMEM),
                  pl.BlockSpec((None, tq, hd), lambda b, h, i: (b, i, oq + h)),
                  pl.BlockSpec((None, seq, hd), lambda b, h, i: (b, 0, ok + h)),
                  pl.BlockSpec((None, seq, hd), lambda b, h, i: (b, 0, ov + h)),
                  vec, vec, vec],
        out_specs=pl.BlockSpec((None, tq, hd), lambda b, h, i: (b, i, h)),
        scratch_shapes=[pltpu.VMEM((seq, hd), BF16), pltpu.VMEM((nq, hd + ATTN_SUM_ROWS, tq), BF16),
                        pltpu.VMEM((2 * tq, hd), BF16), pltpu.VMEM((1, 2 * tq), F32),
                        pltpu.VMEM((hd + ATTN_SUM_ROWS, 2 * tq), F32)],
        compiler_params=_cparams(("parallel", "parallel", "arbitrary")),
        name="diff_flash_attention",
    )(lam.astype(F32).reshape(1), z3, z3, z3, gain(q_g), gain(k_g), subln_g.reshape(1, hd))


def _layer(x, p_i, lambda_init, norm_mix_g, w_in, rwkv_mu, rwkv_w0, rwkv_w2, rwkv_a0, rwkv_a2, rwkv_g2,
           rwkv_k_k, rwkv_k_a, rwkv_r_k, rwkv_ln_w, rwkv_ln_b, q_norm_g, k_norm_g,
           lam_q1, lam_k1, lam_q2, lam_k2, subln_g, w_branch_a, w_branch_b, w_out,
           norm_ffn_g, w_ffn_in, ffn_conv_w, ffn_conv_b, w_ffn_out, norm_ple_g, w_ple_gate, w_ple_proj):
    bsz, seq, d = x.shape
    t = bsz * seq
    rw = w_branch_a.shape[0]
    dw = w_branch_b.shape[0]
    n_w, n_a, n_g = rwkv_w2.shape[0], rwkv_a2.shape[0], rwkv_g2.shape[0]
    lora_pad = 128 - n_w
    assert n_w <= 128 and n_a == 128 and n_g == 256 and rw % GROUP_W == 0 and dw % GROUP_W == 0
    dff = w_ffn_out.shape[0]

    c_lo = 3 * rw
    c_diff = c_lo + n_w + n_a + n_g
    col_q = c_lo
    col_ga = col_q + 3 * dw
    col_gb = col_ga + d
    n_main = col_gb + d
    w_in_t = w_in.T
    mu_rkv = rwkv_mu[:c_lo].reshape(1, c_lo)
    mu_l = jnp.concatenate([rwkv_mu[c_lo:c_lo + n_w], jnp.zeros((lora_pad,), F32),
                            rwkv_mu[c_lo + n_w:]]).reshape(1, 512)
    w2p = jnp.concatenate([rwkv_w2, jnp.zeros((lora_pad, rw), F32)], axis=0)

    tm, tn, nr = TILES['tm'], TILES['tn'], TILES['norm_rows']
    x2 = x.reshape(t, d)
    h = _rmsnorm(x2, norm_mix_g, nr)
    skip = c_diff - c_lo
    assert skip % 8 == 0
    z = _matmul_wt(h, w_in_t, lambda j: pl.multiple_of(jnp.where(j < c_lo // tn, j * tn, j * tn + skip), 8),
                   n_main, F32, tm, tn, "in_proj")
    z_l = _matmul_wt(h, w_in_t, lambda j: c_lo, 512, F32, tm, 512, "in_proj_lora", pad_at=n_w, pad_rows=lora_pad)
    row = lambda v_: v_.reshape(1, -1)
    o_a = _rwkv(z.reshape(bsz, seq, n_main), 0, rw, 2 * rw, z_l.reshape(bsz, seq, 512), 0, mu_rkv, mu_l, row(rwkv_w0), row(rwkv_a0), w2p, rwkv_a2, rwkv_g2,
                row(rwkv_k_k), row(rwkv_k_a), row(rwkv_r_k), row(rwkv_ln_w), row(rwkv_ln_b), rw,
                TILES['rwkv_rows'])
    lam = (jnp.exp(jnp.sum(lam_q1 * lam_k1)) - jnp.exp(jnp.sum(lam_q2 * lam_k2)) + lambda_init)
    o_b = _diff_attn(z.reshape(bsz, seq, n_main), col_q, dw, q_norm_g, k_norm_g, lam, subln_g, 1.0 - lambda_init,
                     TILES['attn'])
    merged = _matmul([(o_a.reshape(t, rw), w_branch_a), (o_b.reshape(t, dw), w_branch_b)],
                     [(z, col_ga // tn), (z, col_gb // tn)], _ep_merge, d, BF16, tm, tn, "branch_merge")
    x2 = _matmul([(merged, w_out)], [(x2, 0)], _ep_residual, d, F32, tm, tn, "out_proj")
    h = _rmsnorm(x2, norm_ffn_g, nr)
    act = _ffn_in(h, w_ffn_in, ffn_conv_w, ffn_conv_b.reshape(1, 2 * dff), dff, seq, tm, TILES['ffn_in_tn'])
    x2 = _matmul([(act, w_ffn_out.astype(BF16))], [(x2, 0)], _ep_residual, d, F32, TILES['ffn_out_tm'],
                 TILES['ffn_out_tn'], "ffn_out")
    h = _rmsnorm(x2, norm_ple_g, nr)
    pe = p_i.reshape(t, -1).astype(BF16)
    x2 = _matmul([(h, w_ple_gate), (pe, w_ple_proj)], [(x2, 0)], _ep_ple, d, F32, tm, tn, "ple_gate")
    return x2.reshape(bsz, seq, d)


def kernel(x, p, norm_mix_g, w_in, rwkv_mu, rwkv_w0, rwkv_w2, rwkv_a0, rwkv_a2, rwkv_g2, rwkv_k_k, rwkv_k_a,
           rwkv_r_k, rwkv_ln_w, rwkv_ln_b, q_norm_g, k_norm_g, lam_q1, lam_k1, lam_q2, lam_k2, subln_g,
           w_branch_a, w_branch_b, w_out, norm_ffn_g, w_ffn_in, ffn_conv_w, ffn_conv_b, w_ffn_out,
           norm_ple_g, w_ple_gate, w_ple_proj):
    depth = p.shape[0]
    for i in range(depth):
        lambda_init = 0.8 - 0.6 * math.exp(-0.3 * i)
        x = _layer(x, p[i], lambda_init, norm_mix_g[i], w_in[i], rwkv_mu[i], rwkv_w0[i], rwkv_w2[i], rwkv_a0[i],
                   rwkv_a2[i], rwkv_g2[i], rwkv_k_k[i], rwkv_k_a[i], rwkv_r_k[i], rwkv_ln_w[i], rwkv_ln_b[i],
                   q_norm_g[i], k_norm_g[i], lam_q1[i], lam_k1[i], lam_q2[i], lam_k2[i], subln_g[i],
                   w_branch_a[i], w_branch_b[i], w_out[i], norm_ffn_g[i], w_ffn_in[i], ffn_conv_w[i],
                   ffn_conv_b[i], w_ffn_out[i], norm_ple_g[i], w_ple_gate[i], w_ple_proj[i])
    return x
```

```python
import functools
import math

import jax
import jax.numpy as jnp
import numpy as np
from jax import lax
from jax.experimental import pallas as pl
from jax.experimental.pallas import tpu as pltpu

F32 = jnp.float32
BF16 = jnp.bfloat16

NORM_EPS = 1e-6
RWKV_GN_EPS = 64e-5
HEAD = 64
CHUNK = 64
GROUP_W = 256
LANES = 128
VMEM_LIMIT = 56 * 1024 * 1024
TILES = dict(tm=1024, tn=512, rwkv_rows=512, attn=512, norm_rows=256, ffn_in_tn=256, ffn_out_tm=512,
             ffn_out_tn=512)


def _cparams(sem):
    return pltpu.CompilerParams(dimension_semantics=sem, vmem_limit_bytes=VMEM_LIMIT)


def _dot(a, b, dims):
    return lax.dot_general(a, b, (dims, ((), ())), preferred_element_type=F32)


_NN = ((1,), (0,))
_NT = ((1,), (1,))
_TN = ((0,), (0,))


def _split(x):
    hi = x.astype(BF16)
    lo = (x - hi.astype(F32)).astype(BF16)
    return hi, lo


def _dot2_exact_rhs(a, b_bf16, dims):
    ah, al = _split(a)
    return _dot(ah, b_bf16, dims) + _dot(al, b_bf16, dims)


def _rmsnorm_kernel(x_ref, g_ref, o_ref):
    x = x_ref[...]
    ms = jnp.mean(x * x, axis=-1, keepdims=True)
    o_ref[...] = (x * lax.rsqrt(ms + NORM_EPS) * g_ref[...]).astype(o_ref.dtype)


def _rmsnorm(x2d, g, tr=256):
    t, d = x2d.shape
    return pl.pallas_call(
        _rmsnorm_kernel,
        out_shape=jax.ShapeDtypeStruct((t, d), BF16),
        grid=(t // tr,),
        in_specs=[pl.BlockSpec((tr, d), lambda i: (i, 0)), pl.BlockSpec((1, d), lambda i: (0, 0))],
        out_specs=pl.BlockSpec((tr, d), lambda i: (i, 0)),
        compiler_params=_cparams(("parallel",)),
        name="rmsnorm",
    )(x2d, g.reshape(1, d))


def _mm_kernel(*refs, n_dots, n_extra, epilogue):
    out_ref = refs[2 * n_dots + n_extra]
    res = []
    for d in range(n_dots):
        res.append(jnp.dot(refs[2 * d][...], refs[2 * d + 1][...], preferred_element_type=F32))
    extras = [refs[2 * n_dots + e][...] for e in range(n_extra)]
    out_ref[...] = epilogue(res, extras).astype(out_ref.dtype)


def _mm_ws_kernel(*refs, n_dots, n_extra, epilogue):
    out_ref = refs[2 * n_dots + n_extra]
    w_bf = refs[2 * n_dots + n_extra + 1:]

    @pl.when(pl.program_id(1) == 0)
    def _():
        for d in range(n_dots):
            w_bf[d][...] = refs[2 * d + 1][...].astype(BF16)

    res = [jnp.dot(refs[2 * d][...], w_bf[d][...], preferred_element_type=F32) for d in range(n_dots)]
    extras = [refs[2 * n_dots + e][...] for e in range(n_extra)]
    out_ref[...] = epilogue(res, extras).astype(out_ref.dtype)


def _matmul(dots, extras, epilogue, n_out, out_dtype, tm, tn, name):
    m = dots[0][0].shape[0]
    stationary = dots[0][1].dtype == F32
    if stationary:
        ij = lambda f: (lambda j, i: f(i, j))
        grid = (n_out // tn, m // tm)
        kern_fn = _mm_ws_kernel
        scratch = [pltpu.VMEM((a.shape[1], tn), BF16) for a, _ in dots]
        sem = ("arbitrary", "arbitrary")
    else:
        ij = lambda f: f
        grid = (m // tm, n_out // tn)
        kern_fn = _mm_kernel
        scratch = []
        sem = ("parallel", "arbitrary")
    in_specs, args = [], []
    for a, w in dots:
        k = a.shape[1]
        in_specs.append(pl.BlockSpec((tm, k), ij(lambda i, j: (i, 0))))
        in_specs.append(pl.BlockSpec((k, tn), ij(lambda i, j: (0, j))))
        args += [a, w]
    for arr, off in extras:
        in_specs.append(pl.BlockSpec((tm, tn), ij(lambda i, j, off=off: (i, j + off))))
        args.append(arr)
    kern = functools.partial(kern_fn, n_dots=len(dots), n_extra=len(extras), epilogue=epilogue)
    return pl.pallas_call(
        kern,
        out_shape=jax.ShapeDtypeStruct((m, n_out), out_dtype),
        grid=grid,
        in_specs=in_specs,
        out_specs=pl.BlockSpec((tm, tn), ij(lambda i, j: (i, j))),
        scratch_shapes=scratch,
        compiler_params=_cparams(sem),
        name=name,
    )(*args)


def _mm_wt_kernel(a_ref, wt_ref, o_ref, w_bf, *, pad_at, pad_rows):
    @pl.when(pl.program_id(1) == 0)
    def _():
        wt = wt_ref[...]
        if pad_rows:
            tn = wt.shape[0]
            wt = jnp.concatenate([wt[:pad_at], jnp.zeros((pad_rows, wt.shape[1]), F32),
                                  wt[pad_at:tn - pad_rows]], axis=0)
        w_bf[...] = jnp.transpose(wt).astype(BF16)

    o_ref[...] = jnp.dot(a_ref[...], w_bf[...], preferred_element_type=F32).astype(o_ref.dtype)


def _matmul_wt(a, wt, row_start, n_out, out_dtype, tm, tn, name, pad_at=0, pad_rows=0):
    m, k = a.shape
    return pl.pallas_call(
        functools.partial(_mm_wt_kernel, pad_at=pad_at, pad_rows=pad_rows),
        out_shape=jax.ShapeDtypeStruct((m, n_out), out_dtype),
        grid=(n_out // tn, m // tm),
        in_specs=[pl.BlockSpec((tm, k), lambda j, i: (i, 0)),
                  pl.BlockSpec((pl.Element(tn), pl.Element(k)), lambda j, i: (row_start(j), 0))],
        out_specs=pl.BlockSpec((tm, tn), lambda j, i: (i, j)),
        scratch_shapes=[pltpu.VMEM((k, tn), BF16)],
        compiler_params=_cparams(("arbitrary", "arbitrary")),
        name=name,
    )(a, wt)


def _ep_identity(res, extras):
    return res[0]


def _ep_merge(res, extras):
    return jax.nn.sigmoid(extras[0]) * res[0] + jax.nn.sigmoid(extras[1]) * res[1]


def _ep_residual(res, extras):
    return extras[0] + res[0]


def _ep_ple(res, extras):
    return extras[0] + jax.nn.sigmoid(res[0]) * res[1]


def _ffn_in_kernel(a_ref, wg_ref, wu_ref, cwg_ref, cwu_ref, cbg_ref, cbu_ref, o_ref, carry_ref, w_ref,
                   *, tiles_per_seq):
    i = pl.program_id(1)
    tn = wg_ref.shape[1]

    @pl.when(i == 0)
    def _():
        w_ref[:, 0:tn] = wg_ref[...].astype(BF16)
        w_ref[:, tn:2 * tn] = wu_ref[...].astype(BF16)

    @pl.when(i % tiles_per_seq == 0)
    def _():
        carry_ref[...] = jnp.zeros_like(carry_ref)

    a = a_ref[...]
    tm = a.shape[0]
    row = lax.broadcasted_iota(jnp.int32, (tm, 1), 0)

    def conv(u, carry, cw, cb):
        c1 = carry[7:8, :]
        c2 = carry[6:7, :]
        p1 = jnp.where(row == 0, c1, pltpu.roll(u, 1, 0))
        p2 = jnp.where(row == 0, c2, jnp.where(row == 1, c1, pltpu.roll(u, 2, 0)))
        return cb + p2 * cw[0:1, :] + p1 * cw[1:2, :] + u * cw[2:3, :]

    u = jnp.dot(a, w_ref[...], preferred_element_type=F32)
    ug = u[:, 0:tn]
    uu = u[:, tn:2 * tn]
    gate = conv(ug, carry_ref[0], cwg_ref[...], cbg_ref[...])
    up = conv(uu, carry_ref[1], cwu_ref[...], cbu_ref[...])
    carry_ref[0] = ug[tm - 8:tm, :]
    carry_ref[1] = uu[tm - 8:tm, :]
    o_ref[...] = (gate * jax.nn.sigmoid(gate) * up).astype(o_ref.dtype)


def _ffn_in(h, w, conv_w, conv_b, dff, seq, tm, tn):
    t, k = h.shape
    nj = dff // tn
    kern = functools.partial(_ffn_in_kernel, tiles_per_seq=seq // tm)
    return pl.pallas_call(
        kern,
        out_shape=jax.ShapeDtypeStruct((t, dff), BF16),
        grid=(nj, t // tm),
        in_specs=[pl.BlockSpec((tm, k), lambda j, i: (i, 0)),
                  pl.BlockSpec((k, tn), lambda j, i: (0, j)),
                  pl.BlockSpec((k, tn), lambda j, i: (0, j + nj)),
                  pl.BlockSpec((3, tn), lambda j, i: (0, j)),
                  pl.BlockSpec((3, tn), lambda j, i: (0, j + nj)),
                  pl.BlockSpec((1, tn), lambda j, i: (0, j)),
                  pl.BlockSpec((1, tn), lambda j, i: (0, j + nj))],
        out_specs=pl.BlockSpec((tm, tn), lambda j, i: (i, j)),
        scratch_shapes=[pltpu.VMEM((2, 8, tn), F32), pltpu.VMEM((k, 2 * tn), BF16)],
        compiler_params=_cparams(("arbitrary", "arbitrary")),
        name="ffn_in_conv_gate",
    )(h, w, w, conv_w, conv_w, conv_b, conv_b)


def _rwkv_consts():
    r = lax.broadcasted_iota(jnp.int32, (GROUP_W, GROUP_W), 0) // HEAD
    c = lax.broadcasted_iota(jnp.int32, (GROUP_W, GROUP_W), 1) // HEAD
    bd = r == c
    t = lax.broadcasted_iota(jnp.int32, (CHUNK, GROUP_W), 0)
    s = lax.broadcasted_iota(jnp.int32, (CHUNK, GROUP_W), 1) % HEAD
    tt = lax.broadcasted_iota(jnp.int32, (CHUNK, CHUNK), 0)
    ss = lax.broadcasted_iota(jnp.int32, (CHUNK, CHUNK), 1)
    lane_head = lax.broadcasted_iota(jnp.int32, (CHUNK, LANES), 1) // HEAD
    head_lanes = [lane_head == hh for hh in range(LANES // HEAD)]
    return dict(bd=bd, ones_bd=bd.astype(BF16), strict=s < t, incl=s <= t, head_lanes=head_lanes,
                eye=(s == t).astype(F32), ltri=(ss <= tt).astype(BF16))


def _block_diag(y, c):
    yb = y.astype(BF16)
    zero = jnp.zeros((CHUNK, LANES), BF16)
    heads_per_tile = LANES // HEAD
    rows = []
    for h in range(GROUP_W // HEAD):
        tile = h // heads_per_tile
        piece = jnp.where(c['head_lanes'][h % heads_per_tile], yb[:, tile * LANES:(tile + 1) * LANES], zero)
        rows.append(jnp.concatenate([piece if t == tile else zero for t in range(GROUP_W // LANES)], axis=1))
    return jnp.concatenate(rows, axis=0)


def _mm(a, b_bf16, dims):
    return _dot(a.astype(BF16), b_bf16, dims)


def _segsum(x, c):
    return _mm(x, c['ones_bd'], _NN)


def _cumsum_rows(l_bf16, x):
    xh, xl = _split(x)
    return _dot(l_bf16, xh, _NN) + _dot(l_bf16, xl, _NN)


def _rwkv_chunks(rs, kraws, vs, wpres, apres, gs, prm, ht, c):
    k_k, k_a, r_k, ln_w, ln_b = prm
    n = len(rs)
    rng = range(n)
    lds = [-np.float32(math.exp(-0.5)) * jax.nn.sigmoid(wpres[i]) for i in rng]
    avs = [jax.nn.sigmoid(apres[i]) for i in rng]
    kk0 = [kraws[i] * k_k for i in rng]
    ssq = [_segsum(kk0[i] * kk0[i], c) for i in rng]
    cums = [_cumsum_rows(c['ltri'], lds[i]) for i in rng]
    kks = [kk0[i] * lax.rsqrt(jnp.maximum(ssq[i], 1e-24)) for i in rng]
    ks = [kraws[i] * (1.0 + (avs[i] - 1.0) * k_a) for i in rng]
    bs = [kks[i] * avs[i] for i in rng]
    last = [cums[i][CHUNK - 1:CHUNK, :] for i in rng]
    dec_out = [jnp.exp(-cums[i]) for i in rng]
    dec_all = [jnp.exp(last[i]) for i in rng]
    r_t = [rs[i] * jnp.exp(cums[i]) for i in rng]
    a_t = [-kks[i] * jnp.exp(cums[i] - lds[i]) for i in rng]
    b_t = [bs[i] * dec_out[i] for i in rng]
    k_t = [ks[i] * dec_out[i] for i in rng]
    ar = [jnp.concatenate([a_t[i], r_t[i]], axis=0).astype(BF16) for i in rng]
    sb = [_dot(ar[i], _block_diag(b_t[i], c), _NT) for i in rng]
    sk = [_dot(ar[i], _block_diag(k_t[i], c), _NT) for i in rng]
    s_ab = [jnp.where(c['strict'], sb[i][:CHUNK], 0.0) for i in rng]
    s_rb = [jnp.where(c['incl'], sb[i][CHUNK:], 0.0).astype(BF16) for i in rng]
    s_ak = [jnp.where(c['strict'], sk[i][:CHUNK], 0.0).astype(BF16) for i in rng]
    s_rk = [jnp.where(c['incl'], sk[i][CHUNK:], 0.0).astype(BF16) for i in rng]
    pw = s_ab
    pw_bd = [_block_diag(pw[i], c) for i in rng]
    inv = [c['eye'] + pw[i] for i in rng]
    for _ in range(5):
        pw = [_mm(pw[i], pw_bd[i], _NN) for i in rng]
        pw_bd = [_block_diag(pw[i], c) for i in rng]
        inv = [inv[i] + _mm(inv[i], pw_bd[i], _NN) for i in rng]
    inv = [inv[i].astype(BF16) for i in rng]
    v_bd = [_block_diag(vs[i], c) for i in rng]
    w = [_dot(inv[i], _block_diag(a_t[i], c), _NN) for i in rng]
    y = [_dot(s_ak[i], v_bd[i], _NN) for i in rng]
    uv = [_dot(inv[i], _block_diag(y[i], c), _NN) for i in rng]
    rw = [(r_t[i] + _dot(s_rb[i], _block_diag(w[i], c), _NN)).astype(BF16) for i in rng]
    ov = [_dot(s_rb[i], _block_diag(uv[i], c), _NN) + _dot(s_rk[i], v_bd[i], _NN) for i in rng]
    bk = [(jnp.concatenate([b_t[i], k_t[i]], axis=0) * dec_all[i]).astype(BF16) for i in rng]
    m_st = [jnp.where(c['bd'], _mm(w[i], bk[i][:CHUNK], _TN), 0.0).astype(BF16) for i in rng]
    c_st = [jnp.where(c['bd'], _mm(jnp.concatenate([uv[i], vs[i]], axis=0), bk[i], _TN), 0.0) for i in rng]
    bonus = [_segsum(rs[i] * ks[i] * r_k, c) * vs[i] for i in rng]
    outs = []
    for i in rng:
        hb = ht.astype(BF16)
        outs.append(_dot(rw[i], hb, _NT) + ov[i])
        ht = ht * dec_all[i] + _dot(hb, m_st[i], _NN) + c_st[i]
    mean = [_segsum(outs[i], c) * (1.0 / HEAD) for i in rng]
    d = [outs[i] - mean[i] for i in rng]
    var = [_segsum(d[i] * d[i], c) * (1.0 / HEAD) for i in rng]
    res = [(d[i] * lax.rsqrt(var[i] + RWKV_GN_EPS) * ln_w + ln_b + bonus[i]) * gs[i] for i in rng]
    return res, ht


def _rwkv_kernel(zr_ref, zk_ref, zv_ref, zl_ref, mur_ref, muk_ref, muv_ref, mul_ref,
                 w0_ref, a0_ref, w2_ref, a2_ref, g2_ref, kk_ref, ka_ref, rk_ref, lnw_ref, lnb_ref,
                 o_ref, ht_ref, carry_ref, carryl_ref, *, n_chunks):
    step = pl.program_id(2)

    @pl.when(step == 0)
    def _():
        ht_ref[...] = jnp.zeros_like(ht_ref)
        carry_ref[...] = jnp.zeros_like(carry_ref)
        carryl_ref[...] = jnp.zeros_like(carryl_ref)

    rows = zr_ref.shape[0]
    row = lax.broadcasted_iota(jnp.int32, (rows, 1), 0)

    def shifted(z, prev_row, mu):
        prev = jnp.where(row == 0, prev_row, pltpu.roll(z, 1, 0))
        return z + (prev - z) * mu

    zr, zk, zv, zl = zr_ref[...], zk_ref[...], zv_ref[...], zl_ref[...]
    r_all = shifted(zr, carry_ref[0, 0:1, :], mur_ref[...])
    k_all = shifted(zk, carry_ref[1, 0:1, :], muk_ref[...])
    v_all = shifted(zv, carry_ref[2, 0:1, :], muv_ref[...])
    l_all = shifted(zl, carryl_ref[0:1, :], mul_ref[...])
    carry_ref[0] = jnp.broadcast_to(zr[rows - 1:rows, :], (8, GROUP_W))
    carry_ref[1] = jnp.broadcast_to(zk[rows - 1:rows, :], (8, GROUP_W))
    carry_ref[2] = jnp.broadcast_to(zv[rows - 1:rows, :], (8, GROUP_W))
    carryl_ref[...] = jnp.broadcast_to(zl[rows - 1:rows, :], carryl_ref.shape)

    wpre_all = w0_ref[...] + _mm(jnp.tanh(l_all[:, 0:128]), w2_ref[...].astype(BF16), _NN)
    apre_all = a0_ref[...] + _mm(l_all[:, 128:256], a2_ref[...].astype(BF16), _NN)
    g_all = _mm(jax.nn.sigmoid(l_all[:, 256:512]), g2_ref[...].astype(BF16), _NN)

    c = _rwkv_consts()
    prm = (kk_ref[...], ka_ref[...], rk_ref[...], lnw_ref[...], lnb_ref[...])
    sls = [slice(ci * CHUNK, (ci + 1) * CHUNK) for ci in range(n_chunks)]
    pick = lambda x: [x[sl] for sl in sls]
    outs, ht = _rwkv_chunks(pick(r_all), pick(k_all), pick(v_all), pick(wpre_all), pick(apre_all), pick(g_all),
                            prm, ht_ref[...], c)
    for sl, out in zip(sls, outs):
        o_ref[sl, :] = out.astype(o_ref.dtype)
    ht_ref[...] = ht


def _rwkv(z3, col_r, col_k, col_v, zl3, col_l, mu_rkv, mu_l, w0, a0, w2p, a2, g2, k_k, k_a, r_k, ln_w, ln_b,
          width, rows):
    bsz, seq, _ = z3.shape
    ng = width // GROUP_W
    gw = GROUP_W

    def zspec(col):
        return pl.BlockSpec((None, rows, gw), lambda b, g, s, o=col // gw: (b, s, o + g))

    def vspec(off=0):
        return pl.BlockSpec((1, gw), lambda b, g, s, o=off: (0, o + g))

    def mspec(k):
        return pl.BlockSpec((k, gw), lambda b, g, s: (0, g))

    in_specs = [zspec(col_r), zspec(col_k), zspec(col_v),
                pl.BlockSpec((None, rows, 512), lambda b, g, s, o=col_l // 512: (b, s, o)),
                vspec(0), vspec(ng), vspec(2 * ng),
                pl.BlockSpec((1, 512), lambda b, g, s: (0, 0)),
                vspec(), vspec(), mspec(128), mspec(128), mspec(256),
                vspec(), vspec(), vspec(), vspec(), vspec()]
    kern = functools.partial(_rwkv_kernel, n_chunks=rows // CHUNK)
    return pl.pallas_call(
        kern,
        out_shape=jax.ShapeDtypeStruct((bsz, seq, width), BF16),
        grid=(bsz, ng, seq // rows),
        in_specs=in_specs,
        out_specs=pl.BlockSpec((None, rows, gw), lambda b, g, s: (b, s, g)),
        scratch_shapes=[pltpu.VMEM((gw, gw), F32), pltpu.VMEM((3, 8, gw), F32), pltpu.VMEM((8, 512), F32)],
        compiler_params=_cparams(("parallel", "parallel", "arbitrary")),
        name="rwkv7_chunked",
    )(z3, z3, z3, zl3, mu_rkv, mu_rkv, mu_rkv, mu_l, w0, a0, w2p, a2, g2, k_k, k_a, r_k, ln_w, ln_b)


ATTN_COLS = 512
ATTN_AHEAD = 2
ATTN_UNROLL = 4
ATTN_SUM_ROWS = 16


def _diff_attn_kernel(lam_ref, q_ref, k_ref, v_ref, qg_ref, kg_ref, g_ref, o_ref, kn_ref, vt_ref, q2_ref, m_ref,
                      acc_ref, *, tq, out_scale):
    i = pl.program_id(2)
    hd = 2 * HEAD
    nq = vt_ref.shape[0]
    rr = lax.broadcasted_iota(jnp.int32, (hd, hd), 0) // HEAD
    cc = lax.broadcasted_iota(jnp.int32, (hd, hd), 1) // HEAD
    ones_bd = (rr == cc).astype(BF16)

    def qk_norm(x, g):
        ms = _dot2_exact_rhs(x * x, ones_bd, _NN) * (1.0 / HEAD)
        return x * lax.rsqrt(ms + NORM_EPS) * g

    @pl.when(i == 0)
    def _():
        for blk in range(nq):
            rows = slice(blk * tq, (blk + 1) * tq)
            kn_ref[rows, :] = qk_norm(k_ref[rows, :], kg_ref[...]).astype(BF16)
            vt_ref[blk, 0:hd, :] = jnp.transpose(v_ref[rows, :]).astype(BF16)
            vt_ref[blk, hd:hd + ATTN_SUM_ROWS, :] = jnp.ones((ATTN_SUM_ROWS, tq), BF16)

    scale = HEAD ** -0.5 * math.log2(math.e)
    q = (qk_norm(q_ref[...], qg_ref[...]) * scale).astype(BF16)
    cols = min(ATTN_COLS, 2 * tq)
    n_col = 2 * tq // cols
    qs = cols // 2
    lane = lax.broadcasted_iota(jnp.int32, (tq, hd), 1)
    zero = jnp.zeros_like(q)
    q_c1 = jnp.where(lane < HEAD, q, zero)
    q_c2 = jnp.where(lane >= HEAD, q, zero)
    for c in range(n_col):
        q2_ref[c * cols:c * cols + qs, :] = q_c1[c * qs:(c + 1) * qs, :]
        q2_ref[c * cols + qs:(c + 1) * cols, :] = q_c2[c * qs:(c + 1) * qs, :]
    m_ref[...] = jnp.full_like(m_ref, -jnp.inf)
    acc_ref[...] = jnp.zeros_like(acc_ref)

    def run(blocks):
        ks = [kn_ref[pl.ds(pl.multiple_of(j * tq, tq), tq), :] for j, _ in blocks]
        vts = [vt_ref[j] for j, _ in blocks]
        items = [(b, c) for b in range(len(blocks)) for c in range(n_col)]

        def n_keys(item):
            b, c = item
            return (c + 1) * qs if blocks[b][1] else tq

        def scores(item):
            b, c = item
            return _dot(ks[b][0:n_keys(item), :], q2_ref[c * cols:(c + 1) * cols, :], _NT)

        ahead = [scores(it) for it in items[:ATTN_AHEAD]]
        for n, (b, c) in enumerate(items):
            cs = slice(c * cols, (c + 1) * cols)
            s = ahead.pop(0)
            if n + ATTN_AHEAD < len(items):
                ahead.append(scores(items[n + ATTN_AHEAD]))
            if blocks[b][1]:
                kpos = lax.broadcasted_iota(jnp.int32, s.shape, 0)
                qpos = lax.broadcasted_iota(jnp.int32, s.shape, 1) % qs + c * qs
                s = jnp.where(kpos // CHUNK <= qpos // CHUNK, s, -jnp.inf)
            m_old = m_ref[:, cs]
            m_new = jnp.maximum(m_old, jnp.max(s, axis=0, keepdims=True))
            p = jnp.exp2((s - m_new).astype(BF16))
            alpha = jnp.exp2(m_old - m_new)
            acc_ref[:, cs] = alpha * acc_ref[:, cs] + _dot(vts[b][:, 0:n_keys((b, c))], p, _NN)
            m_ref[:, cs] = m_new

    def body(jj, carry):
        run([(ATTN_UNROLL * jj + t, False) for t in range(ATTN_UNROLL)])
        return carry

    lax.fori_loop(0, i // ATTN_UNROLL, body, 0)
    base = (i // ATTN_UNROLL) * ATTN_UNROLL
    for rem in range(ATTN_UNROLL):

        @pl.when(i % ATTN_UNROLL == rem)
        def _():
            run([(base + t, False) for t in range(rem)] + [(i, True)])

    acc = acc_ref[0:hd, :] / acc_ref[hd:hd + 1, :]
    o = jnp.concatenate([acc[:, c * cols:c * cols + qs] - lam_ref[0] * acc[:, c * cols + qs:(c + 1) * cols]
                         for c in range(n_col)], axis=1)
    o = jnp.transpose(o)
    ms = jnp.mean(o * o, axis=-1, keepdims=True)
    o_ref[...] = (o * lax.rsqrt(ms + NORM_EPS) * g_ref[...] * out_scale).astype(o_ref.dtype)


def _diff_attn(z3, col_q, width, q_g, k_g, lam, subln_g, out_scale, tq):
    bsz, seq, _ = z3.shape
    hd = 2 * HEAD
    nh = width // hd
    nq = seq // tq
    oq, ok, ov = (col_q // hd, (col_q + width) // hd, (col_q + 2 * width) // hd)
    gain = lambda g: jnp.tile(g, hd // HEAD).reshape(1, hd)
    vec = pl.BlockSpec((1, hd), lambda b, h, i: (0, 0))
    kern = functools.partial(_diff_attn_kernel, tq=tq, out_scale=out_scale)
    return pl.pallas_call(
        kern,
        out_shape=jax.ShapeDtypeStruct((bsz, seq, width), BF16),
        grid=(bsz, nh, nq),
        in_specs=[pl.BlockSpec(memory_space=pltpu.SMEM),
                  pl.BlockSpec((None, tq, hd), lambda b, h, i: (b, i, oq + h)),
                  pl.BlockSpec((None, seq, hd), lambda b, h, i: (b, 0, ok + h)),
                  pl.BlockSpec((None, seq, hd), lambda b, h, i: (b, 0, ov + h)),
                  vec, vec, vec],
        out_specs=pl.BlockSpec((None, tq, hd), lambda b, h, i: (b, i, h)),
        scratch_shapes=[pltpu.VMEM((seq, hd), BF16), pltpu.VMEM((nq, hd + ATTN_SUM_ROWS, tq), BF16),
                        pltpu.VMEM((2 * tq, hd), BF16), pltpu.VMEM((1, 2 * tq), F32),
                        pltpu.VMEM((hd + ATTN_SUM_ROWS, 2 * tq), F32)],
        compiler_params=_cparams(("parallel", "parallel", "arbitrary")),
        name="diff_flash_attention",
    )(lam.astype(F32).reshape(1), z3, z3, z3, gain(q_g), gain(k_g), subln_g.reshape(1, hd))


def _layer(x, p_i, lambda_init, norm_mix_g, w_in, rwkv_mu, rwkv_w0, rwkv_w2, rwkv_a0, rwkv_a2, rwkv_g2,
           rwkv_k_k, rwkv_k_a, rwkv_r_k, rwkv_ln_w, rwkv_ln_b, q_norm_g, k_norm_g,
           lam_q1, lam_k1, lam_q2, lam_k2, subln_g, w_branch_a, w_branch_b, w_out,
           norm_ffn_g, w_ffn_in, ffn_conv_w, ffn_conv_b, w_ffn_out, norm_ple_g, w_ple_gate, w_ple_proj):
    bsz, seq, d = x.shape
    t = bsz * seq
    rw = w_branch_a.shape[0]
    dw = w_branch_b.shape[0]
    n_w, n_a, n_g = rwkv_w2.shape[0], rwkv_a2.shape[0], rwkv_g2.shape[0]
    lora_pad = 128 - n_w
    assert n_w <= 128 and n_a == 128 and n_g == 256 and rw % GROUP_W == 0 and dw % GROUP_W == 0
    dff = w_ffn_out.shape[0]

    c_lo = 3 * rw
    c_diff = c_lo + n_w + n_a + n_g
    col_q = c_lo
    col_ga = col_q + 3 * dw
    col_gb = col_ga + d
    n_main = col_gb + d
    w_in_t = w_in.T
    mu_rkv = rwkv_mu[:c_lo].reshape(1, c_lo)
    mu_l = jnp.concatenate([rwkv_mu[c_lo:c_lo + n_w], jnp.zeros((lora_pad,), F32),
                            rwkv_mu[c_lo + n_w:]]).reshape(1, 512)
    w2p = jnp.concatenate([rwkv_w2, jnp.zeros((lora_pad, rw), F32)], axis=0)

    tm, tn, nr = TILES['tm'], TILES['tn'], TILES['norm_rows']
    x2 = x.reshape(t, d)
    h = _rmsnorm(x2, norm_mix_g, nr)
    skip = c_diff - c_lo
    assert skip % 8 == 0
    z = _matmul_wt(h, w_in_t, lambda j: pl.multiple_of(jnp.where(j < c_lo // tn, j * tn, j * tn + skip), 8),
                   n_main, F32, tm, tn, "in_proj")
    z_l = _matmul_wt(h, w_in_t, lambda j: c_lo, 512, F32, tm, 512, "in_proj_lora", pad_at=n_w, pad_rows=lora_pad)
    row = lambda v_: v_.reshape(1, -1)
    o_a = _rwkv(z.reshape(bsz, seq, n_main), 0, rw, 2 * rw, z_l.reshape(bsz, seq, 512), 0, mu_rkv, mu_l, row(rwkv_w0), row(rwkv_a0), w2p, rwkv_a2, rwkv_g2,
                row(rwkv_k_k), row(rwkv_k_a), row(rwkv_r_k), row(rwkv_ln_w), row(rwkv_ln_b), rw,
                TILES['rwkv_rows'])
    lam = (jnp.exp(jnp.sum(lam_q1 * lam_k1)) - jnp.exp(jnp.sum(lam_q2 * lam_k2)) + lambda_init)
    o_b = _diff_attn(z.reshape(bsz, seq, n_main), col_q, dw, q_norm_g, k_norm_g, lam, subln_g, 1.0 - lambda_init,
                     TILES['attn'])
    merged = _matmul([(o_a.reshape(t, rw), w_branch_a), (o_b.reshape(t, dw), w_branch_b)],
                     [(z, col_ga // tn), (z, col_gb // tn)], _ep_merge, d, BF16, tm, tn, "branch_merge")
    x2 = _matmul([(merged, w_out)], [(x2, 0)], _ep_residual, d, F32, tm, tn, "out_proj")
    h = _rmsnorm(x2, norm_ffn_g, nr)
    act = _ffn_in(h, w_ffn_in, ffn_conv_w, ffn_conv_b.reshape(1, 2 * dff), dff, seq, tm, TILES['ffn_in_tn'])
    x2 = _matmul([(act, w_ffn_out.astype(BF16))], [(x2, 0)], _ep_residual, d, F32, TILES['ffn_out_tm'],
                 TILES['ffn_out_tn'], "ffn_out")
    h = _rmsnorm(x2, norm_ple_g, nr)
    pe = p_i.reshape(t, -1).astype(BF16)
    x2 = _matmul([(h, w_ple_gate), (pe, w_ple_proj)], [(x2, 0)], _ep_ple, d, F32, tm, tn, "ple_gate")
    return x2.reshape(bsz, seq, d)


def kernel(x, p, norm_mix_g, w_in, rwkv_mu, rwkv_w0, rwkv_w2, rwkv_a0, rwkv_a2, rwkv_g2, rwkv_k_k, rwkv_k_a,
           rwkv_r_k, rwkv_ln_w, rwkv_ln_b, q_norm_g, k_norm_g, lam_q1, lam_k1, lam_q2, lam_k2, subln_g,
           w_branch_a, w_branch_b, w_out, norm_ffn_g, w_ffn_in, ffn_conv_w, ffn_conv_b, w_ffn_out,
           norm_ple_g, w_ple_gate, w_ple_proj):
    depth = p.shape[0]
    for i in range(depth):
        lambda_init = 0.8 - 0.6 * math.exp(-0.3 * i)
        x = _layer(x, p[i], lambda_init, norm_mix_g[i], w_in[i], rwkv_mu[i], rwkv_w0[i], rwkv_w2[i], rwkv_a0[i],
                   rwkv_a2[i], rwkv_g2[i], rwkv_k_k[i], rwkv_k_a[i], rwkv_r_k[i], rwkv_ln_w[i], rwkv_ln_b[i],
                   q_norm_g[i], k_norm_g[i], lam_q1[i], lam_k1[i], lam_q2[i], lam_k2[i], subln_g[i],
                   w_branch_a[i], w_branch_b[i], w_out[i], norm_ffn_g[i], w_ffn_in[i], ffn_conv_w[i],
                   ffn_conv_b[i], w_ffn_out[i], norm_ple_g[i], w_ple_gate[i], w_ple_proj[i])
    return x
```

```python
import functools
import math

import jax
import jax.numpy as jnp
import numpy as np
from jax import lax
from jax.experimental import pallas as pl
from jax.experimental.pallas import tpu as pltpu

F32 = jnp.float32
BF16 = jnp.bfloat16

NORM_EPS = 1e-6
RWKV_GN_EPS = 64e-5
HEAD = 64
CHUNK = 64
GROUP_W = 256
LANES = 128
VMEM_LIMIT = 56 * 1024 * 1024
TILES = dict(tm=1024, tn=512, rwkv_rows=1024, attn=512, norm_rows=256, ffn_in_tn=256, ffn_out_tm=512,
             ffn_out_tn=512)


def _cparams(sem):
    return pltpu.CompilerParams(dimension_semantics=sem, vmem_limit_bytes=VMEM_LIMIT)


def _dot(a, b, dims):
    return lax.dot_general(a, b, (dims, ((), ())), preferred_element_type=F32)


_NN = ((1,), (0,))
_NT = ((1,), (1,))
_TN = ((0,), (0,))


def _split(x):
    hi = x.astype(BF16)
    lo = (x - hi.astype(F32)).astype(BF16)
    return hi, lo


def _dot2_exact_rhs(a, b_bf16, dims):
    ah, al = _split(a)
    return _dot(ah, b_bf16, dims) + _dot(al, b_bf16, dims)


def _rmsnorm_kernel(x_ref, g_ref, o_ref):
    x = x_ref[...]
    ms = jnp.mean(x * x, axis=-1, keepdims=True)
    o_ref[...] = (x * lax.rsqrt(ms + NORM_EPS) * g_ref[...]).astype(o_ref.dtype)


def _rmsnorm(x2d, g, tr=256):
    t, d = x2d.shape
    return pl.pallas_call(
        _rmsnorm_kernel,
        out_shape=jax.ShapeDtypeStruct((t, d), BF16),
        grid=(t // tr,),
        in_specs=[pl.BlockSpec((tr, d), lambda i: (i, 0)), pl.BlockSpec((1, d), lambda i: (0, 0))],
        out_specs=pl.BlockSpec((tr, d), lambda i: (i, 0)),
        compiler_params=_cparams(("parallel",)),
        name="rmsnorm",
    )(x2d, g.reshape(1, d))


def _mm_kernel(*refs, n_dots, n_extra, epilogue):
    out_ref = refs[2 * n_dots + n_extra]
    res = []
    for d in range(n_dots):
        res.append(jnp.dot(refs[2 * d][...], refs[2 * d + 1][...], preferred_element_type=F32))
    extras = [refs[2 * n_dots + e][...] for e in range(n_extra)]
    out_ref[...] = epilogue(res, extras).astype(out_ref.dtype)


def _mm_ws_kernel(*refs, n_dots, n_extra, epilogue):
    out_ref = refs[2 * n_dots + n_extra]
    w_bf = refs[2 * n_dots + n_extra + 1:]

    @pl.when(pl.program_id(1) == 0)
    def _():
        for d in range(n_dots):
            w_bf[d][...] = refs[2 * d + 1][...].astype(BF16)

    res = [jnp.dot(refs[2 * d][...], w_bf[d][...], preferred_element_type=F32) for d in range(n_dots)]
    extras = [refs[2 * n_dots + e][...] for e in range(n_extra)]
    out_ref[...] = epilogue(res, extras).astype(out_ref.dtype)


def _matmul(dots, extras, epilogue, n_out, out_dtype, tm, tn, name):
    m = dots[0][0].shape[0]
    stationary = dots[0][1].dtype == F32
    if stationary:
        ij = lambda f: (lambda j, i: f(i, j))
        grid = (n_out // tn, m // tm)
        kern_fn = _mm_ws_kernel
        scratch = [pltpu.VMEM((a.shape[1], tn), BF16) for a, _ in dots]
        sem = ("arbitrary", "arbitrary")
    else:
        ij = lambda f: f
        grid = (m // tm, n_out // tn)
        kern_fn = _mm_kernel
        scratch = []
        sem = ("parallel", "arbitrary")
    in_specs, args = [], []
    for a, w in dots:
        k = a.shape[1]
        in_specs.append(pl.BlockSpec((tm, k), ij(lambda i, j: (i, 0))))
        in_specs.append(pl.BlockSpec((k, tn), ij(lambda i, j: (0, j))))
        args += [a, w]
    for arr, off in extras:
        in_specs.append(pl.BlockSpec((tm, tn), ij(lambda i, j, off=off: (i, j + off))))
        args.append(arr)
    kern = functools.partial(kern_fn, n_dots=len(dots), n_extra=len(extras), epilogue=epilogue)
    return pl.pallas_call(
        kern,
        out_shape=jax.ShapeDtypeStruct((m, n_out), out_dtype),
        grid=grid,
        in_specs=in_specs,
        out_specs=pl.BlockSpec((tm, tn), ij(lambda i, j: (i, j))),
        scratch_shapes=scratch,
        compiler_params=_cparams(sem),
        name=name,
    )(*args)


def _mm_wt_kernel(a_ref, wt_ref, o_ref, w_bf, *, pad_at, pad_rows):
    @pl.when(pl.program_id(1) == 0)
    def _():
        wt = wt_ref[...]
        if pad_rows:
            tn = wt.shape[0]
            wt = jnp.concatenate([wt[:pad_at], jnp.zeros((pad_rows, wt.shape[1]), F32),
                                  wt[pad_at:tn - pad_rows]], axis=0)
        w_bf[...] = jnp.transpose(wt).astype(BF16)

    o_ref[...] = jnp.dot(a_ref[...], w_bf[...], preferred_element_type=F32).astype(o_ref.dtype)


def _matmul_wt(a, wt, row_start, n_out, out_dtype, tm, tn, name, pad_at=0, pad_rows=0):
    m, k = a.shape
    return pl.pallas_call(
        functools.partial(_mm_wt_kernel, pad_at=pad_at, pad_rows=pad_rows),
        out_shape=jax.ShapeDtypeStruct((m, n_out), out_dtype),
        grid=(n_out // tn, m // tm),
        in_specs=[pl.BlockSpec((tm, k), lambda j, i: (i, 0)),
                  pl.BlockSpec((pl.Element(tn), pl.Element(k)), lambda j, i: (row_start(j), 0))],
        out_specs=pl.BlockSpec((tm, tn), lambda j, i: (i, j)),
        scratch_shapes=[pltpu.VMEM((k, tn), BF16)],
        compiler_params=_cparams(("arbitrary", "arbitrary")),
        name=name,
    )(a, wt)


def _ep_identity(res, extras):
    return res[0]


def _ep_merge(res, extras):
    return jax.nn.sigmoid(extras[0]) * res[0] + jax.nn.sigmoid(extras[1]) * res[1]


def _ep_residual(res, extras):
    return extras[0] + res[0]


def _ep_ple(res, extras):
    return extras[0] + jax.nn.sigmoid(res[0]) * res[1]


def _ffn_in_kernel(a_ref, wg_ref, wu_ref, cwg_ref, cwu_ref, cbg_ref, cbu_ref, o_ref, carry_ref, w_ref,
                   *, tiles_per_seq):
    i = pl.program_id(1)
    tn = wg_ref.shape[1]

    @pl.when(i == 0)
    def _():
        w_ref[:, 0:tn] = wg_ref[...].astype(BF16)
        w_ref[:, tn:2 * tn] = wu_ref[...].astype(BF16)

    @pl.when(i % tiles_per_seq == 0)
    def _():
        carry_ref[...] = jnp.zeros_like(carry_ref)

    a = a_ref[...]
    tm = a.shape[0]
    row = lax.broadcasted_iota(jnp.int32, (tm, 1), 0)

    def conv(u, carry, cw, cb):
        c1 = carry[7:8, :]
        c2 = carry[6:7, :]
        p1 = jnp.where(row == 0, c1, pltpu.roll(u, 1, 0))
        p2 = jnp.where(row == 0, c2, jnp.where(row == 1, c1, pltpu.roll(u, 2, 0)))
        return cb + p2 * cw[0:1, :] + p1 * cw[1:2, :] + u * cw[2:3, :]

    u = jnp.dot(a, w_ref[...], preferred_element_type=F32)
    ug = u[:, 0:tn]
    uu = u[:, tn:2 * tn]
    gate = conv(ug, carry_ref[0], cwg_ref[...], cbg_ref[...])
    up = conv(uu, carry_ref[1], cwu_ref[...], cbu_ref[...])
    carry_ref[0] = ug[tm - 8:tm, :]
    carry_ref[1] = uu[tm - 8:tm, :]
    o_ref[...] = (gate * jax.nn.sigmoid(gate) * up).astype(o_ref.dtype)


def _ffn_in(h, w, conv_w, conv_b, dff, seq, tm, tn):
    t, k = h.shape
    nj = dff // tn
    kern = functools.partial(_ffn_in_kernel, tiles_per_seq=seq // tm)
    return pl.pallas_call(
        kern,
        out_shape=jax.ShapeDtypeStruct((t, dff), BF16),
        grid=(nj, t // tm),
        in_specs=[pl.BlockSpec((tm, k), lambda j, i: (i, 0)),
                  pl.BlockSpec((k, tn), lambda j, i: (0, j)),
                  pl.BlockSpec((k, tn), lambda j, i: (0, j + nj)),
                  pl.BlockSpec((3, tn), lambda j, i: (0, j)),
                  pl.BlockSpec((3, tn), lambda j, i: (0, j + nj)),
                  pl.BlockSpec((1, tn), lambda j, i: (0, j)),
                  pl.BlockSpec((1, tn), lambda j, i: (0, j + nj))],
        out_specs=pl.BlockSpec((tm, tn), lambda j, i: (i, j)),
        scratch_shapes=[pltpu.VMEM((2, 8, tn), F32), pltpu.VMEM((k, 2 * tn), BF16)],
        compiler_params=_cparams(("arbitrary", "arbitrary")),
        name="ffn_in_conv_gate",
    )(h, w, w, conv_w, conv_w, conv_b, conv_b)


def _rwkv_consts():
    r = lax.broadcasted_iota(jnp.int32, (GROUP_W, GROUP_W), 0) // HEAD
    c = lax.broadcasted_iota(jnp.int32, (GROUP_W, GROUP_W), 1) // HEAD
    bd = r == c
    t = lax.broadcasted_iota(jnp.int32, (CHUNK, GROUP_W), 0)
    s = lax.broadcasted_iota(jnp.int32, (CHUNK, GROUP_W), 1) % HEAD
    tt = lax.broadcasted_iota(jnp.int32, (CHUNK, CHUNK), 0)
    ss = lax.broadcasted_iota(jnp.int32, (CHUNK, CHUNK), 1)
    lane_head = lax.broadcasted_iota(jnp.int32, (CHUNK, LANES), 1) // HEAD
    head_lanes = [lane_head == hh for hh in range(LANES // HEAD)]
    col_head = lax.broadcasted_iota(jnp.int32, (HEAD, GROUP_W), 1) // HEAD
    head_cols = [col_head == hh for hh in range(GROUP_W // HEAD)]
    return dict(bd=bd, ones_bd=bd.astype(BF16), strict=s < t, incl=s <= t, head_lanes=head_lanes,
                head_cols=head_cols,
                eye=(s == t).astype(F32), ltri=(ss <= tt).astype(BF16))


def _block_diag(y, c):
    yb = y.astype(BF16)
    zero = jnp.zeros((CHUNK, LANES), BF16)
    heads_per_tile = LANES // HEAD
    rows = []
    for h in range(GROUP_W // HEAD):
        tile = h // heads_per_tile
        piece = jnp.where(c['head_lanes'][h % heads_per_tile], yb[:, tile * LANES:(tile + 1) * LANES], zero)
        rows.append(jnp.concatenate([piece if t == tile else zero for t in range(GROUP_W // LANES)], axis=1))
    return jnp.concatenate(rows, axis=0)


def _fold_heads(x, c):
    out = None
    for h in range(GROUP_W // HEAD):
        part = jnp.where(c['head_cols'][h], x[h * HEAD:(h + 1) * HEAD, :], 0.0)
        out = part if out is None else out + part
    return out


def _mm(a, b_bf16, dims):
    return _dot(a.astype(BF16), b_bf16, dims)


def _segsum(x, c):
    return _mm(x, c['ones_bd'], _NN)


def _segsum_many(xs, c):
    tot = _segsum(jnp.concatenate(xs, axis=0), c)
    return [tot[i * CHUNK:(i + 1) * CHUNK, :] for i in range(len(xs))]


def _cumsum_rows(l_bf16, x):
    xh, xl = _split(x)
    return _dot(l_bf16, xh, _NN) + _dot(l_bf16, xl, _NN)


RWKV_GROUP = 8
RWKV_LAG = 3


def _rwkv_pipeline(rs, kraws, vs, wpres, apres, gs, prm, ht, c):
    k_k, k_a, r_k, ln_w, ln_b = prm
    n = len(rs)
    st = [dict() for _ in range(n)]

    def stages(idx):
        sub = [st[i] for i in idx]
        for i, s in zip(idx, sub):
            s['ld'] = -np.float32(math.exp(-0.5)) * jax.nn.sigmoid(wpres[i])
            s['kk0'] = kraws[i] * k_k
        for s, ssq in zip(sub, _segsum_many([s['kk0'] * s['kk0'] for s in sub], c)):
            s['ssq'] = ssq
        for s in sub:
            s['cum'] = _cumsum_rows(c['ltri'], s['ld'])
        yield
        for i, s in zip(idx, sub):
            av = jax.nn.sigmoid(apres[i])
            kk = s['kk0'] * lax.rsqrt(jnp.maximum(s['ssq'], 1e-24))
            k = kraws[i] * (1.0 + (av - 1.0) * k_a)
            last = s['cum'][CHUNK - 1:CHUNK, :]
            dec_out = jnp.exp(-s['cum'])
            s['dec_all'] = jnp.exp(last)
            s['r_t'] = rs[i] * jnp.exp(s['cum'])
            s['a_t'] = -kk * jnp.exp(s['cum'] - s['ld'])
            s['b_t'] = kk * av * dec_out
            s['k_t'] = k * dec_out
            s['ar'] = jnp.concatenate([s['a_t'], s['r_t']], axis=0).astype(BF16)
            s['bonus_in'] = rs[i] * k * r_k
        yield
        for s in sub:
            s['sb'] = _dot(s['ar'], _block_diag(s['b_t'], c), _NT)
        for s in sub:
            s['sk'] = _dot(s['ar'], _block_diag(s['k_t'], c), _NT)
        for s in sub:
            s['pw'] = jnp.where(c['strict'], s['sb'][:CHUNK], 0.0)
            s['s_rb'] = jnp.where(c['incl'], s['sb'][CHUNK:], 0.0).astype(BF16)
            s['s_ak'] = jnp.where(c['strict'], s['sk'][:CHUNK], 0.0).astype(BF16)
            s['s_rk'] = jnp.where(c['incl'], s['sk'][CHUNK:], 0.0).astype(BF16)
            s['inv'] = c['eye'] + s['pw']
        yield
        for s in sub:
            s['pw'] = _mm(s['pw'], _block_diag(s['pw'], c), _NN)
        yield
        for _ in range(4):
            for s in sub:
                both = _mm(jnp.concatenate([s['pw'], s['inv']], axis=0), _block_diag(s['pw'], c), _NN)
                s['pw'] = both[:CHUNK]
                s['inv'] = s['inv'] + both[CHUNK:]
            yield
        for s in sub:
            s['inv'] = (s['inv'] + _mm(s['inv'], _block_diag(s['pw'], c), _NN)).astype(BF16)
        for i, s in zip(idx, sub):
            s['v_bd'] = _block_diag(vs[i], c)
        for s in sub:
            s['w'] = _dot(s['inv'], _block_diag(s['a_t'], c), _NN)
        for s in sub:
            both = _dot(jnp.concatenate([s['s_ak'], s['s_rk']], axis=0), s['v_bd'], _NN)
            s['y'] = both[:CHUNK]
            s['ov_v'] = both[CHUNK:]
        yield
        for s in sub:
            s['uv'] = _dot(s['inv'], _block_diag(s['y'], c), _NN)
        for s in sub:
            s['rw'] = (s['r_t'] + _dot(s['s_rb'], _block_diag(s['w'], c), _NN)).astype(BF16)
        yield
        for s in sub:
            s['ov'] = _dot(s['s_rb'], _block_diag(s['uv'], c), _NN) + s['ov_v']
        for s in sub:
            s['bk'] = (jnp.concatenate([s['b_t'], s['k_t']], axis=0) * s['dec_all']).astype(BF16)
        for s in sub:
            s['m_st'] = jnp.where(c['bd'], _mm(s['w'], s['bk'][:CHUNK], _TN), 0.0).astype(BF16)
        for i, s in zip(idx, sub):
            s['c_st'] = _fold_heads(_mm(jnp.concatenate([s['uv'], vs[i]], axis=0), s['bk'], _TN), c)
        for i, s, tot in zip(idx, sub, _segsum_many([s['bonus_in'] for s in sub], c)):
            s['bonus'] = tot * vs[i]
        yield

    gens = [stages(list(range(n))[j:j + RWKV_GROUP]) for j in range(0, n, RWKV_GROUP)]
    live = list(range(len(gens)))
    tick = 0
    while live:
        for g in list(live):
            if tick >= g * RWKV_LAG:
                try:
                    next(gens[g])
                except StopIteration:
                    live.remove(g)
        tick += 1
    outs = []
    for i in range(n):
        s = st[i]
        outs.append(_dot(s['rw'], _block_diag(ht, c), _NT) + s['ov'])
        ht = ht * s['dec_all'] + _mm(ht, s['m_st'], _NN) + s['c_st']
    mean = _segsum_many(outs, c)
    d = [outs[i] - mean[i] * (1.0 / HEAD) for i in range(n)]
    var = _segsum_many([d[i] * d[i] for i in range(n)], c)
    res = [(d[i] * lax.rsqrt(var[i] * (1.0 / HEAD) + RWKV_GN_EPS) * ln_w + ln_b + st[i]['bonus']) * gs[i]
           for i in range(n)]
    return res, ht


def _rwkv_kernel(zr_ref, zk_ref, zv_ref, zl_ref, mur_ref, muk_ref, muv_ref, mul_ref,
                 w0_ref, a0_ref, w2_ref, a2_ref, g2_ref, kk_ref, ka_ref, rk_ref, lnw_ref, lnb_ref,
                 o_ref, ht_ref, carry_ref, carryl_ref, *, n_chunks):
    step = pl.program_id(2)

    @pl.when(step == 0)
    def _():
        ht_ref[...] = jnp.zeros_like(ht_ref)
        carry_ref[...] = jnp.zeros_like(carry_ref)
        carryl_ref[...] = jnp.zeros_like(carryl_ref)

    rows = zr_ref.shape[0]
    row = lax.broadcasted_iota(jnp.int32, (rows, 1), 0)

    def shifted(z, prev_row, mu):
        prev = jnp.where(row == 0, prev_row, pltpu.roll(z, 1, 0))
        return z + (prev - z) * mu

    zr, zk, zv, zl = zr_ref[...], zk_ref[...], zv_ref[...], zl_ref[...]
    r_all = shifted(zr, carry_ref[0, 0:1, :], mur_ref[...])
    k_all = shifted(zk, carry_ref[1, 0:1, :], muk_ref[...])
    v_all = shifted(zv, carry_ref[2, 0:1, :], muv_ref[...])
    l_all = shifted(zl, carryl_ref[0:1, :], mul_ref[...])
    carry_ref[0] = jnp.broadcast_to(zr[rows - 1:rows, :], (8, GROUP_W))
    carry_ref[1] = jnp.broadcast_to(zk[rows - 1:rows, :], (8, GROUP_W))
    carry_ref[2] = jnp.broadcast_to(zv[rows - 1:rows, :], (8, GROUP_W))
    carryl_ref[...] = jnp.broadcast_to(zl[rows - 1:rows, :], carryl_ref.shape)

    wpre_all = w0_ref[...] + _mm(jnp.tanh(l_all[:, 0:128]), w2_ref[...].astype(BF16), _NN)
    apre_all = a0_ref[...] + _mm(l_all[:, 128:256], a2_ref[...].astype(BF16), _NN)
    g_all = _mm(jax.nn.sigmoid(l_all[:, 256:512]), g2_ref[...].astype(BF16), _NN)

    c = _rwkv_consts()
    prm = (kk_ref[...], ka_ref[...], rk_ref[...], lnw_ref[...], lnb_ref[...])
    sls = [slice(ci * CHUNK, (ci + 1) * CHUNK) for ci in range(n_chunks)]
    pick = lambda x: [x[sl] for sl in sls]
    outs, ht = _rwkv_pipeline(pick(r_all), pick(k_all), pick(v_all), pick(wpre_all), pick(apre_all), pick(g_all),
                            prm, ht_ref[...], c)
    for sl, out in zip(sls, outs):
        o_ref[sl, :] = out.astype(o_ref.dtype)
    ht_ref[...] = ht


def _rwkv(z3, col_r, col_k, col_v, zl3, col_l, mu_rkv, mu_l, w0, a0, w2p, a2, g2, k_k, k_a, r_k, ln_w, ln_b,
          width, rows):
    bsz, seq, _ = z3.shape
    ng = width // GROUP_W
    gw = GROUP_W

    def zspec(col):
        return pl.BlockSpec((None, rows, gw), lambda b, g, s, o=col // gw: (b, s, o + g))

    def vspec(off=0):
        return pl.BlockSpec((1, gw), lambda b, g, s, o=off: (0, o + g))

    def mspec(k):
        return pl.BlockSpec((k, gw), lambda b, g, s: (0, g))

    in_specs = [zspec(col_r), zspec(col_k), zspec(col_v),
                pl.BlockSpec((None, rows, 512), lambda b, g, s, o=col_l // 512: (b, s, o)),
                vspec(0), vspec(ng), vspec(2 * ng),
                pl.BlockSpec((1, 512), lambda b, g, s: (0, 0)),
                vspec(), vspec(), mspec(128), mspec(128), mspec(256),
                vspec(), vspec(), vspec(), vspec(), vspec()]
    kern = functools.partial(_rwkv_kernel, n_chunks=rows // CHUNK)
    return pl.pallas_call(
        kern,
        out_shape=jax.ShapeDtypeStruct((bsz, seq, width), BF16),
        grid=(bsz, ng, seq // rows),
        in_specs=in_specs,
        out_specs=pl.BlockSpec((None, rows, gw), lambda b, g, s: (b, s, g)),
        scratch_shapes=[pltpu.VMEM((HEAD, gw), F32), pltpu.VMEM((3, 8, gw), F32), pltpu.VMEM((8, 512), F32)],
        compiler_params=_cparams(("parallel", "parallel", "arbitrary")),
        name="rwkv7_chunked",
    )(z3, z3, z3, zl3, mu_rkv, mu_rkv, mu_rkv, mu_l, w0, a0, w2p, a2, g2, k_k, k_a, r_k, ln_w, ln_b)


ATTN_COLS = 512
ATTN_AHEAD = 2
ATTN_UNROLL = 4
ATTN_SUM_ROWS = 16


def _diff_attn_kernel(lam_ref, q_ref, k_ref, v_ref, qg_ref, kg_ref, g_ref, o_ref, kn_ref, vt_ref, q2_ref, m_ref,
                      acc_ref, *, tq, out_scale):
    i = pl.program_id(2)
    hd = 2 * HEAD
    nq = vt_ref.shape[0]
    rr = lax.broadcasted_iota(jnp.int32, (hd, hd), 0) // HEAD
    cc = lax.broadcasted_iota(jnp.int32, (hd, hd), 1) // HEAD
    ones_bd = (rr == cc).astype(BF16)

    def qk_norm(x, g):
        ms = _dot2_exact_rhs(x * x, ones_bd, _NN) * (1.0 / HEAD)
        return x * lax.rsqrt(ms + NORM_EPS) * g

    @pl.when(i == 0)
    def _():
        for blk in range(nq):
            rows = slice(blk * tq, (blk + 1) * tq)
            kn_ref[rows, :] = qk_norm(k_ref[rows, :], kg_ref[...]).astype(BF16)
            vt_ref[blk, 0:hd, :] = jnp.transpose(v_ref[rows, :]).astype(BF16)
            vt_ref[blk, hd:hd + ATTN_SUM_ROWS, :] = jnp.ones((ATTN_SUM_ROWS, tq), BF16)

    scale = HEAD ** -0.5 * math.log2(math.e)
    q = (qk_norm(q_ref[...], qg_ref[...]) * scale).astype(BF16)
    cols = min(ATTN_COLS, 2 * tq)
    n_col = 2 * tq // cols
    qs = cols // 2
    lane = lax.broadcasted_iota(jnp.int32, (tq, hd), 1)
    zero = jnp.zeros_like(q)
    q_c1 = jnp.where(lane < HEAD, q, zero)
    q_c2 = jnp.where(lane >= HEAD, q, zero)
    for c in range(n_col):
        q2_ref[c * cols:c * cols + qs, :] = q_c1[c * qs:(c + 1) * qs, :]
        q2_ref[c * cols + qs:(c + 1) * cols, :] = q_c2[c * qs:(c + 1) * qs, :]
    m_ref[...] = jnp.full_like(m_ref, -jnp.inf)
    acc_ref[...] = jnp.zeros_like(acc_ref)

    def run(blocks):
        ks = [kn_ref[pl.ds(pl.multiple_of(j * tq, tq), tq), :] for j, _ in blocks]
        vts = [vt_ref[j] for j, _ in blocks]
        items = [(b, c) for b in range(len(blocks)) for c in range(n_col)]

        def n_keys(item):
            b, c = item
            return (c + 1) * qs if blocks[b][1] else tq

        def scores(item):
            b, c = item
            return _dot(ks[b][0:n_keys(item), :], q2_ref[c * cols:(c + 1) * cols, :], _NT)

        ahead = [scores(it) for it in items[:ATTN_AHEAD]]
        for n, (b, c) in enumerate(items):
            cs = slice(c * cols, (c + 1) * cols)
            s = ahead.pop(0)
            if n + ATTN_AHEAD < len(items):
                ahead.append(scores(items[n + ATTN_AHEAD]))
            if blocks[b][1]:
                kpos = lax.broadcasted_iota(jnp.int32, s.shape, 0)
                qpos = lax.broadcasted_iota(jnp.int32, s.shape, 1) % qs + c * qs
                s = jnp.where(kpos // CHUNK <= qpos // CHUNK, s, -jnp.inf)
            m_old = m_ref[:, cs]
            m_new = jnp.maximum(m_old, jnp.max(s, axis=0, keepdims=True))
            p = jnp.exp2((s - m_new).astype(BF16))
            alpha = jnp.exp2(m_old - m_new)
            acc_ref[:, cs] = alpha * acc_ref[:, cs] + _dot(vts[b][:, 0:n_keys((b, c))], p, _NN)
            m_ref[:, cs] = m_new

    def body(jj, carry):
        run([(ATTN_UNROLL * jj + t, False) for t in range(ATTN_UNROLL)])
        return carry

    lax.fori_loop(0, i // ATTN_UNROLL, body, 0)
    base = (i // ATTN_UNROLL) * ATTN_UNROLL
    for rem in range(ATTN_UNROLL):

        @pl.when(i % ATTN_UNROLL == rem)
        def _():
            run([(base + t, False) for t in range(rem)] + [(i, True)])

    acc = acc_ref[0:hd, :] / acc_ref[hd:hd + 1, :]
    o = jnp.concatenate([acc[:, c * cols:c * cols + qs] - lam_ref[0] * acc[:, c * cols + qs:(c + 1) * cols]
                         for c in range(n_col)], axis=1)
    o = jnp.transpose(o)
    ms = jnp.mean(o * o, axis=-1, keepdims=True)
    o_ref[...] = (o * lax.rsqrt(ms + NORM_EPS) * g_ref[...] * out_scale).astype(o_ref.dtype)


def _diff_attn(z3, col_q, width, q_g, k_g, lam, subln_g, out_scale, tq):
    bsz, seq, _ = z3.shape
    hd = 2 * HEAD
    nh = width // hd
    nq = seq // tq
    oq, ok, ov = (col_q // hd, (col_q + width) // hd, (col_q + 2 * width) // hd)
    gain = lambda g: jnp.tile(g, hd // HEAD).reshape(1, hd)
    vec = pl.BlockSpec((1, hd), lambda b, h, i: (0, 0))
    kern = functools.partial(_diff_attn_kernel, tq=tq, out_scale=out_scale)
    return pl.pallas_call(
        kern,
        out_shape=jax.ShapeDtypeStruct((bsz, seq, width), BF16),
        grid=(bsz, nh, nq),
        in_specs=[pl.BlockSpec(memory_space=pltpu.SMEM),
                  pl.BlockSpec((None, tq, hd), lambda b, h, i: (b, i, oq + h)),
                  pl.BlockSpec((None, seq, hd), lambda b, h, i: (b, 0, ok + h)),
                  pl.BlockSpec((None, seq, hd), lambda b, h, i: (b, 0, ov + h)),
                  vec, vec, vec],
        out_specs=pl.BlockSpec((None, tq, hd), lambda b, h, i: (b, i, h)),
        scratch_shapes=[pltpu.VMEM((seq, hd), BF16), pltpu.VMEM((nq, hd + ATTN_SUM_ROWS, tq), BF16),
                        pltpu.VMEM((2 * tq, hd), BF16), pltpu.VMEM((1, 2 * tq), F32),
                        pltpu.VMEM((hd + ATTN_SUM_ROWS, 2 * tq), F32)],
        compiler_params=_cparams(("parallel", "parallel", "arbitrary")),
        name="diff_flash_attention",
    )(lam.astype(F32).reshape(1), z3, z3, z3, gain(q_g), gain(k_g), subln_g.reshape(1, hd))


def _layer(x, p_i, lambda_init, norm_mix_g, w_in, rwkv_mu, rwkv_w0, rwkv_w2, rwkv_a0, rwkv_a2, rwkv_g2,
           rwkv_k_k, rwkv_k_a, rwkv_r_k, rwkv_ln_w, rwkv_ln_b, q_norm_g, k_norm_g,
           lam_q1, lam_k1, lam_q2, lam_k2, subln_g, w_branch_a, w_branch_b, w_out,
           norm_ffn_g, w_ffn_in, ffn_conv_w, ffn_conv_b, w_ffn_out, norm_ple_g, w_ple_gate, w_ple_proj):
    bsz, seq, d = x.shape
    t = bsz * seq
    rw = w_branch_a.shape[0]
    dw = w_branch_b.shape[0]
    n_w, n_a, n_g = rwkv_w2.shape[0], rwkv_a2.shape[0], rwkv_g2.shape[0]
    lora_pad = 128 - n_w
    assert n_w <= 128 and n_a == 128 and n_g == 256 and rw % GROUP_W == 0 and dw % GROUP_W == 0
    dff = w_ffn_out.shape[0]

    c_lo = 3 * rw
    c_diff = c_lo + n_w + n_a + n_g
    col_q = c_lo
    col_ga = col_q + 3 * dw
    col_gb = col_ga + d
    n_main = col_gb + d
    w_in_t = w_in.T
    mu_rkv = rwkv_mu[:c_lo].reshape(1, c_lo)
    mu_l = jnp.concatenate([rwkv_mu[c_lo:c_lo + n_w], jnp.zeros((lora_pad,), F32),
                            rwkv_mu[c_lo + n_w:]]).reshape(1, 512)
    w2p = jnp.concatenate([rwkv_w2, jnp.zeros((lora_pad, rw), F32)], axis=0)

    tm, tn, nr = TILES['tm'], TILES['tn'], TILES['norm_rows']
    x2 = x.reshape(t, d)
    h = _rmsnorm(x2, norm_mix_g, nr)
    skip = c_diff - c_lo
    assert skip % 8 == 0
    z = _matmul_wt(h, w_in_t, lambda j: pl.multiple_of(jnp.where(j < c_lo // tn, j * tn, j * tn + skip), 8),
                   n_main, F32, tm, tn, "in_proj")
    z_l = _matmul_wt(h, w_in_t, lambda j: c_lo, 512, F32, tm, 512, "in_proj_lora", pad_at=n_w, pad_rows=lora_pad)
    row = lambda v_: v_.reshape(1, -1)
    o_a = _rwkv(z.reshape(bsz, seq, n_main), 0, rw, 2 * rw, z_l.reshape(bsz, seq, 512), 0, mu_rkv, mu_l, row(rwkv_w0), row(rwkv_a0), w2p, rwkv_a2, rwkv_g2,
                row(rwkv_k_k), row(rwkv_k_a), row(rwkv_r_k), row(rwkv_ln_w), row(rwkv_ln_b), rw,
                TILES['rwkv_rows'])
    lam = (jnp.exp(jnp.sum(lam_q1 * lam_k1)) - jnp.exp(jnp.sum(lam_q2 * lam_k2)) + lambda_init)
    o_b = _diff_attn(z.reshape(bsz, seq, n_main), col_q, dw, q_norm_g, k_norm_g, lam, subln_g, 1.0 - lambda_init,
                     TILES['attn'])
    merged = _matmul([(o_a.reshape(t, rw), w_branch_a), (o_b.reshape(t, dw), w_branch_b)],
                     [(z, col_ga // tn), (z, col_gb // tn)], _ep_merge, d, BF16, tm, tn, "branch_merge")
    x2 = _matmul([(merged, w_out)], [(x2, 0)], _ep_residual, d, F32, tm, tn, "out_proj")
    h = _rmsnorm(x2, norm_ffn_g, nr)
    act = _ffn_in(h, w_ffn_in, ffn_conv_w, ffn_conv_b.reshape(1, 2 * dff), dff, seq, tm, TILES['ffn_in_tn'])
    x2 = _matmul([(act, w_ffn_out.astype(BF16))], [(x2, 0)], _ep_residual, d, F32, TILES['ffn_out_tm'],
                 TILES['ffn_out_tn'], "ffn_out")
    h = _rmsnorm(x2, norm_ple_g, nr)
    pe = p_i.reshape(t, -1).astype(BF16)
    x2 = _matmul([(h, w_ple_gate), (pe, w_ple_proj)], [(x2, 0)], _ep_ple, d, F32, tm, tn, "ple_gate")
    return x2.reshape(bsz, seq, d)


def kernel(x, p, norm_mix_g, w_in, rwkv_mu, rwkv_w0, rwkv_w2, rwkv_a0, rwkv_a2, rwkv_g2, rwkv_k_k, rwkv_k_a,
           rwkv_r_k, rwkv_ln_w, rwkv_ln_b, q_norm_g, k_norm_g, lam_q1, lam_k1, lam_q2, lam_k2, subln_g,
           w_branch_a, w_branch_b, w_out, norm_ffn_g, w_ffn_in, ffn_conv_w, ffn_conv_b, w_ffn_out,
           norm_ple_g, w_ple_gate, w_ple_proj):
    depth = p.shape[0]
    for i in range(depth):
        lambda_init = 0.8 - 0.6 * math.exp(-0.3 * i)
        x = _layer(x, p[i], lambda_init, norm_mix_g[i], w_in[i], rwkv_mu[i], rwkv_w0[i], rwkv_w2[i], rwkv_a0[i],
                   rwkv_a2[i], rwkv_g2[i], rwkv_k_k[i], rwkv_k_a[i], rwkv_r_k[i], rwkv_ln_w[i], rwkv_ln_b[i],
                   q_norm_g[i], k_norm_g[i], lam_q1[i], lam_k1[i], lam_q2[i], lam_k2[i], subln_g[i],
                   w_branch_a[i], w_branch_b[i], w_out[i], norm_ffn_g[i], w_ffn_in[i], ffn_conv_w[i],
                   ffn_conv_b[i], w_ffn_out[i], norm_ple_g[i], w_ple_gate[i], w_ple_proj[i])
    return x
```

```python
import functools
import math

import jax
import jax.numpy as jnp
import numpy as np
from jax import lax
from jax.experimental import pallas as pl
from jax.experimental.pallas import tpu as pltpu

F32 = jnp.float32
BF16 = jnp.bfloat16

NORM_EPS = 1e-6
RWKV_GN_EPS = 64e-5
HEAD = 64
CHUNK = 64
GROUP_W = 256
LANES = 128
VMEM_LIMIT = 56 * 1024 * 1024
TILES = dict(tm=1024, tn=512, proj_tm=1024, proj_tn=512, rwkv_rows=2048, attn=512, norm_rows=256, ffn_in_tn=256,
             ffn_out_tm=512, ffn_out_tn=512)


def _cparams(sem):
    return pltpu.CompilerParams(dimension_semantics=sem, vmem_limit_bytes=VMEM_LIMIT)


def _dot(a, b, dims):
    return lax.dot_general(a, b, (dims, ((), ())), preferred_element_type=F32)


_NN = ((1,), (0,))
_NT = ((1,), (1,))
_TN = ((0,), (0,))


def _split(x):
    hi = x.astype(BF16)
    lo = (x - hi.astype(F32)).astype(BF16)
    return hi, lo


def _dot2_exact_rhs(a, b_bf16, dims):
    ah, al = _split(a)
    return _dot(ah, b_bf16, dims) + _dot(al, b_bf16, dims)


def _rmsnorm_kernel(x_ref, g_ref, o_ref):
    x = x_ref[...]
    ms = jnp.mean(x * x, axis=-1, keepdims=True)
    o_ref[...] = (x * lax.rsqrt(ms + NORM_EPS) * g_ref[...]).astype(o_ref.dtype)


def _rmsnorm(x2d, g, tr=256):
    t, d = x2d.shape
    return pl.pallas_call(
        _rmsnorm_kernel,
        out_shape=jax.ShapeDtypeStruct((t, d), BF16),
        grid=(t // tr,),
        in_specs=[pl.BlockSpec((tr, d), lambda i: (i, 0)), pl.BlockSpec((1, d), lambda i: (0, 0))],
        out_specs=pl.BlockSpec((tr, d), lambda i: (i, 0)),
        compiler_params=_cparams(("parallel",)),
        name="rmsnorm",
    )(x2d, g.reshape(1, d))


def _mm_kernel(*refs, n_dots, n_extra, epilogue):
    out_ref = refs[2 * n_dots + n_extra]
    res = []
    for d in range(n_dots):
        res.append(jnp.dot(refs[2 * d][...], refs[2 * d + 1][...], preferred_element_type=F32))
    extras = [refs[2 * n_dots + e][...] for e in range(n_extra)]
    out_ref[...] = epilogue(res, extras).astype(out_ref.dtype)


def _mm_ws_kernel(*refs, n_dots, n_extra, epilogue):
    out_ref = refs[2 * n_dots + n_extra]
    w_bf = refs[2 * n_dots + n_extra + 1:]

    @pl.when(pl.program_id(1) == 0)
    def _():
        for d in range(n_dots):
            w_bf[d][...] = refs[2 * d + 1][...].astype(BF16)

    res = [jnp.dot(refs[2 * d][...], w_bf[d][...], preferred_element_type=F32) for d in range(n_dots)]
    extras = [refs[2 * n_dots + e][...] for e in range(n_extra)]
    out_ref[...] = epilogue(res, extras).astype(out_ref.dtype)


def _matmul(dots, extras, epilogue, n_out, out_dtype, tm, tn, name):
    m = dots[0][0].shape[0]
    stationary = dots[0][1].dtype == F32
    if stationary:
        ij = lambda f: (lambda j, i: f(i, j))
        grid = (n_out // tn, m // tm)
        kern_fn = _mm_ws_kernel
        scratch = [pltpu.VMEM((a.shape[1], tn), BF16) for a, _ in dots]
        sem = ("arbitrary", "arbitrary")
    else:
        ij = lambda f: f
        grid = (m // tm, n_out // tn)
        kern_fn = _mm_kernel
        scratch = []
        sem = ("parallel", "arbitrary")
    in_specs, args = [], []
    for a, w in dots:
        k = a.shape[1]
        in_specs.append(pl.BlockSpec((tm, k), ij(lambda i, j: (i, 0))))
        in_specs.append(pl.BlockSpec((k, tn), ij(lambda i, j: (0, j))))
        args += [a, w]
    for arr, off in extras:
        in_specs.append(pl.BlockSpec((tm, tn), ij(lambda i, j, off=off: (i, j + off))))
        args.append(arr)
    kern = functools.partial(kern_fn, n_dots=len(dots), n_extra=len(extras), epilogue=epilogue)
    return pl.pallas_call(
        kern,
        out_shape=jax.ShapeDtypeStruct((m, n_out), out_dtype),
        grid=grid,
        in_specs=in_specs,
        out_specs=pl.BlockSpec((tm, tn), ij(lambda i, j: (i, j))),
        scratch_shapes=scratch,
        compiler_params=_cparams(sem),
        name=name,
    )(*args)


def _mm_wt_kernel(a_ref, wt_ref, o_ref, w_bf, *, pad_at, pad_rows):
    @pl.when(pl.program_id(1) == 0)
    def _():
        wt = wt_ref[...]
        if pad_rows:
            tn = wt.shape[0]
            wt = jnp.concatenate([wt[:pad_at], jnp.zeros((pad_rows, wt.shape[1]), F32),
                                  wt[pad_at:tn - pad_rows]], axis=0)
        w_bf[...] = jnp.transpose(wt).astype(BF16)

    o_ref[...] = jnp.dot(a_ref[...], w_bf[...], preferred_element_type=F32).astype(o_ref.dtype)


def _matmul_wt(a, wt, row_start, n_out, out_dtype, tm, tn, name, pad_at=0, pad_rows=0):
    m, k = a.shape
    return pl.pallas_call(
        functools.partial(_mm_wt_kernel, pad_at=pad_at, pad_rows=pad_rows),
        out_shape=jax.ShapeDtypeStruct((m, n_out), out_dtype),
        grid=(n_out // tn, m // tm),
        in_specs=[pl.BlockSpec((tm, k), lambda j, i: (i, 0)),
                  pl.BlockSpec((pl.Element(tn), pl.Element(k)), lambda j, i: (row_start(j), 0))],
        out_specs=pl.BlockSpec((tm, tn), lambda j, i: (i, j)),
        scratch_shapes=[pltpu.VMEM((k, tn), BF16)],
        compiler_params=_cparams(("arbitrary", "arbitrary")),
        name=name,
    )(a, wt)


def _ep_identity(res, extras):
    return res[0]


def _ep_merge(res, extras):
    return jax.nn.sigmoid(extras[0]) * res[0] + jax.nn.sigmoid(extras[1]) * res[1]


def _ep_residual(res, extras):
    return extras[0] + res[0]


def _ep_ple(res, extras):
    return extras[0] + jax.nn.sigmoid(res[0]) * res[1]


def _ffn_in_kernel(a_ref, wg_ref, wu_ref, cwg_ref, cwu_ref, cbg_ref, cbu_ref, o_ref, carry_ref, w_ref,
                   *, tiles_per_seq):
    i = pl.program_id(1)
    tn = wg_ref.shape[1]

    @pl.when(i == 0)
    def _():
        w_ref[:, 0:tn] = wg_ref[...].astype(BF16)
        w_ref[:, tn:2 * tn] = wu_ref[...].astype(BF16)

    @pl.when(i % tiles_per_seq == 0)
    def _():
        carry_ref[...] = jnp.zeros_like(carry_ref)

    a = a_ref[...]
    tm = a.shape[0]
    row = lax.broadcasted_iota(jnp.int32, (tm, 1), 0)

    def conv(u, carry, cw, cb):
        c1 = carry[7:8, :]
        c2 = carry[6:7, :]
        p1 = jnp.where(row == 0, c1, pltpu.roll(u, 1, 0))
        p2 = jnp.where(row == 0, c2, jnp.where(row == 1, c1, pltpu.roll(u, 2, 0)))
        return cb + p2 * cw[0:1, :] + p1 * cw[1:2, :] + u * cw[2:3, :]

    u = jnp.dot(a, w_ref[...], preferred_element_type=F32)
    ug = u[:, 0:tn]
    uu = u[:, tn:2 * tn]
    gate = conv(ug, carry_ref[0], cwg_ref[...], cbg_ref[...])
    up = conv(uu, carry_ref[1], cwu_ref[...], cbu_ref[...])
    carry_ref[0] = ug[tm - 8:tm, :]
    carry_ref[1] = uu[tm - 8:tm, :]
    o_ref[...] = (gate * jax.nn.sigmoid(gate) * up).astype(o_ref.dtype)


def _ffn_in(h, w, conv_w, conv_b, dff, seq, tm, tn):
    t, k = h.shape
    nj = dff // tn
    kern = functools.partial(_ffn_in_kernel, tiles_per_seq=seq // tm)
    return pl.pallas_call(
        kern,
        out_shape=jax.ShapeDtypeStruct((t, dff), BF16),
        grid=(nj, t // tm),
        in_specs=[pl.BlockSpec((tm, k), lambda j, i: (i, 0)),
                  pl.BlockSpec((k, tn), lambda j, i: (0, j)),
                  pl.BlockSpec((k, tn), lambda j, i: (0, j + nj)),
                  pl.BlockSpec((3, tn), lambda j, i: (0, j)),
                  pl.BlockSpec((3, tn), lambda j, i: (0, j + nj)),
                  pl.BlockSpec((1, tn), lambda j, i: (0, j)),
                  pl.BlockSpec((1, tn), lambda j, i: (0, j + nj))],
        out_specs=pl.BlockSpec((tm, tn), lambda j, i: (i, j)),
        scratch_shapes=[pltpu.VMEM((2, 8, tn), F32), pltpu.VMEM((k, 2 * tn), BF16)],
        compiler_params=_cparams(("arbitrary", "arbitrary")),
        name="ffn_in_conv_gate",
    )(h, w, w, conv_w, conv_w, conv_b, conv_b)


def _rwkv_consts():
    r = lax.broadcasted_iota(jnp.int32, (GROUP_W, GROUP_W), 0) // HEAD
    c = lax.broadcasted_iota(jnp.int32, (GROUP_W, GROUP_W), 1) // HEAD
    bd = r == c
    t = lax.broadcasted_iota(jnp.int32, (CHUNK, GROUP_W), 0)
    s = lax.broadcasted_iota(jnp.int32, (CHUNK, GROUP_W), 1) % HEAD
    tt = lax.broadcasted_iota(jnp.int32, (CHUNK, CHUNK), 0)
    ss = lax.broadcasted_iota(jnp.int32, (CHUNK, CHUNK), 1)
    lane_head = lax.broadcasted_iota(jnp.int32, (CHUNK, LANES), 1) // HEAD
    head_lanes = [lane_head == hh for hh in range(LANES // HEAD)]
    col_head = lax.broadcasted_iota(jnp.int32, (HEAD, GROUP_W), 1) // HEAD
    head_cols = [col_head == hh for hh in range(GROUP_W // HEAD)]
    return dict(bd=bd, ones_bd=bd.astype(BF16), strict=s < t, incl=s <= t, head_lanes=head_lanes,
                head_cols=head_cols,
                eye=(s == t).astype(F32), ltri=(ss <= tt).astype(BF16))


def _block_diag(y, c):
    yb = y.astype(BF16)
    zero = jnp.zeros((CHUNK, LANES), BF16)
    heads_per_tile = LANES // HEAD
    rows = []
    for h in range(GROUP_W // HEAD):
        tile = h // heads_per_tile
        piece = jnp.where(c['head_lanes'][h % heads_per_tile], yb[:, tile * LANES:(tile + 1) * LANES], zero)
        rows.append(jnp.concatenate([piece if t == tile else zero for t in range(GROUP_W // LANES)], axis=1))
    return jnp.concatenate(rows, axis=0)


def _fold_heads(x, c):
    out = None
    for h in range(GROUP_W // HEAD):
        part = jnp.where(c['head_cols'][h], x[h * HEAD:(h + 1) * HEAD, :], 0.0)
        out = part if out is None else out + part
    return out


def _mm(a, b_bf16, dims):
    return _dot(a.astype(BF16), b_bf16, dims)


def _segsum(x, c):
    return _mm(x, c['ones_bd'], _NN)


def _segsum_many(xs, c):
    tot = _segsum(jnp.concatenate(xs, axis=0), c)
    return [tot[i * CHUNK:(i + 1) * CHUNK, :] for i in range(len(xs))]


def _cumsum_rows(l_bf16, x):
    xh, xl = _split(x)
    return _dot(l_bf16, xh, _NN) + _dot(l_bf16, xl, _NN)


RWKV_GROUP = 8
RWKV_LAG = 9


def _rwkv_pipeline(rs, kraws, vs, wpres, apres, gs, prm, ht, c):
    k_k, k_a, r_k, ln_w, ln_b = prm
    n = len(rs)
    st = [dict() for _ in range(n)]

    def stages(idx):
        sub = [st[i] for i in idx]
        for i, s in zip(idx, sub):
            s['ld'] = -np.float32(math.exp(-0.5)) * jax.nn.sigmoid(wpres[i])
            s['kk0'] = kraws[i] * k_k
        for s, ssq in zip(sub, _segsum_many([s['kk0'] * s['kk0'] for s in sub], c)):
            s['ssq'] = ssq
        for s in sub:
            s['cum'] = _cumsum_rows(c['ltri'], s['ld'])
        yield
        for i, s in zip(idx, sub):
            av = jax.nn.sigmoid(apres[i])
            kk = s['kk0'] * lax.rsqrt(jnp.maximum(s['ssq'], 1e-24))
            k = kraws[i] * (1.0 + (av - 1.0) * k_a)
            last = s['cum'][CHUNK - 1:CHUNK, :]
            dec_out = jnp.exp(-s['cum'])
            s['dec_all'] = jnp.exp(last)
            s['r_t'] = rs[i] * jnp.exp(s['cum'])
            s['a_t'] = -kk * jnp.exp(s['cum'] - s['ld'])
            s['b_t'] = kk * av * dec_out
            s['k_t'] = k * dec_out
            s['ar'] = jnp.concatenate([s['a_t'], s['r_t']], axis=0).astype(BF16)
            s['bonus_in'] = rs[i] * k * r_k
        yield
        for s in sub:
            s['sb'] = _dot(s['ar'], _block_diag(s['b_t'], c), _NT)
        for s in sub:
            s['sk'] = _dot(s['ar'], _block_diag(s['k_t'], c), _NT)
        for s in sub:
            s['pw'] = jnp.where(c['strict'], s['sb'][:CHUNK], 0.0)
            s['s_rb'] = jnp.where(c['incl'], s['sb'][CHUNK:], 0.0).astype(BF16)
            s['s_ak'] = jnp.where(c['strict'], s['sk'][:CHUNK], 0.0).astype(BF16)
            s['s_rk'] = jnp.where(c['incl'], s['sk'][CHUNK:], 0.0).astype(BF16)
            s['inv'] = c['eye'] + s['pw']
        yield
        for s in sub:
            s['pw'] = _mm(s['pw'], _block_diag(s['pw'], c), _NN)
        yield
        for _ in range(4):
            for s in sub:
                both = _mm(jnp.concatenate([s['pw'], s['inv']], axis=0), _block_diag(s['pw'], c), _NN)
                s['pw'] = both[:CHUNK]
                s['inv'] = s['inv'] + both[CHUNK:]
            yield
        for s in sub:
            s['inv'] = (s['inv'] + _mm(s['inv'], _block_diag(s['pw'], c), _NN)).astype(BF16)
        for i, s in zip(idx, sub):
            s['v_bd'] = _block_diag(vs[i], c)
        for s in sub:
            s['w'] = _dot(s['inv'], _block_diag(s['a_t'], c), _NN)
        for s in sub:
            both = _dot(jnp.concatenate([s['s_ak'], s['s_rk']], axis=0), s['v_bd'], _NN)
            s['y'] = both[:CHUNK]
            s['ov_v'] = both[CHUNK:]
        yield
        for s in sub:
            s['uv'] = _dot(s['inv'], _block_diag(s['y'], c), _NN)
        for s in sub:
            s['rw'] = (s['r_t'] + _dot(s['s_rb'], _block_diag(s['w'], c), _NN)).astype(BF16)
        yield
        for s in sub:
            s['ov'] = _dot(s['s_rb'], _block_diag(s['uv'], c), _NN) + s['ov_v']
        for s in sub:
            s['bk'] = (jnp.concatenate([s['b_t'], s['k_t']], axis=0) * s['dec_all']).astype(BF16)
        for s in sub:
            s['m_st'] = jnp.where(c['bd'], _mm(s['w'], s['bk'][:CHUNK], _TN), 0.0).astype(BF16)
        for i, s in zip(idx, sub):
            s['c_st'] = _fold_heads(_mm(jnp.concatenate([s['uv'], vs[i]], axis=0), s['bk'], _TN), c)
        for i, s, tot in zip(idx, sub, _segsum_many([s['bonus_in'] for s in sub], c)):
            s['bonus'] = tot * vs[i]
        yield

    state = [ht]
    res = [None] * n

    def chain(idx):
        outs = []
        for i in idx:
            s = st[i]
            outs.append(_dot(s['rw'], _block_diag(state[0], c), _NT) + s['ov'])
            state[0] = state[0] * s['dec_all'] + _mm(state[0], s['m_st'], _NN) + s['c_st']
            yield
        mean = _segsum_many(outs, c)
        d = [o - m * (1.0 / HEAD) for o, m in zip(outs, mean)]
        var = _segsum_many([x * x for x in d], c)
        for i, x, v in zip(idx, d, var):
            res[i] = (x * lax.rsqrt(v * (1.0 / HEAD) + RWKV_GN_EPS) * ln_w + ln_b + st[i]['bonus']) * gs[i]
        yield

    groups = [list(range(n))[j:j + RWKV_GROUP] for j in range(0, n, RWKV_GROUP)]
    tasks = [[stages(g), chain(g)] for g in groups]
    tick = 0
    while any(tasks):
        for gi, pending in enumerate(tasks):
            if not pending or tick < gi * RWKV_LAG:
                continue
            if len(pending) == 1 and gi > 0 and tasks[gi - 1]:
                continue
            try:
                next(pending[0])
            except StopIteration:
                pending.pop(0)
        tick += 1
    return res, state[0]


def _rwkv_kernel(zr_ref, zk_ref, zv_ref, zl_ref, mur_ref, muk_ref, muv_ref, mul_ref,
                 w0_ref, a0_ref, w2_ref, a2_ref, g2_ref, kk_ref, ka_ref, rk_ref, lnw_ref, lnb_ref,
                 o_ref, ht_ref, carry_ref, carryl_ref, *, n_chunks):
    step = pl.program_id(2)

    @pl.when(step == 0)
    def _():
        ht_ref[...] = jnp.zeros_like(ht_ref)
        carry_ref[...] = jnp.zeros_like(carry_ref)
        carryl_ref[...] = jnp.zeros_like(carryl_ref)

    rows = zr_ref.shape[0]
    row = lax.broadcasted_iota(jnp.int32, (rows, 1), 0)

    def shifted(z, prev_row, mu):
        prev = jnp.where(row == 0, prev_row, pltpu.roll(z, 1, 0))
        return z + (prev - z) * mu

    zr, zk, zv, zl = zr_ref[...], zk_ref[...], zv_ref[...], zl_ref[...]
    r_all = shifted(zr, carry_ref[0, 0:1, :], mur_ref[...])
    k_all = shifted(zk, carry_ref[1, 0:1, :], muk_ref[...])
    v_all = shifted(zv, carry_ref[2, 0:1, :], muv_ref[...])
    l_all = shifted(zl, carryl_ref[0:1, :], mul_ref[...])
    carry_ref[0] = jnp.broadcast_to(zr[rows - 1:rows, :], (8, GROUP_W))
    carry_ref[1] = jnp.broadcast_to(zk[rows - 1:rows, :], (8, GROUP_W))
    carry_ref[2] = jnp.broadcast_to(zv[rows - 1:rows, :], (8, GROUP_W))
    carryl_ref[...] = jnp.broadcast_to(zl[rows - 1:rows, :], carryl_ref.shape)

    wpre_all = w0_ref[...] + _mm(jnp.tanh(l_all[:, 0:128]), w2_ref[...].astype(BF16), _NN)
    apre_all = a0_ref[...] + _mm(l_all[:, 128:256], a2_ref[...].astype(BF16), _NN)
    g_all = _mm(jax.nn.sigmoid(l_all[:, 256:512]), g2_ref[...].astype(BF16), _NN)

    c = _rwkv_consts()
    prm = (kk_ref[...], ka_ref[...], rk_ref[...], lnw_ref[...], lnb_ref[...])
    sls = [slice(ci * CHUNK, (ci + 1) * CHUNK) for ci in range(n_chunks)]
    pick = lambda x: [x[sl] for sl in sls]
    outs, ht = _rwkv_pipeline(pick(r_all), pick(k_all), pick(v_all), pick(wpre_all), pick(apre_all), pick(g_all),
                            prm, ht_ref[...], c)
    for sl, out in zip(sls, outs):
        o_ref[sl, :] = out.astype(o_ref.dtype)
    ht_ref[...] = ht


def _rwkv(z3, col_r, col_k, col_v, zl3, col_l, mu_rkv, mu_l, w0, a0, w2p, a2, g2, k_k, k_a, r_k, ln_w, ln_b,
          width, rows):
    bsz, seq, _ = z3.shape
    ng = width // GROUP_W
    gw = GROUP_W

    def zspec(col):
        return pl.BlockSpec((None, rows, gw), lambda b, g, s, o=col // gw: (b, s, o + g))

    def vspec(off=0):
        return pl.BlockSpec((1, gw), lambda b, g, s, o=off: (0, o + g))

    def mspec(k):
        return pl.BlockSpec((k, gw), lambda b, g, s: (0, g))

    in_specs = [zspec(col_r), zspec(col_k), zspec(col_v),
                pl.BlockSpec((None, rows, 512), lambda b, g, s, o=col_l // 512: (b, s, o)),
                vspec(0), vspec(ng), vspec(2 * ng),
                pl.BlockSpec((1, 512), lambda b, g, s: (0, 0)),
                vspec(), vspec(), mspec(128), mspec(128), mspec(256),
                vspec(), vspec(), vspec(), vspec(), vspec()]
    kern = functools.partial(_rwkv_kernel, n_chunks=rows // CHUNK)
    return pl.pallas_call(
        kern,
        out_shape=jax.ShapeDtypeStruct((bsz, seq, width), BF16),
        grid=(bsz, ng, seq // rows),
        in_specs=in_specs,
        out_specs=pl.BlockSpec((None, rows, gw), lambda b, g, s: (b, s, g)),
        scratch_shapes=[pltpu.VMEM((HEAD, gw), F32), pltpu.VMEM((3, 8, gw), F32), pltpu.VMEM((8, 512), F32)],
        compiler_params=_cparams(("parallel", "parallel", "arbitrary")),
        name="rwkv7_chunked",
    )(z3, z3, z3, zl3, mu_rkv, mu_rkv, mu_rkv, mu_l, w0, a0, w2p, a2, g2, k_k, k_a, r_k, ln_w, ln_b)


ATTN_COLS = 512
ATTN_AHEAD = 2
ATTN_UNROLL = 4
ATTN_SUM_ROWS = 16


def _diff_attn_kernel(lam_ref, q_ref, k_ref, v_ref, qg_ref, kg_ref, g_ref, o_ref, kn_ref, vt_ref, q2_ref, m_ref,
                      acc_ref, *, tq, out_scale):
    i = pl.program_id(2)
    hd = 2 * HEAD
    nq = vt_ref.shape[0]
    rr = lax.broadcasted_iota(jnp.int32, (hd, hd), 0) // HEAD
    cc = lax.broadcasted_iota(jnp.int32, (hd, hd), 1) // HEAD
    ones_bd = (rr == cc).astype(BF16)

    def qk_norm(x, g):
        ms = _dot2_exact_rhs(x * x, ones_bd, _NN) * (1.0 / HEAD)
        return x * lax.rsqrt(ms + NORM_EPS) * g

    @pl.when(i == 0)
    def _():
        for blk in range(nq):
            rows = slice(blk * tq, (blk + 1) * tq)
            kn_ref[rows, :] = qk_norm(k_ref[rows, :], kg_ref[...]).astype(BF16)
            vt_ref[blk, 0:hd, :] = jnp.transpose(v_ref[rows, :]).astype(BF16)
            vt_ref[blk, hd:hd + ATTN_SUM_ROWS, :] = jnp.ones((ATTN_SUM_ROWS, tq), BF16)

    scale = HEAD ** -0.5 * math.log2(math.e)
    q = (qk_norm(q_ref[...], qg_ref[...]) * scale).astype(BF16)
    cols = min(ATTN_COLS, 2 * tq)
    n_col = 2 * tq // cols
    qs = cols // 2
    lane = lax.broadcasted_iota(jnp.int32, (tq, hd), 1)
    zero = jnp.zeros_like(q)
    q_c1 = jnp.where(lane < HEAD, q, zero)
    q_c2 = jnp.where(lane >= HEAD, q, zero)
    for c in range(n_col):
        q2_ref[c * cols:c * cols + qs, :] = q_c1[c * qs:(c + 1) * qs, :]
        q2_ref[c * cols + qs:(c + 1) * cols, :] = q_c2[c * qs:(c + 1) * qs, :]
    m_ref[...] = jnp.full_like(m_ref, -jnp.inf)
    acc_ref[...] = jnp.zeros_like(acc_ref)

    def run(blocks):
        ks = [kn_ref[pl.ds(pl.multiple_of(j * tq, tq), tq), :] for j, _ in blocks]
        vts = [vt_ref[j] for j, _ in blocks]
        items = [(b, c) for b in range(len(blocks)) for c in range(n_col)]

        def n_keys(item):
            b, c = item
            return (c + 1) * qs if blocks[b][1] else tq

        def scores(item):
            b, c = item
            return _dot(ks[b][0:n_keys(item), :], q2_ref[c * cols:(c + 1) * cols, :], _NT)

        ahead = [scores(it) for it in items[:ATTN_AHEAD]]
        for n, (b, c) in enumerate(items):
            cs = slice(c * cols, (c + 1) * cols)
            s = ahead.pop(0)
            if n + ATTN_AHEAD < len(items):
                ahead.append(scores(items[n + ATTN_AHEAD]))
            if blocks[b][1]:
                kpos = lax.broadcasted_iota(jnp.int32, s.shape, 0)
                qpos = lax.broadcasted_iota(jnp.int32, s.shape, 1) % qs + c * qs
                s = jnp.where(kpos // CHUNK <= qpos // CHUNK, s, -jnp.inf)
            m_old = m_ref[:, cs]
            m_new = jnp.maximum(m_old, jnp.max(s, axis=0, keepdims=True))
            p = jnp.exp2((s - m_new).astype(BF16))
            alpha = jnp.exp2(m_old - m_new)
            acc_ref[:, cs] = alpha * acc_ref[:, cs] + _dot(vts[b][:, 0:n_keys((b, c))], p, _NN)
            m_ref[:, cs] = m_new

    def body(jj, carry):
        run([(ATTN_UNROLL * jj + t, False) for t in range(ATTN_UNROLL)])
        return carry

    lax.fori_loop(0, i // ATTN_UNROLL, body, 0)
    base = (i // ATTN_UNROLL) * ATTN_UNROLL
    for rem in range(ATTN_UNROLL):

        @pl.when(i % ATTN_UNROLL == rem)
        def _():
            run([(base + t, False) for t in range(rem)] + [(i, True)])

    acc = acc_ref[0:hd, :] / acc_ref[hd:hd + 1, :]
    o = jnp.concatenate([acc[:, c * cols:c * cols + qs] - lam_ref[0] * acc[:, c * cols + qs:(c + 1) * cols]
                         for c in range(n_col)], axis=1)
    o = jnp.transpose(o)
    ms = jnp.mean(o * o, axis=-1, keepdims=True)
    o_ref[...] = (o * lax.rsqrt(ms + NORM_EPS) * g_ref[...] * out_scale).astype(o_ref.dtype)


def _diff_attn(z3, col_q, width, q_g, k_g, lam, subln_g, out_scale, tq):
    bsz, seq, _ = z3.shape
    hd = 2 * HEAD
    nh = width // hd
    nq = seq // tq
    oq, ok, ov = (col_q // hd, (col_q + width) // hd, (col_q + 2 * width) // hd)
    gain = lambda g: jnp.tile(g, hd // HEAD).reshape(1, hd)
    vec = pl.BlockSpec((1, hd), lambda b, h, i: (0, 0))
    kern = functools.partial(_diff_attn_kernel, tq=tq, out_scale=out_scale)
    return pl.pallas_call(
        kern,
        out_shape=jax.ShapeDtypeStruct((bsz, seq, width), BF16),
        grid=(bsz, nh, nq),
        in_specs=[pl.BlockSpec(memory_space=pltpu.SMEM),
                  pl.BlockSpec((None, tq, hd), lambda b, h, i: (b, i, oq + h)),
                  pl.BlockSpec((None, seq, hd), lambda b, h, i: (b, 0, ok + h)),
                  pl.BlockSpec((None, seq, hd), lambda b, h, i: (b, 0, ov + h)),
                  vec, vec, vec],
        out_specs=pl.BlockSpec((None, tq, hd), lambda b, h, i: (b, i, h)),
        scratch_shapes=[pltpu.VMEM((seq, hd), BF16), pltpu.VMEM((nq, hd + ATTN_SUM_ROWS, tq), BF16),
                        pltpu.VMEM((2 * tq, hd), BF16), pltpu.VMEM((1, 2 * tq), F32),
                        pltpu.VMEM((hd + ATTN_SUM_ROWS, 2 * tq), F32)],
        compiler_params=_cparams(("parallel", "parallel", "arbitrary")),
        name="diff_flash_attention",
    )(lam.astype(F32).reshape(1), z3, z3, z3, gain(q_g), gain(k_g), subln_g.reshape(1, hd))


def _layer(x, p_i, lambda_init, norm_mix_g, w_in, rwkv_mu, rwkv_w0, rwkv_w2, rwkv_a0, rwkv_a2, rwkv_g2,
           rwkv_k_k, rwkv_k_a, rwkv_r_k, rwkv_ln_w, rwkv_ln_b, q_norm_g, k_norm_g,
           lam_q1, lam_k1, lam_q2, lam_k2, subln_g, w_branch_a, w_branch_b, w_out,
           norm_ffn_g, w_ffn_in, ffn_conv_w, ffn_conv_b, w_ffn_out, norm_ple_g, w_ple_gate, w_ple_proj):
    bsz, seq, d = x.shape
    t = bsz * seq
    rw = w_branch_a.shape[0]
    dw = w_branch_b.shape[0]
    n_w, n_a, n_g = rwkv_w2.shape[0], rwkv_a2.shape[0], rwkv_g2.shape[0]
    lora_pad = 128 - n_w
    assert n_w <= 128 and n_a == 128 and n_g == 256 and rw % GROUP_W == 0 and dw % GROUP_W == 0
    dff = w_ffn_out.shape[0]

    c_lo = 3 * rw
    c_diff = c_lo + n_w + n_a + n_g
    col_q = c_lo
    col_ga = col_q + 3 * dw
    col_gb = col_ga + d
    n_main = col_gb + d
    w_in_t = w_in.T
    mu_rkv = rwkv_mu[:c_lo].reshape(1, c_lo)
    mu_l = jnp.concatenate([rwkv_mu[c_lo:c_lo + n_w], jnp.zeros((lora_pad,), F32),
                            rwkv_mu[c_lo + n_w:]]).reshape(1, 512)
    w2p = jnp.concatenate([rwkv_w2, jnp.zeros((lora_pad, rw), F32)], axis=0)

    tm, tn, nr = TILES['tm'], TILES['tn'], TILES['norm_rows']
    x2 = x.reshape(t, d)
    h = _rmsnorm(x2, norm_mix_g, nr)
    skip = c_diff - c_lo
    assert skip % 8 == 0
    ptm, ptn = TILES['proj_tm'], TILES['proj_tn']
    z = _matmul_wt(h, w_in_t, lambda j: pl.multiple_of(jnp.where(j < c_lo // ptn, j * ptn, j * ptn + skip), 8),
                   n_main, F32, ptm, ptn, "in_proj")
    z_l = _matmul_wt(h, w_in_t, lambda j: c_lo, 512, F32, tm, 512, "in_proj_lora", pad_at=n_w, pad_rows=lora_pad)
    row = lambda v_: v_.reshape(1, -1)
    o_a = _rwkv(z.reshape(bsz, seq, n_main), 0, rw, 2 * rw, z_l.reshape(bsz, seq, 512), 0, mu_rkv, mu_l, row(rwkv_w0), row(rwkv_a0), w2p, rwkv_a2, rwkv_g2,
                row(rwkv_k_k), row(rwkv_k_a), row(rwkv_r_k), row(rwkv_ln_w), row(rwkv_ln_b), rw,
                TILES['rwkv_rows'])
    lam = (jnp.exp(jnp.sum(lam_q1 * lam_k1)) - jnp.exp(jnp.sum(lam_q2 * lam_k2)) + lambda_init)
    o_b = _diff_attn(z.reshape(bsz, seq, n_main), col_q, dw, q_norm_g, k_norm_g, lam, subln_g, 1.0 - lambda_init,
                     TILES['attn'])
    merged = _matmul([(o_a.reshape(t, rw), w_branch_a), (o_b.reshape(t, dw), w_branch_b)],
                     [(z, col_ga // tn), (z, col_gb // tn)], _ep_merge, d, BF16, tm, tn, "branch_merge")
    x2 = _matmul([(merged, w_out)], [(x2, 0)], _ep_residual, d, F32, tm, tn, "out_proj")
    h = _rmsnorm(x2, norm_ffn_g, nr)
    act = _ffn_in(h, w_ffn_in, ffn_conv_w, ffn_conv_b.reshape(1, 2 * dff), dff, seq, tm, TILES['ffn_in_tn'])
    x2 = _matmul([(act, w_ffn_out.astype(BF16))], [(x2, 0)], _ep_residual, d, F32, TILES['ffn_out_tm'],
                 TILES['ffn_out_tn'], "ffn_out")
    h = _rmsnorm(x2, norm_ple_g, nr)
    pe = p_i.reshape(t, -1).astype(BF16)
    x2 = _matmul([(h, w_ple_gate), (pe, w_ple_proj)], [(x2, 0)], _ep_ple, d, F32, tm, tn, "ple_gate")
    return x2.reshape(bsz, seq, d)


def kernel(x, p, norm_mix_g, w_in, rwkv_mu, rwkv_w0, rwkv_w2, rwkv_a0, rwkv_a2, rwkv_g2, rwkv_k_k, rwkv_k_a,
           rwkv_r_k, rwkv_ln_w, rwkv_ln_b, q_norm_g, k_norm_g, lam_q1, lam_k1, lam_q2, lam_k2, subln_g,
           w_branch_a, w_branch_b, w_out, norm_ffn_g, w_ffn_in, ffn_conv_w, ffn_conv_b, w_ffn_out,
           norm_ple_g, w_ple_gate, w_ple_proj):
    depth = p.shape[0]
    for i in range(depth):
        lambda_init = 0.8 - 0.6 * math.exp(-0.3 * i)
        x = _layer(x, p[i], lambda_init, norm_mix_g[i], w_in[i], rwkv_mu[i], rwkv_w0[i], rwkv_w2[i], rwkv_a0[i],
                   rwkv_a2[i], rwkv_g2[i], rwkv_k_k[i], rwkv_k_a[i], rwkv_r_k[i], rwkv_ln_w[i], rwkv_ln_b[i],
                   q_norm_g[i], k_norm_g[i], lam_q1[i], lam_k1[i], lam_q2[i], lam_k2[i], subln_g[i],
                   w_branch_a[i], w_branch_b[i], w_out[i], norm_ffn_g[i], w_ffn_in[i], ffn_conv_w[i],
                   ffn_conv_b[i], w_ffn_out[i], norm_ple_g[i], w_ple_gate[i], w_ple_proj[i])
    return x
```
